```python
import math
import jax
import jax.numpy as jnp
from jax import lax
import numpy as np

D_MODEL = 2048
BATCH = 1
SEQ = 8192
DEPTH = 1

CHUNK = 64
Q_BLOCK = 128
DIFF_HEADS = 8
DIFF_HEAD_DIM = 64
DIFF_V_DIM = 2 * DIFF_HEAD_DIM
DIFF_QK_WIDTH = DIFF_HEADS * 2 * DIFF_HEAD_DIM
DIFF_WIDTH = DIFF_HEADS * DIFF_V_DIM
GDN_HEADS = 8
GDN_HEAD_DIM = 128
GDN_WIDTH = GDN_HEADS * GDN_HEAD_DIM
CONV_WIDTH = 4
REL_BUCKETS = 32
REL_MAX_DISTANCE = 128
N_EXPERTS = 64
TOP_K = 8
N_GROUPS = 8
TOPK_GROUPS = 4
EXPERT_FF = 512
SHARED_FF = 512
ROUTED_SCALE = 2.5
EXPERT_BLOCK = 128
DEEPNORM_ALPHA = (2 * DEPTH) ** 0.25
DEEPNORM_BETA = (8 * DEPTH) ** -0.25
LN_EPS = 1e-5
NORM_EPS = 1e-6
IN_SPLIT = (DIFF_QK_WIDTH, DIFF_QK_WIDTH, DIFF_WIDTH,
            GDN_WIDTH, GDN_WIDTH, GDN_WIDTH, GDN_WIDTH, GDN_HEADS, GDN_HEADS,
            D_MODEL, D_MODEL)
IN_WIDTH = sum(IN_SPLIT)

kernel_name = "chunk_causal_hybrid_diffattn_gdn_moe"


def layer_norm(x, g, b):
    xf = x.astype(jnp.float32)
    mu = jnp.mean(xf, axis=-1, keepdims=True)
    var = jnp.mean(jnp.square(xf - mu), axis=-1, keepdims=True)
    y = (xf - mu) * lax.rsqrt(var + LN_EPS) * g.astype(jnp.float32) + b.astype(jnp.float32)
    return y.astype(x.dtype)


def rms_norm(x, w):
    xf = x.astype(jnp.float32)
    y = xf * lax.rsqrt(jnp.mean(jnp.square(xf), axis=-1, keepdims=True) + NORM_EPS) * w.astype(jnp.float32)
    return y.astype(x.dtype)


def l2_normalize(x):
    return x * lax.rsqrt(jnp.sum(jnp.square(x), axis=-1, keepdims=True) + NORM_EPS)


def t5_bucket(rel):
    nb = REL_BUCKETS // 2
    base = jnp.where(rel > 0, nb, 0)
    n = jnp.abs(rel)
    max_exact = nb // 2
    nf = jnp.maximum(n, 1).astype(jnp.float32)
    large = max_exact + (jnp.log(nf / max_exact) / math.log(REL_MAX_DISTANCE / max_exact)
                         * (nb - max_exact)).astype(jnp.int32)
    large = jnp.minimum(large, nb - 1)
    return base + jnp.where(n < max_exact, n, large)


def causal_conv(x, w):
    T = x.shape[1]
    xp = jnp.pad(x, ((0, 0), (CONV_WIDTH - 1, 0), (0, 0)))
    out = xp[:, 0:T] * w[0]
    for j in range(1, CONV_WIDTH):
        out = out + xp[:, j:j + T] * w[j]
    return out


def diff_attention(q, k, v, lam_vecs, subln_w, rel_table, lambda_init):
    B, T, _ = q.shape
    H, dh, dv = DIFF_HEADS, DIFF_HEAD_DIM, DIFF_V_DIM
    q = q.reshape(B, T, H, 2, dh).transpose(3, 0, 2, 1, 4)
    k = k.reshape(B, T, H, 2, dh).transpose(3, 0, 2, 1, 4)
    v = v.reshape(B, T, H, dv).transpose(0, 2, 1, 3)
    lf = lam_vecs.astype(jnp.float32)
    lam = jnp.exp(jnp.sum(lf[0] * lf[1])) - jnp.exp(jnp.sum(lf[2] * lf[3])) + lambda_init
    n_blocks = T // Q_BLOCK
    qb = q.reshape(2, B, H, n_blocks, Q_BLOCK, dh).transpose(3, 0, 1, 2, 4, 5)
    kpos = jnp.arange(T)
    scale = dh ** -0.5

    def block(args):
        qblk, bi = args
        qpos = bi * Q_BLOCK + jnp.arange(Q_BLOCK)
        visible = (kpos[None, :] // CHUNK) <= (qpos[:, None] // CHUNK)
        bias = rel_table[t5_bucket(kpos[None, :] - qpos[:, None])]
        bias = jnp.transpose(bias, (2, 0, 1)).astype(jnp.float32)
        s = jnp.einsum('mbhqd,mbhkd->mbhqk', qblk, k).astype(jnp.float32) * scale + bias
        p = jax.nn.softmax(jnp.where(visible, s, -jnp.inf), axis=-1)
        a = p[0] - lam * p[1]
        return jnp.einsum('bhqk,bhkd->bhqd', a.astype(v.dtype), v)

    o = lax.map(block, (qb, jnp.arange(n_blocks)))
    o = o.transpose(1, 0, 3, 2, 4).reshape(B, T, H, dv)
    o = rms_norm(o, subln_w) * (1.0 - lambda_init)
    return o.reshape(B, T, DIFF_WIDTH)


def gated_delta_chunked(q, k, v, g, beta):
    B, H, T, dk = q.shape
    dv = v.shape[-1]
    C = CHUNK
    N = T // C
    q = q * (dk ** -0.5)
    q = q.reshape(B, H, N, C, dk)
    k = k.reshape(B, H, N, C, dk)
    v = v.reshape(B, H, N, C, dv)
    beta = beta.reshape(B, H, N, C)
    gc = jnp.cumsum(g.reshape(B, H, N, C), axis=-1)
    tri = jnp.tril(jnp.ones((C, C), dtype=bool))
    strict = jnp.tril(jnp.ones((C, C), dtype=bool), -1)
    decay = jnp.exp(jnp.where(tri, gc[..., :, None] - gc[..., None, :], -jnp.inf))
    kb = k * beta[..., None]
    L = jnp.where(strict, jnp.einsum('bhncd,bhnsd->bhncs', kb, k) * decay, 0.0)
    eye = jnp.broadcast_to(jnp.eye(C, dtype=L.dtype), L.shape)
    Tm = lax.linalg.triangular_solve(L + eye, eye, left_side=True, lower=True, unit_diagonal=True)
    u = jnp.einsum('bhncs,bhnsd->bhncd', Tm, v * beta[..., None])
    w = jnp.einsum('bhncs,bhnsd->bhncd', Tm, kb * jnp.exp(gc)[..., None])
    a_qk = jnp.where(tri, jnp.einsum('bhncd,bhnsd->bhncs', q, k) * decay, 0.0)
    q_dec = q * jnp.exp(gc)[..., None]
    k_dec = k * jnp.exp(gc[..., -1:] - gc)[..., None]
    g_last = jnp.exp(gc[..., -1])

    def step(S, inp):
        u_n, w_n, qd_n, aqk_n, kd_n, gl_n = inp
        v_new = u_n - jnp.einsum('bhck,bhkv->bhcv', w_n, S)
        o_n = jnp.einsum('bhck,bhkv->bhcv', qd_n, S) + jnp.einsum('bhcs,bhsv->bhcv', aqk_n, v_new)
        S = S * gl_n[..., None, None] + jnp.einsum('bhck,bhcv->bhkv', kd_n, v_new)
        return S, o_n

    xs = tuple(jnp.moveaxis(t, 2, 0) for t in (u, w, q_dec, a_qk, k_dec, g_last))
    S0 = jnp.zeros((B, H, dk, dv), jnp.float32)
    _, o = lax.scan(step, S0, xs)
    return jnp.moveaxis(o, 0, 2).reshape(B, H, T, dv)


def hybrid_mixer(x, w_in, conv_w, a_log, dt_bias, gdn_norm_w, diff_lambda, diff_subln_w,
                 rel_table, w_branch_a, w_branch_b, w_out, lambda_init):
    B, T, _ = x.shape
    proj = x @ w_in
    points = np.cumsum(IN_SPLIT)[:-1].tolist()
    dq, dk, dvv, gq, gk, gv, gz, ga, gb, gate_a, gate_b = jnp.split(proj, points, axis=-1)
    ya = diff_attention(dq, dk, dvv, diff_lambda, diff_subln_w, rel_table, lambda_init)
    qkv = jax.nn.silu(causal_conv(jnp.concatenate([gq, gk, gv], axis=-1), conv_w))
    gq, gk, gv = jnp.split(qkv, 3, axis=-1)
    heads = lambda t: t.reshape(B, T, GDN_HEADS, GDN_HEAD_DIM).transpose(0, 2, 1, 3).astype(jnp.float32)
    q = l2_normalize(heads(gq))
    k = l2_normalize(heads(gk))
    v = heads(gv)
    beta = jax.nn.sigmoid(gb.astype(jnp.float32)).transpose(0, 2, 1)
    g = (-jnp.exp(a_log.astype(jnp.float32))
         * jax.nn.softplus(ga.astype(jnp.float32) + dt_bias.astype(jnp.float32))).transpose(0, 2, 1)
    o = gated_delta_chunked(q, k, v, g, beta).transpose(0, 2, 1, 3)
    z = gz.reshape(B, T, GDN_HEADS, GDN_HEAD_DIM).astype(jnp.float32)
    o = rms_norm(o, gdn_norm_w) * jax.nn.silu(z)
    yb = o.reshape(B, T, GDN_WIDTH).astype(x.dtype)
    merged = jax.nn.sigmoid(gate_a) * (ya @ w_branch_a) + jax.nn.sigmoid(gate_b) * (yb @ w_branch_b)
    return merged @ w_out


def moe_ffn(x, w_router, router_bias, w_gate, w_up, w_down, ws_gate, ws_up, ws_down):
    B, T, D = x.shape
    n_tok = B * T
    xf = x.reshape(n_tok, D)
    scores = jax.nn.sigmoid(xf.astype(jnp.float32) @ w_router.astype(jnp.float32))
    choice = scores + router_bias.astype(jnp.float32)
    grp = choice.reshape(n_tok, N_GROUPS, N_EXPERTS // N_GROUPS)
    grp_score = lax.top_k(grp, 2)[0].sum(-1)
    _, gidx = lax.top_k(grp_score, TOPK_GROUPS)
    gmask = jax.nn.one_hot(gidx, N_GROUPS, dtype=jnp.float32).sum(1)
    emask = jnp.repeat(gmask, N_EXPERTS // N_GROUPS, axis=-1) > 0
    _, eidx = lax.top_k(jnp.where(emask, choice, -jnp.inf), TOP_K)
    wsel = jnp.take_along_axis(scores, eidx, axis=-1)
    wsel = wsel / jnp.sum(wsel, axis=-1, keepdims=True) * ROUTED_SCALE
    n_assign = n_tok * TOP_K
    n_blocks = -(-n_assign // EXPERT_BLOCK) + N_EXPERTS
    n_rows = n_blocks * EXPERT_BLOCK
    flat_e = eidx.reshape(-1)
    flat_w = wsel.reshape(-1)
    order = jnp.argsort(flat_e)
    sorted_e = flat_e[order]
    counts = jnp.bincount(flat_e, length=N_EXPERTS)
    starts = jnp.cumsum(counts) - counts
    padded = (counts + EXPERT_BLOCK - 1) // EXPERT_BLOCK * EXPERT_BLOCK
    pad_ends = jnp.cumsum(padded)
    pad_starts = pad_ends - padded
    dest = pad_starts[sorted_e] + jnp.arange(n_assign) - starts[sorted_e]
    row_token = jnp.full((n_rows,), n_tok, jnp.int32).at[dest].set((order // TOP_K).astype(jnp.int32))
    row_gate = jnp.zeros((n_rows,), jnp.float32).at[dest].set(flat_w[order])
    block_expert = jnp.minimum(jnp.searchsorted(pad_ends, jnp.arange(n_blocks) * EXPERT_BLOCK, side='right'),
                               N_EXPERTS - 1)
    x_pad = jnp.concatenate([xf, jnp.zeros((1, D), xf.dtype)], axis=0)

    def block(args):
        tok, gate, e = args
        xb = x_pad[tok]
        h = jax.nn.silu(xb @ w_gate[e]) * (xb @ w_up[e])
        return (h @ w_down[e]) * gate[:, None].astype(xb.dtype)

    y = lax.map(block, (row_token.reshape(n_blocks, EXPERT_BLOCK),
                        row_gate.reshape(n_blocks, EXPERT_BLOCK), block_expert))
    routed = jax.ops.segment_sum(y.reshape(n_rows, D), row_token, num_segments=n_tok + 1)[:n_tok]
    shared = (jax.nn.silu(xf @ ws_gate) * (xf @ ws_up)) @ ws_down
    return (routed + shared).reshape(B, T, D)


def setup_inputs(seed: int = 0) -> dict:
    key = jax.random.key(seed)
    ks = jax.random.split(key, 26)
    L, D = DEPTH, D_MODEL
    f32 = jnp.float32
    nrm = lambda k, shape, s: jax.random.normal(k, shape, f32) * s
    x = nrm(ks[0], (BATCH, SEQ, D), 1.0)
    col_scale = jnp.concatenate([
        jnp.ones((2 * DIFF_QK_WIDTH,), f32), jnp.full((DIFF_WIDTH,), DEEPNORM_BETA, f32),
        jnp.ones((2 * GDN_WIDTH,), f32), jnp.full((GDN_WIDTH,), DEEPNORM_BETA, f32),
        jnp.ones((GDN_WIDTH + 2 * GDN_HEADS + 2 * D,), f32)])
    w_in = nrm(ks[1], (L, D, IN_WIDTH), D ** -0.5) * col_scale
    conv_w = nrm(ks[2], (L, CONV_WIDTH, 3 * GDN_WIDTH), CONV_WIDTH ** -0.5)
    gdn_a_log = jnp.log(jax.random.uniform(ks[3], (L, GDN_HEADS), f32, 1.0, 16.0))
    dt = jnp.exp(jax.random.uniform(ks[4], (L, GDN_HEADS), f32, math.log(1e-3), math.log(1e-1)))
    gdn_dt_bias = dt + jnp.log(-jnp.expm1(-dt))
    gdn_norm_w = 1.0 + nrm(ks[5], (L, GDN_HEAD_DIM), 0.02)
    diff_lambda = nrm(ks[6], (L, 4, DIFF_HEAD_DIM), 0.1)
    diff_subln_w = 1.0 + nrm(ks[7], (L, DIFF_V_DIM), 0.02)
    rel_bias_table = nrm(ks[8], (REL_BUCKETS, DIFF_HEADS), 0.5)
    w_branch_a = nrm(ks[9], (L, DIFF_WIDTH, D), DIFF_WIDTH ** -0.5)
    w_branch_b = nrm(ks[10], (L, GDN_WIDTH, D), GDN_WIDTH ** -0.5)
    w_out = nrm(ks[11], (L, D, D), D ** -0.5 * DEEPNORM_BETA)
    ln1_g = 1.0 + nrm(ks[12], (L, D), 0.02)
    ln1_b = nrm(ks[13], (L, D), 0.02)
    w_router = nrm(ks[14], (L, D, N_EXPERTS), D ** -0.5)
    router_bias = nrm(ks[15], (L, N_EXPERTS), 0.01)
    w_gate = nrm(ks[16], (L, N_EXPERTS, D, EXPERT_FF), D ** -0.5)
    w_up = nrm(ks[17], (L, N_EXPERTS, D, EXPERT_FF), D ** -0.5)
    w_down = nrm(ks[18], (L, N_EXPERTS, EXPERT_FF, D), EXPERT_FF ** -0.5 * DEEPNORM_BETA)
    ws_gate = nrm(ks[19], (L, D, SHARED_FF), D ** -0.5)
    ws_up = nrm(ks[20], (L, D, SHARED_FF), D ** -0.5)
    ws_down = nrm(ks[21], (L, SHARED_FF, D), SHARED_FF ** -0.5 * DEEPNORM_BETA)
    ln2_g = 1.0 + nrm(ks[22], (L, D), 0.02)
    ln2_b = nrm(ks[23], (L, D), 0.02)
    return {"x": x, "w_in": w_in, "conv_w": conv_w, "gdn_a_log": gdn_a_log, "gdn_dt_bias": gdn_dt_bias,
            "gdn_norm_w": gdn_norm_w, "diff_lambda": diff_lambda, "diff_subln_w": diff_subln_w,
            "rel_bias_table": rel_bias_table, "w_branch_a": w_branch_a, "w_branch_b": w_branch_b,
            "w_out": w_out, "ln1_g": ln1_g, "ln1_b": ln1_b, "w_router": w_router,
            "router_bias": router_bias, "w_gate": w_gate, "w_up": w_up, "w_down": w_down,
            "ws_gate": ws_gate, "ws_up": ws_up, "ws_down": ws_down, "ln2_g": ln2_g, "ln2_b": ln2_b}


def reference(x, w_in, conv_w, gdn_a_log, gdn_dt_bias, gdn_norm_w, diff_lambda, diff_subln_w,
              rel_bias_table, w_branch_a, w_branch_b, w_out, ln1_g, ln1_b, w_router, router_bias,
              w_gate, w_up, w_down, ws_gate, ws_up, ws_down, ln2_g, ln2_b):
    for l in range(DEPTH):
        lambda_init = 0.8 - 0.6 * math.exp(-0.3 * l)
        h = hybrid_mixer(x, w_in[l], conv_w[l], gdn_a_log[l], gdn_dt_bias[l], gdn_norm_w[l],
                         diff_lambda[l], diff_subln_w[l], rel_bias_table, w_branch_a[l],
                         w_branch_b[l], w_out[l], lambda_init)
        x = layer_norm(DEEPNORM_ALPHA * x + h, ln1_g[l], ln1_b[l])
        h = moe_ffn(x, w_router[l], router_bias[l], w_gate[l], w_up[l], w_down[l],
                    ws_gate[l], ws_up[l], ws_down[l])
        x = layer_norm(DEEPNORM_ALPHA * x + h, ln2_g[l], ln2_b[l])
    return x
```

```python
import functools
import math

import jax
import jax.numpy as jnp
from jax import lax
from jax.experimental import pallas as pl
from jax.experimental.pallas import tpu as pltpu

F32 = jnp.float32
BF16 = jnp.bfloat16
I32 = jnp.int32

D_MODEL = 2048
CHUNK = 64
DIFF_HEADS = 8
DIFF_HEAD_DIM = 64
GDN_HEADS = 8
GDN_HEAD_DIM = 128
CONV_WIDTH = 4
REL_BUCKETS = 32
REL_MAX_DISTANCE = 128
N_EXPERTS = 64
TOP_K = 8
N_GROUPS = 8
TOPK_GROUPS = 4
EXPERT_FF = 512
ROUTED_SCALE = 2.5
DEPTH = 1
DEEPNORM_ALPHA = (2 * DEPTH) ** 0.25
LN_EPS = 1e-5
NORM_EPS = 1e-6
LAMBDA_INIT = 0.8 - 0.6 * math.exp(-0.3 * 0)

LANES = 128
MIB = 1024 * 1024
NEG_BIG = -1e30
GMM_ROWS = 256


def _params(semantics, vmem_mib):
    return pltpu.CompilerParams(dimension_semantics=semantics, vmem_limit_bytes=vmem_mib * MIB)


def _sigmoid(x):
    return 1.0 / (1.0 + jnp.exp(-x))


def _silu(x):
    return x * _sigmoid(x)


def _dot(a, b):
    return jnp.dot(a, b, preferred_element_type=F32)


def _dot_nt(a, b):
    return lax.dot_general(a, b, (((1,), (1,)), ((), ())), preferred_element_type=F32)


def _proj_kernel(x_ref, w_ref, o_ref, wbf_ref, *, tn):
    @pl.when(pl.program_id(1) == 0)
    def _():
        wbf_ref[...] = w_ref[...].astype(BF16)

    acc = _dot(x_ref[...], wbf_ref[...])
    for j in range(tn // LANES):
        o_ref[j] = acc[:, j * LANES:(j + 1) * LANES].astype(o_ref.dtype)


def _proj(xb, w, blk0, nblk, tn, odt):
    T, D = xb.shape
    tm = min(1024, T)
    return pl.pallas_call(
        functools.partial(_proj_kernel, tn=tn),
        grid=(nblk, T // tm),
        in_specs=[pl.BlockSpec((tm, D), lambda n, m: (m, 0)),
                  pl.BlockSpec((D, tn), lambda n, m: (0, blk0 + n))],
        out_specs=pl.BlockSpec((tn // LANES, tm, LANES), lambda n, m: (n, m, 0)),
        out_shape=jax.ShapeDtypeStruct((nblk * tn // LANES, T, LANES), odt),
        scratch_shapes=[pltpu.VMEM((D, tn), BF16)],
        compiler_params=_params(("arbitrary", "arbitrary"), 40),
        name="proj",
    )(xb, w)


def _attn_kernel(q_ref, k_ref, v_ref, bias_ref, lam_ref, subw_ref, o_ref,
                 m1, l1, a1, m2, l2, a2, *, tq):
    qi = pl.program_id(1)
    q = q_ref[0].astype(F32) * (DIFF_HEAD_DIM ** -0.5)
    lane = lax.broadcasted_iota(I32, q.shape, 1)
    q1 = jnp.where(lane < DIFF_HEAD_DIM, q, 0.0).astype(BF16)
    q2 = jnp.where(lane >= DIFF_HEAD_DIM, q, 0.0).astype(BF16)
    for m, l, a in ((m1, l1, a1), (m2, l2, a2)):
        m[...] = jnp.full(m.shape, NEG_BIG, F32)
        l[...] = jnp.zeros(l.shape, F32)
        a[...] = jnp.zeros(a.shape, F32)

    def chunk(j, bias_idx):
        start = pl.multiple_of(j * tq, tq)
        k = k_ref[0, pl.ds(start, tq), :]
        v = v_ref[0, pl.ds(start, tq), :]
        for qz, m, l, a in ((q1, m1, l1, a1), (q2, m2, l2, a2)):
            s = _dot_nt(qz, k)
            if bias_idx is not None:
                s = s + bias_ref[0, bias_idx]
            m_old = m[...]
            m_new = jnp.maximum(m_old, jnp.max(s, axis=1, keepdims=True))
            alpha = jnp.exp(m_old - m_new)
            p = jnp.exp(s - m_new)
            l[...] = alpha * l[...] + jnp.sum(p, axis=1, keepdims=True)
            a[...] = alpha * a[...] + _dot(p.astype(BF16), v)
            m[...] = m_new

    @pl.loop(0, jnp.maximum(qi - 1, 0))
    def _(j):
        chunk(j, None)

    @pl.when(qi >= 1)
    def _():
        chunk(qi - 1, 1)

    chunk(qi, 0)

    lf = lam_ref[...]
    lam = (jnp.exp(jnp.sum(lf[0:1] * lf[1:2], axis=1, keepdims=True))
           - jnp.exp(jnp.sum(lf[2:3] * lf[3:4], axis=1, keepdims=True)) + LAMBDA_INIT)
    o = a1[...] / l1[...] - lam * (a2[...] / l2[...])
    y = o * lax.rsqrt(jnp.mean(o * o, axis=1, keepdims=True) + NORM_EPS) * subw_ref[...] * (1.0 - LAMBDA_INIT)
    o_ref[...] = y.astype(o_ref.dtype)


def _t5_bucket(rel):
    nb = REL_BUCKETS // 2
    base = jnp.where(rel > 0, nb, 0)
    n = jnp.abs(rel)
    max_exact = nb // 2
    nf = jnp.maximum(n, 1).astype(F32)
    large = max_exact + (jnp.log(nf / max_exact) / math.log(REL_MAX_DISTANCE / max_exact)
                         * (nb - max_exact)).astype(I32)
    large = jnp.minimum(large, nb - 1)
    return base + jnp.where(n < max_exact, n, large)


def _attn_bias_tiles(rel_table, tq):
    qpos = jnp.arange(tq)[:, None]
    kpos = jnp.arange(tq)[None, :]
    far = rel_table[_t5_bucket(jnp.asarray(-(tq + 1)))]
    diag = rel_table[_t5_bucket(kpos - qpos)] - far
    visible = (kpos // CHUNK) <= (qpos // CHUNK)
    diag = jnp.where(visible[:, :, None], diag, NEG_BIG)
    sub = rel_table[_t5_bucket(kpos - qpos - tq)] - far
    return jnp.stack([diag, sub], axis=0).transpose(3, 0, 1, 2).astype(F32)


def _attn(qkv, bias_tiles, lam_vecs, subln_w, tq):
    nb, T, _ = qkv.shape
    H = DIFF_HEADS
    return pl.pallas_call(
        functools.partial(_attn_kernel, tq=tq),
        grid=(H, T // tq),
        in_specs=[pl.BlockSpec((1, tq, LANES), lambda h, i: (h, i, 0)),
                  pl.BlockSpec((1, T, LANES), lambda h, i: (H + h, 0, 0)),
                  pl.BlockSpec((1, T, LANES), lambda h, i: (2 * H + h, 0, 0)),
                  pl.BlockSpec((1, 2, tq, tq), lambda h, i: (h, 0, 0, 0)),
                  pl.BlockSpec((4, DIFF_HEAD_DIM), lambda h, i: (0, 0)),
                  pl.BlockSpec((1, LANES), lambda h, i: (0, 0))],
        out_specs=pl.BlockSpec((tq, LANES), lambda h, i: (i, h)),
        out_shape=jax.ShapeDtypeStruct((T, H * LANES), BF16),
        scratch_shapes=[pltpu.VMEM((tq, 1), F32), pltpu.VMEM((tq, 1), F32), pltpu.VMEM((tq, LANES), F32),
                        pltpu.VMEM((tq, 1), F32), pltpu.VMEM((tq, 1), F32), pltpu.VMEM((tq, LANES), F32)],
        compiler_params=_params(("arbitrary", "arbitrary"), 40),
        name="attn",
    )(qkv, qkv, qkv, bias_tiles, lam_vecs, subln_w.reshape(1, LANES))


def _gdn_kernel(g4_ref, gab_ref, convw_ref, alog_ref, dtb_ref, normw_ref, o_ref,
                s_ref, xbuf_ref):
    C = CHUNK
    H = GDN_HEADS
    n = pl.program_id(0)

    @pl.when(n == 0)
    def _():
        s_ref[...] = jnp.zeros(s_ref.shape, F32)
        xbuf_ref[...] = jnp.zeros(xbuf_ref.shape, F32)

    gab = gab_ref[0]
    sp = gab + dtb_ref[...]
    softplus = jnp.maximum(sp, 0.0) + jnp.log(1.0 + jnp.exp(-jnp.abs(sp)))
    g = -jnp.exp(alog_ref[...]) * softplus
    beta_all = _sigmoid(gab)
    row = lax.broadcasted_iota(I32, (C, LANES), 0)
    gc = g
    for sh in (1, 2, 4, 8, 16, 32):
        gc = gc + jnp.where(row >= sh, pltpu.roll(gc, sh, 0), 0.0)
    gc_t = gc.T

    ri = lax.broadcasted_iota(I32, (C, C), 0)
    ci = lax.broadcasted_iota(I32, (C, C), 1)
    tri = ri >= ci
    strict = ri > ci
    eye = (ri == ci).astype(F32)

    for c in range(3 * H):
        xbuf_ref[c, 8:8 + C, :] = g4_ref[c]

    def conv(c):
        acc = xbuf_ref[c, 8 - (CONV_WIDTH - 1):8 - (CONV_WIDTH - 1) + C, :] * convw_ref[0, c:c + 1, :]
        for j in range(1, CONV_WIDTH):
            lo = 8 - (CONV_WIDTH - 1) + j
            acc = acc + xbuf_ref[c, lo:lo + C, :] * convw_ref[j, c:c + 1, :]
        return _silu(acc)

    def l2n(x):
        return x * lax.rsqrt(jnp.sum(x * x, axis=1, keepdims=True) + NORM_EPS)

    scale = GDN_HEAD_DIM ** -0.5
    for h in range(H):
        q = l2n(conv(h)) * scale
        k = l2n(conv(H + h))
        v = conv(2 * H + h)
        z = g4_ref[3 * H + h]
        beta = beta_all[:, 8 + h:9 + h]
        gcol = gc[:, h:h + 1]
        grow = gc_t[h:h + 1, :]
        glast = gc[C - 1:C, h:h + 1]
        eg = jnp.exp(gcol)
        decay = jnp.exp(jnp.where(tri, gcol - grow, -jnp.inf))
        kb = k * beta
        kbf = k.astype(BF16)
        prod = _dot_nt(jnp.concatenate([kb, q], axis=0).astype(BF16), kbf)
        lmat = jnp.where(strict, prod[:C] * decay, 0.0)
        aqk = jnp.where(tri, prod[C:] * decay, 0.0)
        tm = eye - lmat
        pw = lmat
        for _ in range(5):
            pwb = pw.astype(BF16)
            pw = _dot(pwb, pwb)
            tm = tm + _dot(tm.astype(BF16), pw.astype(BF16))
        rhs = jnp.concatenate([v * beta, kb * eg], axis=1).astype(BF16)
        uw = _dot(tm.astype(BF16), rhs)
        u = uw[:, :LANES]
        w = uw[:, LANES:]
        s_old = s_ref[h]
        sd = _dot(jnp.concatenate([w, q * eg], axis=0).astype(BF16), s_old.astype(BF16))
        v_new = u - sd[:C]
        vnb = v_new.astype(BF16)
        o = sd[C:] + _dot(aqk.astype(BF16), vnb)
        k_dec = (k * jnp.exp(glast - gcol)).astype(BF16)
        s_ref[h] = s_old * jnp.exp(glast) + lax.dot_general(
            k_dec, vnb, (((0,), (0,)), ((), ())), preferred_element_type=F32)
        on = o * lax.rsqrt(jnp.mean(o * o, axis=1, keepdims=True) + NORM_EPS) * normw_ref[...]
        o_ref[:, h * LANES:(h + 1) * LANES] = (on * _silu(z)).astype(o_ref.dtype)

    for c in range(3 * H):
        xbuf_ref[c, 0:8, :] = g4_ref[c, C - 8:C, :]


def _gdn(g4, gab, conv_w, a_log, dt_bias, norm_w):
    nb, T, _ = g4.shape
    H = GDN_HEADS
    C = CHUNK
    pad = lambda v, off: jnp.zeros((1, LANES), F32).at[0, off:off + H].set(v.astype(F32))
    return pl.pallas_call(
        _gdn_kernel,
        grid=(T // C,),
        in_specs=[pl.BlockSpec((4 * H, C, LANES), lambda n: (0, n, 0)),
                  pl.BlockSpec((1, C, LANES), lambda n: (0, n, 0)),
                  pl.BlockSpec((CONV_WIDTH, 3 * H, LANES), lambda n: (0, 0, 0)),
                  pl.BlockSpec((1, LANES), lambda n: (0, 0)),
                  pl.BlockSpec((1, LANES), lambda n: (0, 0)),
                  pl.BlockSpec((1, LANES), lambda n: (0, 0))],
        out_specs=pl.BlockSpec((C, H * LANES), lambda n: (n, 0)),
        out_shape=jax.ShapeDtypeStruct((T, H * LANES), BF16),
        scratch_shapes=[pltpu.VMEM((H, GDN_HEAD_DIM, GDN_HEAD_DIM), F32),
                        pltpu.VMEM((3 * H, 8 + C, LANES), F32)],
        compiler_params=_params(("arbitrary",), 32),
        name="gdn",
    )(g4, gab, conv_w.reshape(CONV_WIDTH, 3 * H, LANES), pad(a_log, 0), pad(dt_bias, 0),
      norm_w.reshape(1, LANES).astype(F32))


def _merge_kernel(ya_ref, yb_ref, x_ref, wa_ref, wb_ref, wga_ref, wgb_ref, o_ref,
                  wa_s, wb_s, wga_s, wgb_s):
    @pl.when(pl.program_id(1) == 0)
    def _():
        wa_s[...] = wa_ref[...].astype(BF16)
        wb_s[...] = wb_ref[...].astype(BF16)
        wga_s[...] = wga_ref[...].astype(BF16)
        wgb_s[...] = wgb_ref[...].astype(BF16)

    x = x_ref[...]
    a = _dot(ya_ref[...], wa_s[...])
    b = _dot(yb_ref[...], wb_s[...])
    ga = _sigmoid(_dot(x, wga_s[...]))
    gb = _sigmoid(_dot(x, wgb_s[...]))
    o_ref[...] = (ga * a + gb * b).astype(o_ref.dtype)


def _merge(ya, yb, xb, w_a, w_b, w_ga, w_gb):
    T, D = xb.shape
    W = ya.shape[1]
    tm = min(512, T)
    tn = 512
    wspec = lambda rows: pl.BlockSpec((rows, tn), lambda n, m: (0, n))
    return pl.pallas_call(
        _merge_kernel,
        grid=(D // tn, T // tm),
        in_specs=[pl.BlockSpec((tm, W), lambda n, m: (m, 0)),
                  pl.BlockSpec((tm, W), lambda n, m: (m, 0)),
                  pl.BlockSpec((tm, D), lambda n, m: (m, 0)),
                  wspec(W), wspec(W), wspec(D), wspec(D)],
        out_specs=pl.BlockSpec((tm, tn), lambda n, m: (m, n)),
        out_shape=jax.ShapeDtypeStruct((T, D), BF16),
        scratch_shapes=[pltpu.VMEM((W, tn), BF16), pltpu.VMEM((W, tn), BF16),
                        pltpu.VMEM((D, tn), BF16), pltpu.VMEM((D, tn), BF16)],
        compiler_params=_params(("arbitrary", "arbitrary"), 48),
        name="merge",
    )(ya, yb, xb, w_a, w_b, w_ga, w_gb)


def _layer_norm(x, g, b):
    mu = jnp.mean(x, axis=1, keepdims=True)
    xc = x - mu
    var = jnp.mean(xc * xc, axis=1, keepdims=True)
    return xc * lax.rsqrt(var + LN_EPS) * g + b


def _outln_kernel(mg_ref, w_ref, x_ref, g_ref, b_ref, o_ref, ob_ref):
    h = _dot(mg_ref[...], w_ref[...])
    y = _layer_norm(DEEPNORM_ALPHA * x_ref[...] + h, g_ref[...], b_ref[...])
    o_ref[...] = y
    ob_ref[...] = y.astype(BF16)


def _outln(merged, w_out_bf, x, g, b):
    T, D = x.shape
    tm = min(512, T)
    row = lambda: pl.BlockSpec((tm, D), lambda m: (m, 0))
    vec = lambda: pl.BlockSpec((1, D), lambda m: (0, 0))
    return pl.pallas_call(
        _outln_kernel,
        grid=(T // tm,),
        in_specs=[row(), pl.BlockSpec((D, D), lambda m: (0, 0)), row(), vec(), vec()],
        out_specs=[row(), row()],
        out_shape=[jax.ShapeDtypeStruct((T, D), F32), jax.ShapeDtypeStruct((T, D), BF16)],
        compiler_params=_params(("arbitrary",), 48),
        name="outln",
    )(merged, w_out_bf, x, g.reshape(1, D), b.reshape(1, D))


def _router_kernel(x_ref, wr_ref, bias_ref, eidx_ref, rank_ref, gate_ref, cnt_ref,
                   run_ref, tri_ref, *, tm):
    G = N_GROUPS
    P = N_EXPERTS // N_GROUPS
    i = pl.program_id(0)

    @pl.when(i == 0)
    def _():
        run_ref[...] = jnp.zeros(run_ref.shape, F32)
        r = lax.broadcasted_iota(I32, (tm, tm), 0)
        c = lax.broadcasted_iota(I32, (tm, tm), 1)
        tri_ref[...] = jnp.where(r < c, 1.0, 0.0).astype(BF16)

    logits = lax.dot_general(wr_ref[...], x_ref[...], (((1,), (1,)), ((), ())),
                             precision=lax.Precision.HIGHEST, preferred_element_type=F32)
    scores = _sigmoid(logits)
    choice = scores + bias_ref[...]
    c3 = choice.reshape(G, P, tm)
    s3 = scores.reshape(G, P, tm)
    j_iota = lax.broadcasted_iota(I32, (G, P, tm), 1)
    g_iota = lax.broadcasted_iota(I32, (G, 1, tm), 0)
    e_iota = lax.broadcasted_iota(I32, (G, P, tm), 0) * P + j_iota
    ninf = -jnp.inf

    def amax3(v, idx, big):
        m = jnp.max(jnp.max(v, axis=1, keepdims=True), axis=0, keepdims=True)
        f = jnp.min(jnp.min(jnp.where(v == m, idx, big), axis=1, keepdims=True), axis=0, keepdims=True)
        return m, f

    m1 = jnp.max(c3, axis=1, keepdims=True)
    f1 = jnp.min(jnp.where(c3 == m1, j_iota, P), axis=1, keepdims=True)
    m2 = jnp.max(jnp.where(j_iota == f1, ninf, c3), axis=1, keepdims=True)
    gsc = m1 + m2
    gsel = jnp.zeros((G, 1, tm), jnp.bool_)
    for _ in range(TOPK_GROUPS):
        _, f = amax3(gsc, g_iota, G)
        hit = g_iota == f
        gsel = jnp.logical_or(gsel, hit)
        gsc = jnp.where(hit, ninf, gsc)
    cm = jnp.where(gsel, c3, ninf)

    picks = []
    wts = []
    msel = jnp.zeros((G, P, tm), F32)
    for _ in range(TOP_K):
        _, f = amax3(cm, e_iota, N_EXPERTS)
        hit = e_iota == f
        wts.append(jnp.sum(jnp.sum(jnp.where(hit, s3, 0.0), axis=1, keepdims=True), axis=0, keepdims=True))
        picks.append(f)
        msel = jnp.where(hit, 1.0, msel)
        cm = jnp.where(hit, ninf, cm)
    wsum = wts[0]
    for w in wts[1:]:
        wsum = wsum + w

    m2d = msel.reshape(N_EXPERTS, tm)
    before = run_ref[...] + _dot(m2d.astype(BF16), tri_ref[...])
    b3 = before.reshape(G, P, tm)
    for k in range(TOP_K):
        hit = e_iota == picks[k]
        rk = jnp.sum(jnp.sum(jnp.where(hit, b3, 0.0), axis=1, keepdims=True), axis=0, keepdims=True)
        eidx_ref[k:k + 1, :] = picks[k].reshape(1, tm)
        rank_ref[k:k + 1, :] = rk.reshape(1, tm).astype(I32)
        gate_ref[k:k + 1, :] = (wts[k] / wsum * ROUTED_SCALE).reshape(1, tm)
    run_ref[...] = run_ref[...] + jnp.sum(m2d, axis=1, keepdims=True)
    cnt_ref[...] = run_ref[...].astype(I32)


def _router(x1, w_router, router_bias):
    T, D = x1.shape
    tm = min(512, T)
    E = N_EXPERTS
    tok = lambda: pl.BlockSpec((TOP_K, tm), lambda i: (0, i))
    return pl.pallas_call(
        functools.partial(_router_kernel, tm=tm),
        grid=(T // tm,),
        in_specs=[pl.BlockSpec((tm, D), lambda i: (i, 0)),
                  pl.BlockSpec((E, D), lambda i: (0, 0)),
                  pl.BlockSpec((E, 1), lambda i: (0, 0))],
        out_specs=[tok(), tok(), tok(), pl.BlockSpec((E, 1), lambda i: (0, 0))],
        out_shape=[jax.ShapeDtypeStruct((TOP_K, T), I32), jax.ShapeDtypeStruct((TOP_K, T), I32),
                   jax.ShapeDtypeStruct((TOP_K, T), F32), jax.ShapeDtypeStruct((E, 1), I32)],
        scratch_shapes=[pltpu.VMEM((E, 1), F32), pltpu.VMEM((tm, tm), BF16)],
        compiler_params=_params(("arbitrary",), 32),
        name="router",
    )(x1, w_router.astype(F32).T, router_bias.astype(F32).reshape(E, 1))


def _dispatch_kernel(dest_ref, x_hbm, xs_hbm, sem, *, tm):
    i = pl.program_id(0)

    def row_copy(t, k):
        return pltpu.make_async_copy(x_hbm.at[pl.ds(i * tm + t, 1)],
                                     xs_hbm.at[pl.ds(dest_ref[k, t], 1)], sem)

    def issue(t, carry):
        for k in range(TOP_K):
            row_copy(t, k).start()
        return carry

    def drain(t, carry):
        for k in range(TOP_K):
            row_copy(t, k).wait()
        return carry

    lax.fori_loop(0, tm, issue, 0)
    lax.fori_loop(0, tm, drain, 0)


def _dispatch(dest, x1):
    T, D = x1.shape
    tm = min(128, T)
    return pl.pallas_call(
        functools.partial(_dispatch_kernel, tm=tm),
        grid=(T // tm,),
        in_specs=[pl.BlockSpec((TOP_K, tm), lambda i: (0, i), memory_space=pltpu.SMEM),
                  pl.BlockSpec(memory_space=pl.ANY)],
        out_specs=pl.BlockSpec(memory_space=pl.ANY),
        out_shape=jax.ShapeDtypeStruct((T * TOP_K, D), x1.dtype),
        scratch_shapes=[pltpu.SemaphoreType.DMA(())],
        compiler_params=_params(("arbitrary",), 16),
        name="dispatch",
    )(dest, x1)


def _gmm_kernel(vblk, vexp, vfirst, gstart, nvis, xs_ref, wg_ref, wu_ref, wd_ref, o_ref,
                wg_s, wu_s, wd_s):
    i = pl.program_id(0)
    e = vexp[i]
    e_prev = vexp[jnp.maximum(i - 1, 0)]

    @pl.when(jnp.logical_or(i == 0, e != e_prev))
    def _():
        wg_s[...] = wg_ref[...].astype(BF16)
        wu_s[...] = wu_ref[...].astype(BF16)
        wd_s[...] = wd_ref[...].astype(BF16)

    @pl.when(i < nvis[0])
    def _():
        x = xs_ref[...].astype(BF16)
        g = _dot(x, wg_s[...])
        u = _dot(x, wu_s[...])
        y = _dot((_silu(g) * u).astype(BF16), wd_s[...])
        row = vblk[i] * GMM_ROWS + lax.broadcasted_iota(I32, (GMM_ROWS, 1), 0)
        mine = jnp.logical_and(row >= gstart[e], row < gstart[e + 1])

        @pl.when(vfirst[i] == 1)
        def _():
            o_ref[...] = jnp.where(mine, y, 0.0)

        @pl.when(vfirst[i] == 0)
        def _():
            o_ref[...] = jnp.where(mine, y, o_ref[...])


def _gmm_plan(counts, n_rows):
    E = N_EXPERTS
    nblk = n_rows // GMM_ROWS
    nvis_max = nblk + E
    ends = jnp.cumsum(counts)
    gstart = jnp.concatenate([jnp.zeros((1,), I32), ends]).astype(I32)
    blk_lo = jnp.arange(nblk, dtype=I32) * GMM_ROWS
    e_lo = jnp.searchsorted(ends, blk_lo, side='right').astype(I32)
    e_hi = jnp.searchsorted(ends, blk_lo + (GMM_ROWS - 1), side='right').astype(I32)
    per_blk = e_hi - e_lo + 1
    vend = jnp.cumsum(per_blk).astype(I32)
    vbeg = vend - per_blk
    nvis = vend[-1]
    vi = jnp.minimum(jnp.arange(nvis_max, dtype=I32), nvis - 1)
    vblk = jnp.searchsorted(vend, vi, side='right').astype(I32)
    vexp = e_lo[vblk] + (vi - vbeg[vblk])
    vfirst = jnp.logical_and(vi == vbeg[vblk], jnp.arange(nvis_max) < nvis).astype(I32)
    return vblk, vexp, vfirst, gstart, nvis.reshape(1).astype(I32), nvis_max


def _gmm(xs, counts, w_gate, w_up, w_down):
    R, D = xs.shape
    FF = EXPERT_FF
    vblk, vexp, vfirst, gstart, nvis, nvis_max = _gmm_plan(counts, R)
    grid_spec = pltpu.PrefetchScalarGridSpec(
        num_scalar_prefetch=5,
        grid=(nvis_max,),
        in_specs=[pl.BlockSpec((GMM_ROWS, D), lambda i, vb, ve, vf, gs, nv: (vb[i], 0)),
                  pl.BlockSpec((None, None, D, FF), lambda i, vb, ve, vf, gs, nv: (0, ve[i], 0, 0)),
                  pl.BlockSpec((None, None, D, FF), lambda i, vb, ve, vf, gs, nv: (0, ve[i], 0, 0)),
                  pl.BlockSpec((None, None, FF, D), lambda i, vb, ve, vf, gs, nv: (0, ve[i], 0, 0))],
        out_specs=pl.BlockSpec((GMM_ROWS, D), lambda i, vb, ve, vf, gs, nv: (vb[i], 0)),
        scratch_shapes=[pltpu.VMEM((D, FF), BF16), pltpu.VMEM((D, FF), BF16), pltpu.VMEM((FF, D), BF16)],
    )
    return pl.pallas_call(
        _gmm_kernel,
        grid_spec=grid_spec,
        out_shape=jax.ShapeDtypeStruct((R, D), F32),
        compiler_params=_params(("arbitrary",), 56),
        name="gmm",
    )(vblk, vexp, vfirst, gstart, nvis, xs, w_gate, w_up, w_down)


def _combine_kernel(dest_ref, ys_hbm, gate_ref, x_ref, xb_ref, wsg_ref, wsu_ref, wsd_ref, g_ref, b_ref,
                    o_ref, buf_ref, sem, *, tm):
    def row_copy(t, k):
        return pltpu.make_async_copy(ys_hbm.at[pl.ds(dest_ref[k, t], 1)],
                                     buf_ref.at[k, pl.ds(t, 1)], sem)

    def issue(t, carry):
        for k in range(TOP_K):
            row_copy(t, k).start()
        return carry

    def drain(t, carry):
        for k in range(TOP_K):
            row_copy(t, k).wait()
        return carry

    lax.fori_loop(0, tm, issue, 0)
    xb = xb_ref[...]
    hidden = _silu(_dot(xb, wsg_ref[...])) * _dot(xb, wsu_ref[...])
    acc = DEEPNORM_ALPHA * x_ref[...] + _dot(hidden.astype(BF16), wsd_ref[...])
    lax.fori_loop(0, tm, drain, 0)
    gates = gate_ref[...]
    for k in range(TOP_K):
        acc = acc + gates[:, k:k + 1] * buf_ref[k]
    o_ref[...] = _layer_norm(acc, g_ref[...], b_ref[...])


def _combine(dest, ys, gates_t, x1, x1b, wsg, wsu, wsd, g, b):
    T, D = x1.shape
    tm = min(128, T)
    FF = wsg.shape[1]
    row = lambda: pl.BlockSpec((tm, D), lambda i: (i, 0))
    vec = lambda: pl.BlockSpec((1, D), lambda i: (0, 0))
    return pl.pallas_call(
        functools.partial(_combine_kernel, tm=tm),
        grid=(T // tm,),
        in_specs=[pl.BlockSpec((TOP_K, tm), lambda i: (0, i), memory_space=pltpu.SMEM),
                  pl.BlockSpec(memory_space=pl.ANY),
                  pl.BlockSpec((tm, TOP_K), lambda i: (i, 0)),
                  row(), row(),
                  pl.BlockSpec((D, FF), lambda i: (0, 0)),
                  pl.BlockSpec((D, FF), lambda i: (0, 0)),
                  pl.BlockSpec((FF, D), lambda i: (0, 0)),
                  vec(), vec()],
        out_specs=row(),
        out_shape=jax.ShapeDtypeStruct((T, D), F32),
        scratch_shapes=[pltpu.VMEM((TOP_K, tm, D), F32), pltpu.SemaphoreType.DMA(())],
        compiler_params=_params(("arbitrary",), 40),
        name="combine",
    )(dest, ys, gates_t, x1, x1b, wsg, wsu, wsd, g.reshape(1, D), b.reshape(1, D))


def kernel(x, w_in, conv_w, gdn_a_log, gdn_dt_bias, gdn_norm_w, diff_lambda, diff_subln_w, rel_bias_table,
           w_branch_a, w_branch_b, w_out, ln1_g, ln1_b, w_router, router_bias, w_gate, w_up, w_down,
           ws_gate, ws_up, ws_down, ln2_g, ln2_b):
    B, T, D = x.shape
    assert B == 1 and D == D_MODEL
    x2 = x.reshape(T, D)
    xb = x2.astype(BF16)
    w_in0 = w_in[0]
    H = DIFF_HEADS

    qkv = _proj(xb, w_in0, 0, 6, 512, BF16)
    g4 = _proj(xb, w_in0, 6, 8, 512, F32)
    gab = _proj(xb, w_in0, 56, 1, LANES, F32)
    gate_col = 7 * 1024 + 2 * GDN_HEADS
    w_ga = w_in0[:, gate_col:gate_col + D]
    w_gb = w_in0[:, gate_col + D:gate_col + 2 * D]

    tq = min(512, T)
    ya = _attn(qkv, _attn_bias_tiles(rel_bias_table.astype(F32), tq), diff_lambda[0].astype(F32),
               diff_subln_w[0].astype(F32), tq)
    yb = _gdn(g4, gab, conv_w[0].astype(F32), gdn_a_log[0], gdn_dt_bias[0], gdn_norm_w[0])
    merged = _merge(ya, yb, xb, w_branch_a[0], w_branch_b[0], w_ga, w_gb)
    x1, x1b = _outln(merged, w_out[0].astype(BF16), x2, ln1_g[0], ln1_b[0])

    eidx, rank, gates, counts = _router(x1, w_router[0], router_bias[0])
    counts = counts.reshape(N_EXPERTS)
    starts = jnp.cumsum(counts) - counts
    dest = starts[eidx] + rank
    xs = _dispatch(dest, x1)
    ys = _gmm(xs, counts, w_gate, w_up, w_down)
    out = _combine(dest, ys, gates.T, x1, x1b, ws_gate[0].astype(BF16), ws_up[0].astype(BF16),
                   ws_down[0].astype(BF16), ln2_g[0], ln2_b[0])
    return out.reshape(B, T, D)
```

```python
import functools
import math

import jax
import jax.numpy as jnp
from jax import lax
from jax.experimental import pallas as pl
from jax.experimental.pallas import tpu as pltpu

F32 = jnp.float32
BF16 = jnp.bfloat16
I32 = jnp.int32

D_MODEL = 2048
CHUNK = 64
DIFF_HEADS = 8
DIFF_HEAD_DIM = 64
GDN_HEADS = 8
GDN_HEAD_DIM = 128
CONV_WIDTH = 4
REL_BUCKETS = 32
REL_MAX_DISTANCE = 128
N_EXPERTS = 64
TOP_K = 8
N_GROUPS = 8
TOPK_GROUPS = 4
EXPERT_FF = 512
ROUTED_SCALE = 2.5
DEPTH = 1
DEEPNORM_ALPHA = (2 * DEPTH) ** 0.25
LN_EPS = 1e-5
NORM_EPS = 1e-6
LAMBDA_INIT = 0.8 - 0.6 * math.exp(-0.3 * 0)

LANES = 128
MIB = 1024 * 1024
NEG_BIG = -1e30
GMM_ROWS = 256


def _params(semantics, vmem_mib):
    return pltpu.CompilerParams(dimension_semantics=semantics, vmem_limit_bytes=vmem_mib * MIB)


def _sigmoid(x):
    return 1.0 / (1.0 + jnp.exp(-x))


def _silu(x):
    return x * _sigmoid(x)


def _dot(a, b):
    return jnp.dot(a, b, preferred_element_type=F32)


def _dot_nt(a, b):
    return lax.dot_general(a, b, (((1,), (1,)), ((), ())), preferred_element_type=F32)


def _proj_kernel(x_ref, w_ref, o_ref, wbf_ref, *, tn):
    @pl.when(pl.program_id(1) == 0)
    def _():
        wbf_ref[...] = w_ref[...].astype(BF16)

    acc = _dot(x_ref[...], wbf_ref[...])
    for j in range(tn // LANES):
        o_ref[j] = acc[:, j * LANES:(j + 1) * LANES].astype(o_ref.dtype)


def _proj(xb, w, blk0, nblk, tn, odt):
    T, D = xb.shape
    tm = min(1024, T)
    return pl.pallas_call(
        functools.partial(_proj_kernel, tn=tn),
        grid=(nblk, T // tm),
        in_specs=[pl.BlockSpec((tm, D), lambda n, m: (m, 0)),
                  pl.BlockSpec((D, tn), lambda n, m: (0, blk0 + n))],
        out_specs=pl.BlockSpec((tn // LANES, tm, LANES), lambda n, m: (n, m, 0)),
        out_shape=jax.ShapeDtypeStruct((nblk * tn // LANES, T, LANES), odt),
        scratch_shapes=[pltpu.VMEM((D, tn), BF16)],
        compiler_params=_params(("arbitrary", "arbitrary"), 40),
        name="proj",
    )(xb, w)


def _attn_kernel(q_ref, k_ref, v_ref, bias_ref, lam_ref, subw_ref, o_ref,
                 m1, l1, a1, m2, l2, a2, *, tq):
    qi = pl.program_id(1)
    q = q_ref[0].astype(F32) * (DIFF_HEAD_DIM ** -0.5)
    lane = lax.broadcasted_iota(I32, q.shape, 1)
    q1 = jnp.where(lane < DIFF_HEAD_DIM, q, 0.0).astype(BF16)
    q2 = jnp.where(lane >= DIFF_HEAD_DIM, q, 0.0).astype(BF16)
    for m, l, a in ((m1, l1, a1), (m2, l2, a2)):
        m[...] = jnp.full(m.shape, NEG_BIG, F32)
        l[...] = jnp.zeros(l.shape, F32)
        a[...] = jnp.zeros(a.shape, F32)

    def chunk(j, bias_idx):
        start = pl.multiple_of(j * tq, tq)
        k = k_ref[0, pl.ds(start, tq), :]
        v = v_ref[0, pl.ds(start, tq), :]
        maps = (0, 1)
        ms, ls, accs = (m1, m2), (l1, l2), (a1, a2)
        s = [_dot_nt(qz, k) for qz in (q1, q2)]
        if bias_idx is not None:
            bias = bias_ref[0, bias_idx]
            s = [s[i] + bias for i in maps]
        m_old = [ms[i][...] for i in maps]
        m_new = [jnp.maximum(m_old[i], jnp.max(s[i], axis=1, keepdims=True)) for i in maps]
        alpha = [jnp.exp(m_old[i] - m_new[i]) for i in maps]
        p = [jnp.exp(s[i] - jnp.concatenate([m_new[i]] * (tq // LANES), axis=1)) for i in maps]
        pv = [_dot(p[i].astype(BF16), v) for i in maps]
        for i in maps:
            ls[i][...] = alpha[i] * ls[i][...] + jnp.sum(p[i], axis=1, keepdims=True)
            accs[i][...] = alpha[i] * accs[i][...] + pv[i]
            ms[i][...] = m_new[i]

    @pl.loop(0, jnp.maximum(qi - 1, 0))
    def _(j):
        chunk(j, None)

    @pl.when(qi >= 1)
    def _():
        chunk(qi - 1, 1)

    chunk(qi, 0)

    lf = lam_ref[...]
    lam = (jnp.exp(jnp.sum(lf[0:1] * lf[1:2], axis=1, keepdims=True))
           - jnp.exp(jnp.sum(lf[2:3] * lf[3:4], axis=1, keepdims=True)) + LAMBDA_INIT)
    o = a1[...] / l1[...] - lam * (a2[...] / l2[...])
    y = o * lax.rsqrt(jnp.mean(o * o, axis=1, keepdims=True) + NORM_EPS) * subw_ref[...] * (1.0 - LAMBDA_INIT)
    o_ref[...] = y.astype(o_ref.dtype)


def _t5_bucket(rel):
    nb = REL_BUCKETS // 2
    base = jnp.where(rel > 0, nb, 0)
    n = jnp.abs(rel)
    max_exact = nb // 2
    nf = jnp.maximum(n, 1).astype(F32)
    large = max_exact + (jnp.log(nf / max_exact) / math.log(REL_MAX_DISTANCE / max_exact)
                         * (nb - max_exact)).astype(I32)
    large = jnp.minimum(large, nb - 1)
    return base + jnp.where(n < max_exact, n, large)


def _attn_bias_tiles(rel_table, tq):
    H = rel_table.shape[1]
    span = 3 * tq
    rel = jnp.arange(span) - (2 * tq - 1)
    onehot = (_t5_bucket(rel)[:, None] == jnp.arange(REL_BUCKETS)[None, :]).astype(F32)
    far = rel_table[REL_BUCKETS // 2 - 1]
    band = (jnp.dot(onehot, rel_table, precision=lax.Precision.HIGHEST) - far).T
    skew = jnp.tile(band, (1, tq + 1))[:, :tq * (span + 1)].reshape(H, tq, span + 1)
    skew = skew[:, ::-1, :]
    sub = skew[:, :, :tq]
    diag = skew[:, :, tq:2 * tq]
    qpos = jnp.arange(tq)[:, None]
    kpos = jnp.arange(tq)[None, :]
    visible = (kpos // CHUNK) <= (qpos // CHUNK)
    diag = jnp.where(visible[None], diag, NEG_BIG)
    return jnp.stack([diag, sub], axis=1).astype(F32)


def _attn(qkv, bias_tiles, lam_vecs, subln_w, tq):
    nb, T, _ = qkv.shape
    H = DIFF_HEADS
    return pl.pallas_call(
        functools.partial(_attn_kernel, tq=tq),
        grid=(H, T // tq),
        in_specs=[pl.BlockSpec((1, tq, LANES), lambda h, i: (h, i, 0)),
                  pl.BlockSpec((1, T, LANES), lambda h, i: (H + h, 0, 0)),
                  pl.BlockSpec((1, T, LANES), lambda h, i: (2 * H + h, 0, 0)),
                  pl.BlockSpec((1, 2, tq, tq), lambda h, i: (h, 0, 0, 0)),
                  pl.BlockSpec((4, DIFF_HEAD_DIM), lambda h, i: (0, 0)),
                  pl.BlockSpec((1, LANES), lambda h, i: (0, 0))],
        out_specs=pl.BlockSpec((tq, LANES), lambda h, i: (i, h)),
        out_shape=jax.ShapeDtypeStruct((T, H * LANES), BF16),
        scratch_shapes=[pltpu.VMEM((tq, LANES), F32)] * 6,
        compiler_params=_params(("arbitrary", "arbitrary"), 40),
        name="attn",
    )(qkv, qkv, qkv, bias_tiles, lam_vecs, subln_w.reshape(1, LANES))


def _gdn_kernel(g4_ref, gab_ref, convw_ref, alog_ref, dtb_ref, normw_ref, o_ref,
                s_ref, xbuf_ref):
    C = CHUNK
    H = GDN_HEADS
    n = pl.program_id(0)

    @pl.when(n == 0)
    def _():
        s_ref[...] = jnp.zeros(s_ref.shape, F32)
        xbuf_ref[...] = jnp.zeros(xbuf_ref.shape, F32)

    gab = gab_ref[0]
    sp = gab + dtb_ref[...]
    softplus = jnp.maximum(sp, 0.0) + jnp.log(1.0 + jnp.exp(-jnp.abs(sp)))
    g = -jnp.exp(alog_ref[...]) * softplus
    beta_all = _sigmoid(gab)
    row = lax.broadcasted_iota(I32, (C, LANES), 0)
    gc = g
    for sh in (1, 2, 4, 8, 16, 32):
        gc = gc + jnp.where(row >= sh, pltpu.roll(gc, sh, 0), 0.0)
    gc_t = gc.T

    ri = lax.broadcasted_iota(I32, (C, C), 0)
    ci = lax.broadcasted_iota(I32, (C, C), 1)
    tri = ri >= ci
    strict = ri > ci
    eye = (ri == ci).astype(F32)

    for c in range(3 * H):
        xbuf_ref[c, 8:8 + C, :] = g4_ref[c]

    def conv(c):
        acc = xbuf_ref[c, 8 - (CONV_WIDTH - 1):8 - (CONV_WIDTH - 1) + C, :] * convw_ref[0, c:c + 1, :]
        for j in range(1, CONV_WIDTH):
            lo = 8 - (CONV_WIDTH - 1) + j
            acc = acc + xbuf_ref[c, lo:lo + C, :] * convw_ref[j, c:c + 1, :]
        return _silu(acc)

    def l2n(x):
        return x * lax.rsqrt(jnp.sum(x * x, axis=1, keepdims=True) + NORM_EPS)

    scale = GDN_HEAD_DIM ** -0.5
    heads = range(H)
    q = [l2n(conv(h)) * scale for h in heads]
    k = [l2n(conv(H + h)) for h in heads]
    v = [conv(2 * H + h) for h in heads]
    beta = [beta_all[:, 8 + h:9 + h] for h in heads]
    gcol = [gc[:, h:h + 1] for h in heads]
    glast = [gc[C - 1:C, h:h + 1] for h in heads]
    eg = [jnp.exp(gcol[h]) for h in heads]
    decay = [jnp.exp(jnp.where(tri, gcol[h] - gc_t[h:h + 1, :], -jnp.inf)) for h in heads]
    kb = [k[h] * beta[h] for h in heads]
    prod = [_dot_nt(jnp.concatenate([kb[h], q[h]], axis=0).astype(BF16), k[h].astype(BF16))
            for h in heads]
    lmat = [jnp.where(strict, prod[h][:C] * decay[h], 0.0) for h in heads]
    aqk = [jnp.where(tri, prod[h][C:] * decay[h], 0.0).astype(BF16) for h in heads]
    tm = [eye - lmat[h] for h in heads]
    pw = lmat
    for _ in range(5):
        pwb = [pw[h].astype(BF16) for h in heads]
        pw = [_dot(pwb[h], pwb[h]) for h in heads]
        tm = [tm[h] + _dot(tm[h].astype(BF16), pw[h].astype(BF16)) for h in heads]
    uw = [_dot(tm[h].astype(BF16),
               jnp.concatenate([v[h] * beta[h], kb[h] * eg[h]], axis=1).astype(BF16)) for h in heads]
    s_old = [s_ref[h] for h in heads]
    sd = [_dot(jnp.concatenate([uw[h][:, LANES:], q[h] * eg[h]], axis=0).astype(BF16), s_old[h].astype(BF16))
          for h in heads]
    vnb = [(uw[h][:, :LANES] - sd[h][:C]).astype(BF16) for h in heads]
    o = [sd[h][C:] + _dot(aqk[h], vnb[h]) for h in heads]
    for h in heads:
        k_dec = (k[h] * jnp.exp(glast[h] - gcol[h])).astype(BF16)
        s_ref[h] = s_old[h] * jnp.exp(glast[h]) + lax.dot_general(
            k_dec, vnb[h], (((0,), (0,)), ((), ())), preferred_element_type=F32)
    for h in heads:
        on = o[h] * lax.rsqrt(jnp.mean(o[h] * o[h], axis=1, keepdims=True) + NORM_EPS) * normw_ref[...]
        o_ref[:, h * LANES:(h + 1) * LANES] = (on * _silu(g4_ref[3 * H + h])).astype(o_ref.dtype)

    for c in range(3 * H):
        xbuf_ref[c, 0:8, :] = g4_ref[c, C - 8:C, :]


def _gdn(g4, gab, conv_w, a_log, dt_bias, norm_w):
    nb, T, _ = g4.shape
    H = GDN_HEADS
    C = CHUNK
    pad = lambda v, off: jnp.zeros((1, LANES), F32).at[0, off:off + H].set(v.astype(F32))
    return pl.pallas_call(
        _gdn_kernel,
        grid=(T // C,),
        in_specs=[pl.BlockSpec((4 * H, C, LANES), lambda n: (0, n, 0)),
                  pl.BlockSpec((1, C, LANES), lambda n: (0, n, 0)),
                  pl.BlockSpec((CONV_WIDTH, 3 * H, LANES), lambda n: (0, 0, 0)),
                  pl.BlockSpec((1, LANES), lambda n: (0, 0)),
                  pl.BlockSpec((1, LANES), lambda n: (0, 0)),
                  pl.BlockSpec((1, LANES), lambda n: (0, 0))],
        out_specs=pl.BlockSpec((C, H * LANES), lambda n: (n, 0)),
        out_shape=jax.ShapeDtypeStruct((T, H * LANES), BF16),
        scratch_shapes=[pltpu.VMEM((H, GDN_HEAD_DIM, GDN_HEAD_DIM), F32),
                        pltpu.VMEM((3 * H, 8 + C, LANES), F32)],
        compiler_params=_params(("arbitrary",), 32),
        name="gdn",
    )(g4, gab, conv_w.reshape(CONV_WIDTH, 3 * H, LANES), pad(a_log, 0), pad(dt_bias, 0),
      norm_w.reshape(1, LANES).astype(F32))


def _merge_kernel(ya_ref, yb_ref, x_ref, wa_ref, wb_ref, wga_ref, wgb_ref, o_ref,
                  wa_s, wb_s, wga_s, wgb_s):
    @pl.when(pl.program_id(1) == 0)
    def _():
        wa_s[...] = wa_ref[...].astype(BF16)
        wb_s[...] = wb_ref[...].astype(BF16)
        wga_s[...] = wga_ref[...].astype(BF16)
        wgb_s[...] = wgb_ref[...].astype(BF16)

    x = x_ref[...]
    a = _dot(ya_ref[...], wa_s[...])
    b = _dot(yb_ref[...], wb_s[...])
    ga = _sigmoid(_dot(x, wga_s[...]))
    gb = _sigmoid(_dot(x, wgb_s[...]))
    o_ref[...] = (ga * a + gb * b).astype(o_ref.dtype)


def _merge(ya, yb, xb, w_a, w_b, w_ga, w_gb):
    T, D = xb.shape
    W = ya.shape[1]
    tm = min(512, T)
    tn = 512
    wspec = lambda rows: pl.BlockSpec((rows, tn), lambda n, m: (0, n))
    return pl.pallas_call(
        _merge_kernel,
        grid=(D // tn, T // tm),
        in_specs=[pl.BlockSpec((tm, W), lambda n, m: (m, 0)),
                  pl.BlockSpec((tm, W), lambda n, m: (m, 0)),
                  pl.BlockSpec((tm, D), lambda n, m: (m, 0)),
                  wspec(W), wspec(W), wspec(D), wspec(D)],
        out_specs=pl.BlockSpec((tm, tn), lambda n, m: (m, n)),
        out_shape=jax.ShapeDtypeStruct((T, D), BF16),
        scratch_shapes=[pltpu.VMEM((W, tn), BF16), pltpu.VMEM((W, tn), BF16),
                        pltpu.VMEM((D, tn), BF16), pltpu.VMEM((D, tn), BF16)],
        compiler_params=_params(("arbitrary", "arbitrary"), 48),
        name="merge",
    )(ya, yb, xb, w_a, w_b, w_ga, w_gb)


def _layer_norm(x, g, b):
    mu = jnp.mean(x, axis=1, keepdims=True)
    xc = x - mu
    var = jnp.mean(xc * xc, axis=1, keepdims=True)
    return xc * lax.rsqrt(var + LN_EPS) * g + b


def _outln_kernel(mg_ref, w_ref, x_ref, g_ref, b_ref, o_ref, ob_ref):
    h = _dot(mg_ref[...], w_ref[...])
    y = _layer_norm(DEEPNORM_ALPHA * x_ref[...] + h, g_ref[...], b_ref[...])
    o_ref[...] = y
    ob_ref[...] = y.astype(BF16)


def _outln(merged, w_out_bf, x, g, b):
    T, D = x.shape
    tm = min(512, T)
    row = lambda: pl.BlockSpec((tm, D), lambda m: (m, 0))
    vec = lambda: pl.BlockSpec((1, D), lambda m: (0, 0))
    return pl.pallas_call(
        _outln_kernel,
        grid=(T // tm,),
        in_specs=[row(), pl.BlockSpec((D, D), lambda m: (0, 0)), row(), vec(), vec()],
        out_specs=[row(), row()],
        out_shape=[jax.ShapeDtypeStruct((T, D), F32), jax.ShapeDtypeStruct((T, D), BF16)],
        compiler_params=_params(("arbitrary",), 48),
        name="outln",
    )(merged, w_out_bf, x, g.reshape(1, D), b.reshape(1, D))


def _router_kernel(x_ref, wr_ref, bias_ref, eidx_ref, rank_ref, gate_ref, cnt_ref,
                   run_ref, tri_ref, *, tm):
    G = N_GROUPS
    P = N_EXPERTS // N_GROUPS
    i = pl.program_id(0)

    @pl.when(i == 0)
    def _():
        run_ref[...] = jnp.zeros(run_ref.shape, F32)
        r = lax.broadcasted_iota(I32, (tm, tm), 0)
        c = lax.broadcasted_iota(I32, (tm, tm), 1)
        tri_ref[...] = jnp.where(r < c, 1.0, 0.0).astype(BF16)

    logits = lax.dot_general(wr_ref[...], x_ref[...], (((1,), (1,)), ((), ())),
                             precision=lax.Precision.HIGHEST, preferred_element_type=F32)
    scores = _sigmoid(logits)
    choice = scores + bias_ref[...]
    c3 = choice.reshape(G, P, tm)
    s3 = scores.reshape(G, P, tm)
    j_iota = lax.broadcasted_iota(I32, (G, P, tm), 1)
    g_iota = lax.broadcasted_iota(I32, (G, 1, tm), 0)
    e_iota = lax.broadcasted_iota(I32, (G, P, tm), 0) * P + j_iota
    ninf = -jnp.inf

    def amax3(v, idx, big):
        m = jnp.max(jnp.max(v, axis=1, keepdims=True), axis=0, keepdims=True)
        f = jnp.min(jnp.min(jnp.where(v == m, idx, big), axis=1, keepdims=True), axis=0, keepdims=True)
        return m, f

    m1 = jnp.max(c3, axis=1, keepdims=True)
    f1 = jnp.min(jnp.where(c3 == m1, j_iota, P), axis=1, keepdims=True)
    m2 = jnp.max(jnp.where(j_iota == f1, ninf, c3), axis=1, keepdims=True)
    gsc = m1 + m2
    gsel = jnp.zeros((G, 1, tm), jnp.bool_)
    for _ in range(TOPK_GROUPS):
        _, f = amax3(gsc, g_iota, G)
        hit = g_iota == f
        gsel = jnp.logical_or(gsel, hit)
        gsc = jnp.where(hit, ninf, gsc)
    cm = jnp.where(gsel, c3, ninf)

    picks = []
    wts = []
    msel = jnp.zeros((G, P, tm), F32)
    for _ in range(TOP_K):
        _, f = amax3(cm, e_iota, N_EXPERTS)
        hit = e_iota == f
        wts.append(jnp.sum(jnp.sum(jnp.where(hit, s3, 0.0), axis=1, keepdims=True), axis=0, keepdims=True))
        picks.append(f)
        msel = jnp.where(hit, 1.0, msel)
        cm = jnp.where(hit, ninf, cm)
    wsum = wts[0]
    for w in wts[1:]:
        wsum = wsum + w

    m2d = msel.reshape(N_EXPERTS, tm)
    before = run_ref[...] + _dot(m2d.astype(BF16), tri_ref[...])
    b3 = before.reshape(G, P, tm)
    for k in range(TOP_K):
        hit = e_iota == picks[k]
        rk = jnp.sum(jnp.sum(jnp.where(hit, b3, 0.0), axis=1, keepdims=True), axis=0, keepdims=True)
        eidx_ref[k:k + 1, :] = picks[k].reshape(1, tm)
        rank_ref[k:k + 1, :] = rk.reshape(1, tm).astype(I32)
        gate_ref[k:k + 1, :] = (wts[k] / wsum * ROUTED_SCALE).reshape(1, tm)
    run_ref[...] = run_ref[...] + jnp.sum(m2d, axis=1, keepdims=True)
    cnt_ref[...] = run_ref[...].astype(I32)


def _router(x1, w_router, router_bias):
    T, D = x1.shape
    tm = min(512, T)
    E = N_EXPERTS
    tok = lambda: pl.BlockSpec((TOP_K, tm), lambda i: (0, i))
    return pl.pallas_call(
        functools.partial(_router_kernel, tm=tm),
        grid=(T // tm,),
        in_specs=[pl.BlockSpec((tm, D), lambda i: (i, 0)),
                  pl.BlockSpec((E, D), lambda i: (0, 0)),
                  pl.BlockSpec((E, 1), lambda i: (0, 0))],
        out_specs=[tok(), tok(), tok(), pl.BlockSpec((E, 1), lambda i: (0, 0))],
        out_shape=[jax.ShapeDtypeStruct((TOP_K, T), I32), jax.ShapeDtypeStruct((TOP_K, T), I32),
                   jax.ShapeDtypeStruct((TOP_K, T), F32), jax.ShapeDtypeStruct((E, 1), I32)],
        scratch_shapes=[pltpu.VMEM((E, 1), F32), pltpu.VMEM((tm, tm), BF16)],
        compiler_params=_params(("arbitrary",), 32),
        name="router",
    )(x1, w_router.astype(F32).T, router_bias.astype(F32).reshape(E, 1))


def _dispatch_kernel(dest_ref, x_ref, xs_hbm, sem, *, tm):
    def row_copy(t, k):
        return pltpu.make_async_copy(x_ref.at[pl.ds(t, 1)], xs_hbm.at[pl.ds(dest_ref[k, t], 1)], sem)

    def issue(t, carry):
        for k in range(TOP_K):
            row_copy(t, k).start()
        return carry

    def drain(t, carry):
        for k in range(TOP_K):
            row_copy(t, k).wait()
        return carry

    lax.fori_loop(0, tm, issue, 0)
    lax.fori_loop(0, tm, drain, 0)


def _dispatch(dest, x1):
    T, D = x1.shape
    tm = min(128, T)
    return pl.pallas_call(
        functools.partial(_dispatch_kernel, tm=tm),
        grid=(T // tm,),
        in_specs=[pl.BlockSpec((TOP_K, tm), lambda i: (0, i), memory_space=pltpu.SMEM),
                  pl.BlockSpec((tm, D), lambda i: (i, 0))],
        out_specs=pl.BlockSpec(memory_space=pl.ANY),
        out_shape=jax.ShapeDtypeStruct((T * TOP_K, D), x1.dtype),
        scratch_shapes=[pltpu.SemaphoreType.DMA(())],
        compiler_params=_params(("arbitrary",), 16),
        name="dispatch",
    )(dest, x1)


def _gmm_kernel(vblk, vexp, vfirst, gstart, nvis, xs_ref, wg_ref, wu_ref, wd_ref, o_ref,
                wg_s, wu_s, wd_s):
    i = pl.program_id(0)
    e = vexp[i]
    e_prev = vexp[jnp.maximum(i - 1, 0)]

    @pl.when(jnp.logical_or(i == 0, e != e_prev))
    def _():
        wg_s[...] = wg_ref[...].astype(BF16)
        wu_s[...] = wu_ref[...].astype(BF16)
        wd_s[...] = wd_ref[...].astype(BF16)

    @pl.when(i < nvis[0])
    def _():
        x = xs_ref[...].astype(BF16)
        g = _dot(x, wg_s[...])
        u = _dot(x, wu_s[...])
        y = _dot((_silu(g) * u).astype(BF16), wd_s[...])
        row = vblk[i] * GMM_ROWS + lax.broadcasted_iota(I32, (GMM_ROWS, 1), 0)
        mine = jnp.logical_and(row >= gstart[e], row < gstart[e + 1])

        @pl.when(vfirst[i] == 1)
        def _():
            o_ref[...] = jnp.where(mine, y, 0.0)

        @pl.when(vfirst[i] == 0)
        def _():
            o_ref[...] = jnp.where(mine, y, o_ref[...])


def _gmm_plan(counts, n_rows):
    E = N_EXPERTS
    nblk = n_rows // GMM_ROWS
    nvis_max = nblk + E
    ends = jnp.cumsum(counts)
    gstart = jnp.concatenate([jnp.zeros((1,), I32), ends]).astype(I32)
    blk_lo = jnp.arange(nblk, dtype=I32) * GMM_ROWS
    e_lo = jnp.searchsorted(ends, blk_lo, side='right').astype(I32)
    e_hi = jnp.searchsorted(ends, blk_lo + (GMM_ROWS - 1), side='right').astype(I32)
    per_blk = e_hi - e_lo + 1
    vend = jnp.cumsum(per_blk).astype(I32)
    vbeg = vend - per_blk
    nvis = vend[-1]
    vi = jnp.minimum(jnp.arange(nvis_max, dtype=I32), nvis - 1)
    vblk = jnp.searchsorted(vend, vi, side='right').astype(I32)
    vexp = e_lo[vblk] + (vi - vbeg[vblk])
    vfirst = jnp.logical_and(vi == vbeg[vblk], jnp.arange(nvis_max) < nvis).astype(I32)
    return vblk, vexp, vfirst, gstart, nvis.reshape(1).astype(I32), nvis_max


def _gmm(xs, counts, w_gate, w_up, w_down):
    R, D = xs.shape
    FF = EXPERT_FF
    vblk, vexp, vfirst, gstart, nvis, nvis_max = _gmm_plan(counts, R)
    grid_spec = pltpu.PrefetchScalarGridSpec(
        num_scalar_prefetch=5,
        grid=(nvis_max,),
        in_specs=[pl.BlockSpec((GMM_ROWS, D), lambda i, vb, ve, vf, gs, nv: (vb[i], 0)),
                  pl.BlockSpec((None, None, D, FF), lambda i, vb, ve, vf, gs, nv: (0, ve[i], 0, 0)),
                  pl.BlockSpec((None, None, D, FF), lambda i, vb, ve, vf, gs, nv: (0, ve[i], 0, 0)),
                  pl.BlockSpec((None, None, FF, D), lambda i, vb, ve, vf, gs, nv: (0, ve[i], 0, 0))],
        out_specs=pl.BlockSpec((GMM_ROWS, D), lambda i, vb, ve, vf, gs, nv: (vb[i], 0)),
        scratch_shapes=[pltpu.VMEM((D, FF), BF16), pltpu.VMEM((D, FF), BF16), pltpu.VMEM((FF, D), BF16)],
    )
    return pl.pallas_call(
        _gmm_kernel,
        grid_spec=grid_spec,
        out_shape=jax.ShapeDtypeStruct((R, D), F32),
        compiler_params=_params(("arbitrary",), 56),
        name="gmm",
    )(vblk, vexp, vfirst, gstart, nvis, xs, w_gate, w_up, w_down)


def _combine_kernel(dest_ref, ys_hbm, gate_ref, x_ref, xb_ref, wsg_ref, wsu_ref, wsd_ref, g_ref, b_ref,
                    o_ref, buf_ref, sem, *, tm):
    def row_copy(t, k):
        return pltpu.make_async_copy(ys_hbm.at[pl.ds(dest_ref[k, t], 1)],
                                     buf_ref.at[k, pl.ds(t, 1)], sem)

    def issue(t, carry):
        for k in range(TOP_K):
            row_copy(t, k).start()
        return carry

    def drain(t, carry):
        for k in range(TOP_K):
            row_copy(t, k).wait()
        return carry

    lax.fori_loop(0, tm, issue, 0)
    xb = xb_ref[...]
    hidden = _silu(_dot(xb, wsg_ref[...])) * _dot(xb, wsu_ref[...])
    acc = DEEPNORM_ALPHA * x_ref[...] + _dot(hidden.astype(BF16), wsd_ref[...])
    lax.fori_loop(0, tm, drain, 0)
    gates = gate_ref[...]
    for k in range(TOP_K):
        acc = acc + gates[:, k:k + 1] * buf_ref[k]
    o_ref[...] = _layer_norm(acc, g_ref[...], b_ref[...])


def _combine(dest, ys, gates_t, x1, x1b, wsg, wsu, wsd, g, b):
    T, D = x1.shape
    tm = min(128, T)
    FF = wsg.shape[1]
    row = lambda: pl.BlockSpec((tm, D), lambda i: (i, 0))
    vec = lambda: pl.BlockSpec((1, D), lambda i: (0, 0))
    return pl.pallas_call(
        functools.partial(_combine_kernel, tm=tm),
        grid=(T // tm,),
        in_specs=[pl.BlockSpec((TOP_K, tm), lambda i: (0, i), memory_space=pltpu.SMEM),
                  pl.BlockSpec(memory_space=pl.ANY),
                  pl.BlockSpec((tm, TOP_K), lambda i: (i, 0)),
                  row(), row(),
                  pl.BlockSpec((D, FF), lambda i: (0, 0)),
                  pl.BlockSpec((D, FF), lambda i: (0, 0)),
                  pl.BlockSpec((FF, D), lambda i: (0, 0)),
                  vec(), vec()],
        out_specs=row(),
        out_shape=jax.ShapeDtypeStruct((T, D), F32),
        scratch_shapes=[pltpu.VMEM((TOP_K, tm, D), F32), pltpu.SemaphoreType.DMA(())],
        compiler_params=_params(("arbitrary",), 40),
        name="combine",
    )(dest, ys, gates_t, x1, x1b, wsg, wsu, wsd, g.reshape(1, D), b.reshape(1, D))


def kernel(x, w_in, conv_w, gdn_a_log, gdn_dt_bias, gdn_norm_w, diff_lambda, diff_subln_w, rel_bias_table,
           w_branch_a, w_branch_b, w_out, ln1_g, ln1_b, w_router, router_bias, w_gate, w_up, w_down,
           ws_gate, ws_up, ws_down, ln2_g, ln2_b):
    B, T, D = x.shape
    assert B == 1 and D == D_MODEL
    x2 = x.reshape(T, D)
    xb = x2.astype(BF16)
    w_in0 = w_in[0]
    H = DIFF_HEADS

    qkv = _proj(xb, w_in0, 0, 6, 512, BF16)
    g4 = _proj(xb, w_in0, 6, 8, 512, F32)
    gab = _proj(xb, w_in0, 56, 1, LANES, F32)
    gate_col = 7 * 1024 + 2 * GDN_HEADS
    w_ga = w_in0[:, gate_col:gate_col + D]
    w_gb = w_in0[:, gate_col + D:gate_col + 2 * D]

    tq = min(512, T)
    ya = _attn(qkv, _attn_bias_tiles(rel_bias_table.astype(F32), tq), diff_lambda[0].astype(F32),
               diff_subln_w[0].astype(F32), tq)
    yb = _gdn(g4, gab, conv_w[0].astype(F32), gdn_a_log[0], gdn_dt_bias[0], gdn_norm_w[0])
    merged = _merge(ya, yb, xb, w_branch_a[0], w_branch_b[0], w_ga, w_gb)
    x1, x1b = _outln(merged, w_out[0].astype(BF16), x2, ln1_g[0], ln1_b[0])

    eidx, rank, gates, counts = _router(x1, w_router[0], router_bias[0])
    counts = counts.reshape(N_EXPERTS)
    starts = jnp.cumsum(counts) - counts
    dest = rank + jnp.sum(jnp.where(eidx[None] == jnp.arange(N_EXPERTS, dtype=I32)[:, None, None],
                                    starts[:, None, None], 0), axis=0)
    xs = _dispatch(dest, x1)
    ys = _gmm(xs, counts, w_gate, w_up, w_down)
    out = _combine(dest, ys, gates.T, x1, x1b, ws_gate[0].astype(BF16), ws_up[0].astype(BF16),
                   ws_down[0].astype(BF16), ln2_g[0], ln2_b[0])
    return out.reshape(B, T, D)
```

```python
import functools
import math

import jax
import jax.numpy as jnp
from jax import lax
from jax.experimental import pallas as pl
from jax.experimental.pallas import tpu as pltpu

F32 = jnp.float32
BF16 = jnp.bfloat16
I32 = jnp.int32

D_MODEL = 2048
CHUNK = 64
DIFF_HEADS = 8
DIFF_HEAD_DIM = 64
GDN_HEADS = 8
GDN_HEAD_DIM = 128
CONV_WIDTH = 4
REL_BUCKETS = 32
REL_MAX_DISTANCE = 128
N_EXPERTS = 64
TOP_K = 8
N_GROUPS = 8
TOPK_GROUPS = 4
EXPERT_FF = 512
ROUTED_SCALE = 2.5
DEPTH = 1
DEEPNORM_ALPHA = (2 * DEPTH) ** 0.25
LN_EPS = 1e-5
NORM_EPS = 1e-6
LAMBDA_INIT = 0.8 - 0.6 * math.exp(-0.3 * 0)

LANES = 128
MIB = 1024 * 1024
NEG_BIG = -1e30
LOG2E = math.log2(math.e)
GMM_ROWS = 256


def _params(semantics, vmem_mib):
    return pltpu.CompilerParams(dimension_semantics=semantics, vmem_limit_bytes=vmem_mib * MIB)


def _sigmoid(x):
    return 1.0 / (1.0 + jnp.exp(-x))


def _silu(x):
    return x * _sigmoid(x)


def _dot(a, b):
    return jnp.dot(a, b, preferred_element_type=F32)


def _dot_nt(a, b):
    return lax.dot_general(a, b, (((1,), (1,)), ((), ())), preferred_element_type=F32)


def _split3(a):
    hi = a.astype(BF16)
    r1 = a - hi.astype(F32)
    mid = r1.astype(BF16)
    lo = (r1 - mid.astype(F32)).astype(BF16)
    return hi, mid, lo


def _proj_kernel(x_ref, w_ref, o_ref, wbf_ref, *, tn):
    @pl.when(pl.program_id(1) == 0)
    def _():
        wbf_ref[...] = w_ref[...].astype(BF16)

    acc = _dot(x_ref[...], wbf_ref[...])
    for j in range(tn // LANES):
        o_ref[j] = acc[:, j * LANES:(j + 1) * LANES].astype(o_ref.dtype)


def _proj(xb, w, blk0, nblk, tn, odt):
    T, D = xb.shape
    tm = min(1024, T)
    return pl.pallas_call(
        functools.partial(_proj_kernel, tn=tn),
        grid=(nblk, T // tm),
        in_specs=[pl.BlockSpec((tm, D), lambda n, m: (m, 0)),
                  pl.BlockSpec((D, tn), lambda n, m: (0, blk0 + n))],
        out_specs=pl.BlockSpec((tn // LANES, tm, LANES), lambda n, m: (n, m, 0)),
        out_shape=jax.ShapeDtypeStruct((nblk * tn // LANES, T, LANES), odt),
        scratch_shapes=[pltpu.VMEM((D, tn), BF16)],
        compiler_params=_params(("arbitrary", "arbitrary"), 40),
        name="proj",
    )(xb, w)


def _attn_kernel(q_ref, k_ref, v_ref, bias_ref, lam_ref, subw_ref, o_ref,
                 m1, l1, a1, m2, l2, a2, s_buf, *, tq):
    qi = pl.program_id(1)
    q = q_ref[0].astype(F32) * (DIFF_HEAD_DIM ** -0.5 * LOG2E)
    ones = jnp.ones((tq, LANES), BF16)
    lane = lax.broadcasted_iota(I32, q.shape, 1)
    q1 = jnp.where(lane < DIFF_HEAD_DIM, q, 0.0).astype(BF16)
    q2 = jnp.where(lane >= DIFF_HEAD_DIM, q, 0.0).astype(BF16)
    for m, l, a in ((m1, l1, a1), (m2, l2, a2)):
        m[...] = jnp.full(m.shape, NEG_BIG, F32)
        l[...] = jnp.zeros(l.shape, F32)
        a[...] = jnp.zeros(a.shape, F32)

    def scores(j, slot):
        k = k_ref[0, pl.ds(pl.multiple_of(j * tq, tq), tq), :]
        s_buf[slot, 0] = _dot_nt(q1, k)
        s_buf[slot, 1] = _dot_nt(q2, k)

    def accumulate(j, slot, bias_idx):
        v = jnp.concatenate([v_ref[0, pl.ds(pl.multiple_of(j * tq, tq), tq), :], ones], axis=1)
        for i, (m, l, a) in enumerate(((m1, l1, a1), (m2, l2, a2))):
            s = s_buf[slot, i]
            if bias_idx is not None:
                s = s + bias_ref[0, bias_idx]
            m_old = m[...]
            m_new = jnp.maximum(m_old, jnp.max(s, axis=1, keepdims=True))
            alpha = jnp.exp2(m_old - m_new)
            p = jnp.exp2(s - jnp.concatenate([m_new] * (tq // LANES), axis=1))
            pv = _dot(p.astype(BF16), v)
            l[...] = alpha * l[...] + pv[:, LANES:]
            a[...] = alpha * a[...] + pv[:, :LANES]
            m[...] = m_new

    n_far = jnp.maximum(qi - 1, 0)
    odd = lax.rem(n_far, 2)

    @pl.when(odd == 1)
    def _():
        scores(0, 1)
        scores(1, 0)
        accumulate(0, 1, None)

    @pl.when(odd == 0)
    def _():
        scores(0, 0)

    @pl.loop(0, n_far // 2)
    def _(t):
        base = odd + 2 * t
        scores(base + 1, 1)
        accumulate(base, 0, None)
        scores(base + 2, 0)
        accumulate(base + 1, 1, None)

    @pl.when(qi >= 1)
    def _():
        scores(qi, 1)
        accumulate(qi - 1, 0, 1)
        accumulate(qi, 1, 0)

    @pl.when(qi == 0)
    def _():
        accumulate(0, 0, 0)

    lf = lam_ref[...]
    lam = (jnp.exp(jnp.sum(lf[0:1] * lf[1:2], axis=1, keepdims=True))
           - jnp.exp(jnp.sum(lf[2:3] * lf[3:4], axis=1, keepdims=True)) + LAMBDA_INIT)
    o = a1[...] / l1[...] - lam * (a2[...] / l2[...])
    y = o * lax.rsqrt(jnp.mean(o * o, axis=1, keepdims=True) + NORM_EPS) * subw_ref[...] * (1.0 - LAMBDA_INIT)
    o_ref[...] = y.astype(o_ref.dtype)


def _t5_bucket(rel):
    nb = REL_BUCKETS // 2
    base = jnp.where(rel > 0, nb, 0)
    n = jnp.abs(rel)
    max_exact = nb // 2
    nf = jnp.maximum(n, 1).astype(F32)
    large = max_exact + (jnp.log(nf / max_exact) / math.log(REL_MAX_DISTANCE / max_exact)
                         * (nb - max_exact)).astype(I32)
    large = jnp.minimum(large, nb - 1)
    return base + jnp.where(n < max_exact, n, large)


def _attn_bias_tiles(rel_table, tq):
    H = rel_table.shape[1]
    span = 3 * tq
    rel = jnp.arange(span) - (2 * tq - 1)
    onehot = (_t5_bucket(rel)[:, None] == jnp.arange(REL_BUCKETS)[None, :]).astype(F32)
    far = rel_table[REL_BUCKETS // 2 - 1]
    band = (jnp.dot(onehot, rel_table, precision=lax.Precision.HIGHEST) - far).T
    band = band * LOG2E
    skew = jnp.tile(band, (1, tq))[:, :tq * (span - 1)].reshape(H, tq, span - 1)
    sub = skew[:, :, tq - 1:2 * tq - 1]
    diag = skew[:, :, 2 * tq - 1:3 * tq - 1]
    qpos = jnp.arange(tq)[:, None]
    kpos = jnp.arange(tq)[None, :]
    visible = (kpos // CHUNK) <= (qpos // CHUNK)
    diag = jnp.where(visible[None], diag, NEG_BIG)
    return jnp.stack([diag, sub], axis=1).astype(F32)


def _attn(qkv, bias_tiles, lam_vecs, subln_w, tq):
    nb, T, _ = qkv.shape
    H = DIFF_HEADS
    return pl.pallas_call(
        functools.partial(_attn_kernel, tq=tq),
        grid=(H, T // tq),
        in_specs=[pl.BlockSpec((1, tq, LANES), lambda h, i: (h, i, 0)),
                  pl.BlockSpec((1, T, LANES), lambda h, i: (H + h, 0, 0)),
                  pl.BlockSpec((1, T, LANES), lambda h, i: (2 * H + h, 0, 0)),
                  pl.BlockSpec((1, 2, tq, tq), lambda h, i: (h, 0, 0, 0)),
                  pl.BlockSpec((4, DIFF_HEAD_DIM), lambda h, i: (0, 0)),
                  pl.BlockSpec((1, LANES), lambda h, i: (0, 0))],
        out_specs=pl.BlockSpec((tq, LANES), lambda h, i: (i, h)),
        out_shape=jax.ShapeDtypeStruct((T, H * LANES), BF16),
        scratch_shapes=[pltpu.VMEM((tq, LANES), F32)] * 6 + [pltpu.VMEM((2, 2, tq, tq), F32)],
        compiler_params=_params(("arbitrary", "arbitrary"), 40),
        name="attn",
    )(qkv, qkv, qkv, bias_tiles, lam_vecs, subln_w.reshape(1, LANES))


def _gdn_kernel(g4_ref, gab_ref, convw_ref, alog_ref, dtb_ref, normw_ref, o_ref,
                s_ref, xbuf_ref):
    C = CHUNK
    H = GDN_HEADS
    n = pl.program_id(0)

    @pl.when(n == 0)
    def _():
        s_ref[...] = jnp.zeros(s_ref.shape, F32)
        xbuf_ref[...] = jnp.zeros(xbuf_ref.shape, F32)

    gab = gab_ref[0]
    sp = gab + dtb_ref[...]
    softplus = jnp.maximum(sp, 0.0) + jnp.log(1.0 + jnp.exp(-jnp.abs(sp)))
    g = -jnp.exp(alog_ref[...]) * softplus
    beta_all = _sigmoid(gab)
    row = lax.broadcasted_iota(I32, (C, LANES), 0)
    gc = g
    for sh in (1, 2, 4, 8, 16, 32):
        gc = gc + jnp.where(row >= sh, pltpu.roll(gc, sh, 0), 0.0)
    gc_t = gc.T

    ri = lax.broadcasted_iota(I32, (C, C), 0)
    ci = lax.broadcasted_iota(I32, (C, C), 1)
    tri = ri >= ci
    strict = ri > ci
    eye = (ri == ci).astype(F32)

    for c in range(3 * H):
        xbuf_ref[c, 8:8 + C, :] = g4_ref[c]

    def conv(c):
        acc = xbuf_ref[c, 8 - (CONV_WIDTH - 1):8 - (CONV_WIDTH - 1) + C, :] * convw_ref[0, c:c + 1, :]
        for j in range(1, CONV_WIDTH):
            lo = 8 - (CONV_WIDTH - 1) + j
            acc = acc + xbuf_ref[c, lo:lo + C, :] * convw_ref[j, c:c + 1, :]
        return _silu(acc)

    def l2n(x):
        return x * lax.rsqrt(jnp.sum(x * x, axis=1, keepdims=True) + NORM_EPS)

    scale = GDN_HEAD_DIM ** -0.5
    heads = range(H)
    q = [l2n(conv(h)) * scale for h in heads]
    k = [l2n(conv(H + h)) for h in heads]
    v = [conv(2 * H + h) for h in heads]
    beta = [beta_all[:, 8 + h:9 + h] for h in heads]
    gcol = [gc[:, h:h + 1] for h in heads]
    glast = [gc[C - 1:C, h:h + 1] for h in heads]
    eg = [jnp.exp(gcol[h]) for h in heads]
    decay = [jnp.exp(jnp.where(tri, gcol[h] - gc_t[h:h + 1, :], -jnp.inf)) for h in heads]
    kb = [k[h] * beta[h] for h in heads]
    prod = [_dot_nt(jnp.concatenate([kb[h], q[h]], axis=0).astype(BF16), k[h].astype(BF16))
            for h in heads]
    lmat = [jnp.where(strict, prod[h][:C] * decay[h], 0.0) for h in heads]
    aqk = [jnp.where(tri, prod[h][C:] * decay[h], 0.0).astype(BF16) for h in heads]
    tm = [eye - lmat[h] for h in heads]
    pw = lmat
    for _ in range(5):
        pwb = [pw[h].astype(BF16) for h in heads]
        pw = [_dot(pwb[h], pwb[h]) for h in heads]
        tm = [tm[h] + _dot(tm[h].astype(BF16), pw[h].astype(BF16)) for h in heads]
    uw = [_dot(tm[h].astype(BF16),
               jnp.concatenate([v[h] * beta[h], kb[h] * eg[h]], axis=1).astype(BF16)) for h in heads]
    s_old = [s_ref[h] for h in heads]
    sd = [_dot(jnp.concatenate([uw[h][:, LANES:], q[h] * eg[h]], axis=0).astype(BF16), s_old[h].astype(BF16))
          for h in heads]
    vnb = [(uw[h][:, :LANES] - sd[h][:C]).astype(BF16) for h in heads]
    o = [sd[h][C:] + _dot(aqk[h], vnb[h]) for h in heads]
    for h in heads:
        k_dec = (k[h] * jnp.exp(glast[h] - gcol[h])).astype(BF16)
        s_ref[h] = s_old[h] * jnp.exp(glast[h]) + lax.dot_general(
            k_dec, vnb[h], (((0,), (0,)), ((), ())), preferred_element_type=F32)
    for h in heads:
        on = o[h] * lax.rsqrt(jnp.mean(o[h] * o[h], axis=1, keepdims=True) + NORM_EPS) * normw_ref[...]
        o_ref[:, h * LANES:(h + 1) * LANES] = (on * _silu(g4_ref[3 * H + h])).astype(o_ref.dtype)

    for c in range(3 * H):
        xbuf_ref[c, 0:8, :] = g4_ref[c, C - 8:C, :]


def _gdn(g4, gab, conv_w, a_log, dt_bias, norm_w):
    nb, T, _ = g4.shape
    H = GDN_HEADS
    C = CHUNK
    pad = lambda v, off: jnp.zeros((1, LANES), F32).at[0, off:off + H].set(v.astype(F32))
    return pl.pallas_call(
        _gdn_kernel,
        grid=(T // C,),
        in_specs=[pl.BlockSpec((4 * H, C, LANES), lambda n: (0, n, 0)),
                  pl.BlockSpec((1, C, LANES), lambda n: (0, n, 0)),
                  pl.BlockSpec((CONV_WIDTH, 3 * H, LANES), lambda n: (0, 0, 0)),
                  pl.BlockSpec((1, LANES), lambda n: (0, 0)),
                  pl.BlockSpec((1, LANES), lambda n: (0, 0)),
                  pl.BlockSpec((1, LANES), lambda n: (0, 0))],
        out_specs=pl.BlockSpec((C, H * LANES), lambda n: (n, 0)),
        out_shape=jax.ShapeDtypeStruct((T, H * LANES), BF16),
        scratch_shapes=[pltpu.VMEM((H, GDN_HEAD_DIM, GDN_HEAD_DIM), F32),
                        pltpu.VMEM((3 * H, 8 + C, LANES), F32)],
        compiler_params=_params(("arbitrary",), 32),
        name="gdn",
    )(g4, gab, conv_w.reshape(CONV_WIDTH, 3 * H, LANES), pad(a_log, 0), pad(dt_bias, 0),
      norm_w.reshape(1, LANES).astype(F32))


def _merge_kernel(ya_ref, yb_ref, x_ref, wa_ref, wb_ref, wga_ref, wgb_ref, o_ref,
                  wa_s, wb_s, wga_s, wgb_s):
    @pl.when(pl.program_id(1) == 0)
    def _():
        wa_s[...] = wa_ref[...].astype(BF16)
        wb_s[...] = wb_ref[...].astype(BF16)
        wga_s[...] = wga_ref[...].astype(BF16)
        wgb_s[...] = wgb_ref[...].astype(BF16)

    x = x_ref[...]
    a = _dot(ya_ref[...], wa_s[...])
    b = _dot(yb_ref[...], wb_s[...])
    ga = _sigmoid(_dot(x, wga_s[...]))
    gb = _sigmoid(_dot(x, wgb_s[...]))
    o_ref[...] = (ga * a + gb * b).astype(o_ref.dtype)


def _merge(ya, yb, xb, w_a, w_b, w_ga, w_gb):
    T, D = xb.shape
    W = ya.shape[1]
    tm = min(512, T)
    tn = 512
    wspec = lambda rows: pl.BlockSpec((rows, tn), lambda n, m: (0, n))
    return pl.pallas_call(
        _merge_kernel,
        grid=(D // tn, T // tm),
        in_specs=[pl.BlockSpec((tm, W), lambda n, m: (m, 0)),
                  pl.BlockSpec((tm, W), lambda n, m: (m, 0)),
                  pl.BlockSpec((tm, D), lambda n, m: (m, 0)),
                  wspec(W), wspec(W), wspec(D), wspec(D)],
        out_specs=pl.BlockSpec((tm, tn), lambda n, m: (m, n)),
        out_shape=jax.ShapeDtypeStruct((T, D), BF16),
        scratch_shapes=[pltpu.VMEM((W, tn), BF16), pltpu.VMEM((W, tn), BF16),
                        pltpu.VMEM((D, tn), BF16), pltpu.VMEM((D, tn), BF16)],
        compiler_params=_params(("arbitrary", "arbitrary"), 48),
        name="merge",
    )(ya, yb, xb, w_a, w_b, w_ga, w_gb)


def _layer_norm(x, g, b):
    mu = jnp.mean(x, axis=1, keepdims=True)
    xc = x - mu
    var = jnp.mean(xc * xc, axis=1, keepdims=True)
    return xc * lax.rsqrt(var + LN_EPS) * g + b


def _outln_kernel(mg_ref, w_ref, x_ref, g_ref, b_ref, o_ref, ob_ref):
    h = _dot(mg_ref[...], w_ref[...])
    y = _layer_norm(DEEPNORM_ALPHA * x_ref[...] + h, g_ref[...], b_ref[...])
    o_ref[...] = y
    ob_ref[...] = y.astype(BF16)


def _outln(merged, w_out_bf, x, g, b):
    T, D = x.shape
    tm = min(512, T)
    row = lambda: pl.BlockSpec((tm, D), lambda m: (m, 0))
    vec = lambda: pl.BlockSpec((1, D), lambda m: (0, 0))
    return pl.pallas_call(
        _outln_kernel,
        grid=(T // tm,),
        in_specs=[row(), pl.BlockSpec((D, D), lambda m: (0, 0)), row(), vec(), vec()],
        out_specs=[row(), row()],
        out_shape=[jax.ShapeDtypeStruct((T, D), F32), jax.ShapeDtypeStruct((T, D), BF16)],
        compiler_params=_params(("arbitrary",), 48),
        name="outln",
    )(merged, w_out_bf, x, g.reshape(1, D), b.reshape(1, D))


def _router_kernel(x_ref, wr_ref, bias_ref, eidx_ref, rank_ref, gate_ref, cnt_ref,
                   run_ref, tri_ref, *, tm):
    G = N_GROUPS
    P = N_EXPERTS // N_GROUPS
    i = pl.program_id(0)

    @pl.when(i == 0)
    def _():
        run_ref[...] = jnp.zeros(run_ref.shape, F32)
        r = lax.broadcasted_iota(I32, (tm, tm), 0)
        c = lax.broadcasted_iota(I32, (tm, tm), 1)
        tri_ref[...] = jnp.where(r < c, 1.0, 0.0).astype(BF16)

    w3 = _split3(wr_ref[...])
    x3 = _split3(x_ref[...])
    logits = _dot_nt(w3[1], x3[1])
    for a, b in ((0, 2), (2, 0), (0, 1), (1, 0), (0, 0)):
        logits = logits + _dot_nt(w3[a], x3[b])
    scores = _sigmoid(logits)
    choice = scores + bias_ref[...]
    c3 = choice.reshape(G, P, tm)
    s3 = scores.reshape(G, P, tm)
    j_iota = lax.broadcasted_iota(I32, (G, P, tm), 1)
    g_iota = lax.broadcasted_iota(I32, (G, 1, tm), 0)
    e_iota = lax.broadcasted_iota(I32, (G, P, tm), 0) * P + j_iota
    ninf = -jnp.inf

    def amax3(v, idx, big):
        m = jnp.max(jnp.max(v, axis=1, keepdims=True), axis=0, keepdims=True)
        f = jnp.min(jnp.min(jnp.where(v == m, idx, big), axis=1, keepdims=True), axis=0, keepdims=True)
        return m, f

    m1 = jnp.max(c3, axis=1, keepdims=True)
    f1 = jnp.min(jnp.where(c3 == m1, j_iota, P), axis=1, keepdims=True)
    m2 = jnp.max(jnp.where(j_iota == f1, ninf, c3), axis=1, keepdims=True)
    gsc = m1 + m2
    gsel = jnp.zeros((G, 1, tm), jnp.bool_)
    for _ in range(TOPK_GROUPS):
        _, f = amax3(gsc, g_iota, G)
        hit = g_iota == f
        gsel = jnp.logical_or(gsel, hit)
        gsc = jnp.where(hit, ninf, gsc)
    cm = jnp.where(gsel, c3, ninf)

    picks = []
    wts = []
    msel = jnp.zeros((G, P, tm), F32)
    for _ in range(TOP_K):
        _, f = amax3(cm, e_iota, N_EXPERTS)
        hit = e_iota == f
        wts.append(jnp.sum(jnp.sum(jnp.where(hit, s3, 0.0), axis=1, keepdims=True), axis=0, keepdims=True))
        picks.append(f)
        msel = jnp.where(hit, 1.0, msel)
        cm = jnp.where(hit, ninf, cm)
    wsum = wts[0]
    for w in wts[1:]:
        wsum = wsum + w

    m2d = msel.reshape(N_EXPERTS, tm)
    before = run_ref[...] + _dot(m2d.astype(BF16), tri_ref[...])
    b3 = before.reshape(G, P, tm)
    for k in range(TOP_K):
        hit = e_iota == picks[k]
        rk = jnp.sum(jnp.sum(jnp.where(hit, b3, 0.0), axis=1, keepdims=True), axis=0, keepdims=True)
        eidx_ref[k:k + 1, :] = picks[k].reshape(1, tm)
        rank_ref[k:k + 1, :] = rk.reshape(1, tm).astype(I32)
        gate_ref[k:k + 1, :] = (wts[k] / wsum * ROUTED_SCALE).reshape(1, tm)
    run_ref[...] = run_ref[...] + jnp.sum(m2d, axis=1, keepdims=True)
    cnt_ref[...] = run_ref[...].astype(I32)


def _router(x1, w_router, router_bias):
    T, D = x1.shape
    tm = min(512, T)
    E = N_EXPERTS
    tok = lambda: pl.BlockSpec((TOP_K, tm), lambda i: (0, i))
    return pl.pallas_call(
        functools.partial(_router_kernel, tm=tm),
        grid=(T // tm,),
        in_specs=[pl.BlockSpec((tm, D), lambda i: (i, 0)),
                  pl.BlockSpec((E, D), lambda i: (0, 0)),
                  pl.BlockSpec((E, 1), lambda i: (0, 0))],
        out_specs=[tok(), tok(), tok(), pl.BlockSpec((E, 1), lambda i: (0, 0))],
        out_shape=[jax.ShapeDtypeStruct((TOP_K, T), I32), jax.ShapeDtypeStruct((TOP_K, T), I32),
                   jax.ShapeDtypeStruct((TOP_K, T), F32), jax.ShapeDtypeStruct((E, 1), I32)],
        scratch_shapes=[pltpu.VMEM((E, 1), F32), pltpu.VMEM((tm, tm), BF16)],
        compiler_params=_params(("arbitrary",), 32),
        name="router",
    )(x1, w_router.astype(F32).T, router_bias.astype(F32).reshape(E, 1))


def _dest_kernel(starts_ref, eidx_ref, rank_ref, o_ref):
    e = eidx_ref[...]
    acc = rank_ref[...]
    for j in range(N_EXPERTS):
        acc = acc + jnp.where(e == j, starts_ref[j], 0)
    o_ref[...] = acc


def _dest(starts, eidx, rank):
    full = lambda: pl.BlockSpec(eidx.shape, lambda i, s: (0, 0))
    return pl.pallas_call(
        _dest_kernel,
        grid_spec=pltpu.PrefetchScalarGridSpec(num_scalar_prefetch=1, grid=(1,),
                                               in_specs=[full(), full()], out_specs=full()),
        out_shape=jax.ShapeDtypeStruct(eidx.shape, I32),
        compiler_params=_params(("arbitrary",), 16),
        name="dest",
    )(starts, eidx, rank)


def _dispatch_kernel(dest_ref, x_ref, xs_hbm, sem, *, tm):
    def row_copy(t, k):
        return pltpu.make_async_copy(x_ref.at[pl.ds(t, 1)], xs_hbm.at[pl.ds(dest_ref[k, t], 1)], sem)

    def issue(t, carry):
        for k in range(TOP_K):
            row_copy(t, k).start()
        return carry

    def drain(t, carry):
        for k in range(TOP_K):
            row_copy(t, k).wait()
        return carry

    lax.fori_loop(0, tm, issue, 0)
    lax.fori_loop(0, tm, drain, 0)


def _dispatch(dest, x1):
    T, D = x1.shape
    tm = min(128, T)
    return pl.pallas_call(
        functools.partial(_dispatch_kernel, tm=tm),
        grid=(T // tm,),
        in_specs=[pl.BlockSpec((TOP_K, tm), lambda i: (0, i), memory_space=pltpu.SMEM),
                  pl.BlockSpec((tm, D), lambda i: (i, 0))],
        out_specs=pl.BlockSpec(memory_space=pl.ANY),
        out_shape=jax.ShapeDtypeStruct((T * TOP_K, D), x1.dtype),
        scratch_shapes=[pltpu.SemaphoreType.DMA(())],
        compiler_params=_params(("arbitrary",), 16),
        name="dispatch",
    )(dest, x1)


def _gmm_kernel(vblk, vexp, vfirst, vchg, vslot, vnext, gstart, nvis, xs_ref, wg_hbm, wu_hbm, wd_hbm, o_ref,
                wg_buf, wu_buf, wd_buf, wg_s, wu_s, wd_s, sem):
    i = pl.program_id(0)
    e = vexp[i]

    def weight_copies(expert, slot):
        return [pltpu.make_async_copy(w_hbm.at[0, expert], buf.at[slot], sem.at[slot, j])
                for j, (w_hbm, buf) in enumerate(((wg_hbm, wg_buf), (wu_hbm, wu_buf), (wd_hbm, wd_buf)))]

    @pl.when(i == 0)
    def _():
        for c in weight_copies(e, vslot[0]):
            c.start()

    @pl.when(vchg[i] == 1)
    def _():
        slot = vslot[i]
        for c in weight_copies(e, slot):
            c.wait()

        @pl.when(vnext[i] >= 0)
        def _():
            for c in weight_copies(vnext[i], 1 - slot):
                c.start()

        wg_s[...] = wg_buf[slot].astype(BF16)
        wu_s[...] = wu_buf[slot].astype(BF16)
        wd_s[...] = wd_buf[slot].astype(BF16)

    @pl.when(i < nvis[0])
    def _():
        x = xs_ref[...].astype(BF16)
        g = _dot(x, wg_s[...])
        u = _dot(x, wu_s[...])
        y = _dot((_silu(g) * u).astype(BF16), wd_s[...])
        row = vblk[i] * GMM_ROWS + lax.broadcasted_iota(I32, (GMM_ROWS, 1), 0)
        mine = jnp.logical_and(row >= gstart[e], row < gstart[e + 1])

        @pl.when(vfirst[i] == 1)
        def _():
            o_ref[...] = jnp.where(mine, y, 0.0)

        @pl.when(vfirst[i] == 0)
        def _():
            o_ref[...] = jnp.where(mine, y, o_ref[...])


def _gmm_plan(counts, n_rows):
    E = N_EXPERTS
    nblk = n_rows // GMM_ROWS
    nvis_max = nblk + E
    ends = jnp.cumsum(counts)
    gstart = jnp.concatenate([jnp.zeros((1,), I32), ends]).astype(I32)
    blk_lo = jnp.arange(nblk, dtype=I32) * GMM_ROWS
    e_lo = jnp.searchsorted(ends, blk_lo, side='right').astype(I32)
    e_hi = jnp.searchsorted(ends, blk_lo + (GMM_ROWS - 1), side='right').astype(I32)
    per_blk = e_hi - e_lo + 1
    vend = jnp.cumsum(per_blk).astype(I32)
    vbeg = vend - per_blk
    nvis = vend[-1]
    vi = jnp.minimum(jnp.arange(nvis_max, dtype=I32), nvis - 1)
    vblk = jnp.searchsorted(vend, vi, side='right').astype(I32)
    vexp = jnp.minimum(e_lo[vblk] + (vi - vbeg[vblk]), E - 1)
    vfirst = jnp.logical_and(vi == vbeg[vblk], jnp.arange(nvis_max) < nvis).astype(I32)
    vchg = jnp.concatenate([jnp.ones((1,), I32), (vexp[1:] != vexp[:-1]).astype(I32)])
    vslot = (jnp.cumsum(vchg) - 1) % 2
    pos = jnp.where(vchg == 1, jnp.arange(nvis_max, dtype=I32), nvis_max)
    nxt = jnp.concatenate([lax.cummin(pos[::-1])[::-1][1:], jnp.full((1,), nvis_max, I32)])
    vnext = jnp.where(nxt < nvis_max, vexp[jnp.minimum(nxt, nvis_max - 1)], -1)
    sched = (vblk, vexp, vfirst, vchg, vslot.astype(I32), vnext.astype(I32), gstart, nvis.reshape(1).astype(I32))
    return sched, nvis_max


def _gmm(xs, counts, w_gate, w_up, w_down):
    R, D = xs.shape
    FF = EXPERT_FF
    sched, nvis_max = _gmm_plan(counts, R)
    rows = lambda: pl.BlockSpec((GMM_ROWS, D), lambda i, vb, *_: (vb[i], 0))
    hbm = lambda: pl.BlockSpec(memory_space=pl.ANY)
    grid_spec = pltpu.PrefetchScalarGridSpec(
        num_scalar_prefetch=len(sched),
        grid=(nvis_max,),
        in_specs=[rows(), hbm(), hbm(), hbm()],
        out_specs=rows(),
        scratch_shapes=[pltpu.VMEM((2, D, FF), F32), pltpu.VMEM((2, D, FF), F32), pltpu.VMEM((2, FF, D), F32),
                        pltpu.VMEM((D, FF), BF16), pltpu.VMEM((D, FF), BF16), pltpu.VMEM((FF, D), BF16),
                        pltpu.SemaphoreType.DMA((2, 3))],
    )
    return pl.pallas_call(
        _gmm_kernel,
        grid_spec=grid_spec,
        out_shape=jax.ShapeDtypeStruct((R, D), F32),
        compiler_params=_params(("arbitrary",), 56),
        name="gmm",
    )(*sched, xs, w_gate, w_up, w_down)


def _combine_kernel(dest_ref, ys_hbm, gate_ref, x_ref, xb_ref, wsg_ref, wsu_ref, wsd_ref, g_ref, b_ref,
                    o_ref, buf_ref, sem, *, tm):
    def row_copy(t, k):
        return pltpu.make_async_copy(ys_hbm.at[pl.ds(dest_ref[k, t], 1)],
                                     buf_ref.at[k, pl.ds(t, 1)], sem)

    def issue(t, carry):
        for k in range(TOP_K):
            row_copy(t, k).start()
        return carry

    def drain(t, carry):
        for k in range(TOP_K):
            row_copy(t, k).wait()
        return carry

    lax.fori_loop(0, tm, issue, 0)
    xb = xb_ref[...]
    hidden = _silu(_dot(xb, wsg_ref[...])) * _dot(xb, wsu_ref[...])
    acc = DEEPNORM_ALPHA * x_ref[...] + _dot(hidden.astype(BF16), wsd_ref[...])
    lax.fori_loop(0, tm, drain, 0)
    gates = gate_ref[...]
    for k in range(TOP_K):
        acc = acc + gates[:, k:k + 1] * buf_ref[k]
    o_ref[...] = _layer_norm(acc, g_ref[...], b_ref[...])


def _combine(dest, ys, gates_t, x1, x1b, wsg, wsu, wsd, g, b):
    T, D = x1.shape
    tm = min(128, T)
    FF = wsg.shape[1]
    row = lambda: pl.BlockSpec((tm, D), lambda i: (i, 0))
    vec = lambda: pl.BlockSpec((1, D), lambda i: (0, 0))
    return pl.pallas_call(
        functools.partial(_combine_kernel, tm=tm),
        grid=(T // tm,),
        in_specs=[pl.BlockSpec((TOP_K, tm), lambda i: (0, i), memory_space=pltpu.SMEM),
                  pl.BlockSpec(memory_space=pl.ANY),
                  pl.BlockSpec((tm, TOP_K), lambda i: (i, 0)),
                  row(), row(),
                  pl.BlockSpec((D, FF), lambda i: (0, 0)),
                  pl.BlockSpec((D, FF), lambda i: (0, 0)),
                  pl.BlockSpec((FF, D), lambda i: (0, 0)),
                  vec(), vec()],
        out_specs=row(),
        out_shape=jax.ShapeDtypeStruct((T, D), F32),
        scratch_shapes=[pltpu.VMEM((TOP_K, tm, D), F32), pltpu.SemaphoreType.DMA(())],
        compiler_params=_params(("arbitrary",), 40),
        name="combine",
    )(dest, ys, gates_t, x1, x1b, wsg, wsu, wsd, g.reshape(1, D), b.reshape(1, D))


def kernel(x, w_in, conv_w, gdn_a_log, gdn_dt_bias, gdn_norm_w, diff_lambda, diff_subln_w, rel_bias_table,
           w_branch_a, w_branch_b, w_out, ln1_g, ln1_b, w_router, router_bias, w_gate, w_up, w_down,
           ws_gate, ws_up, ws_down, ln2_g, ln2_b):
    B, T, D = x.shape
    assert B == 1 and D == D_MODEL
    x2 = x.reshape(T, D)
    xb = x2.astype(BF16)
    w_in0 = w_in[0]
    H = DIFF_HEADS

    qkv = _proj(xb, w_in0, 0, 6, 512, BF16)
    g4 = _proj(xb, w_in0, 6, 8, 512, F32)
    gab = _proj(xb, w_in0, 56, 1, LANES, F32)
    gate_col = 7 * 1024 + 2 * GDN_HEADS
    w_ga = w_in0[:, gate_col:gate_col + D]
    w_gb = w_in0[:, gate_col + D:gate_col + 2 * D]

    tq = min(512, T)
    ya = _attn(qkv, _attn_bias_tiles(rel_bias_table.astype(F32), tq), diff_lambda[0].astype(F32),
               diff_subln_w[0].astype(F32), tq)
    yb = _gdn(g4, gab, conv_w[0].astype(F32), gdn_a_log[0], gdn_dt_bias[0], gdn_norm_w[0])
    merged = _merge(ya, yb, xb, w_branch_a[0], w_branch_b[0], w_ga, w_gb)
    x1, x1b = _outln(merged, w_out[0].astype(BF16), x2, ln1_g[0], ln1_b[0])

    eidx, rank, gates, counts = _router(x1, w_router[0], router_bias[0])
    counts = counts.reshape(N_EXPERTS)
    starts = jnp.cumsum(counts) - counts
    dest = _dest(starts.astype(I32), eidx, rank)
    xs = _dispatch(dest, x1)
    ys = _gmm(xs, counts, w_gate, w_up, w_down)
    out = _combine(dest, ys, gates.T, x1, x1b, ws_gate[0].astype(BF16), ws_up[0].astype(BF16),
                   ws_down[0].astype(BF16), ln2_g[0], ln2_b[0])
    return out.reshape(B, T, D)
```

```python
import functools
import math

import jax
import jax.numpy as jnp
from jax import lax
from jax.experimental import pallas as pl
from jax.experimental.pallas import tpu as pltpu

F32 = jnp.float32
BF16 = jnp.bfloat16
I32 = jnp.int32

D_MODEL = 2048
CHUNK = 64
DIFF_HEADS = 8
DIFF_HEAD_DIM = 64
GDN_HEADS = 8
GDN_HEAD_DIM = 128
CONV_WIDTH = 4
REL_BUCKETS = 32
REL_MAX_DISTANCE = 128
N_EXPERTS = 64
TOP_K = 8
N_GROUPS = 8
TOPK_GROUPS = 4
EXPERT_FF = 512
ROUTED_SCALE = 2.5
DEPTH = 1
DEEPNORM_ALPHA = (2 * DEPTH) ** 0.25
LN_EPS = 1e-5
NORM_EPS = 1e-6
LAMBDA_INIT = 0.8 - 0.6 * math.exp(-0.3 * 0)

LANES = 128
MIB = 1024 * 1024
NEG_BIG = -1e30
LOG2E = math.log2(math.e)
GMM_ROWS = 256


def _params(semantics, vmem_mib):
    return pltpu.CompilerParams(dimension_semantics=semantics, vmem_limit_bytes=vmem_mib * MIB)


def _sigmoid(x):
    return 1.0 / (1.0 + jnp.exp(-x))


def _silu(x):
    return x * _sigmoid(x)


def _dot(a, b):
    return jnp.dot(a, b, preferred_element_type=F32)


def _dot_nt(a, b):
    return lax.dot_general(a, b, (((1,), (1,)), ((), ())), preferred_element_type=F32)


def _split3(a):
    hi = a.astype(BF16)
    r1 = a - hi.astype(F32)
    mid = r1.astype(BF16)
    lo = (r1 - mid.astype(F32)).astype(BF16)
    return hi, mid, lo


def _proj_kernel(x_ref, w_ref, o_ref, wbf_ref, *, tn):
    @pl.when(pl.program_id(1) == 0)
    def _():
        wbf_ref[...] = w_ref[...].astype(BF16)

    acc = _dot(x_ref[...], wbf_ref[...])
    for j in range(tn // LANES):
        o_ref[j] = acc[:, j * LANES:(j + 1) * LANES].astype(o_ref.dtype)


def _proj(xb, w, blk0, nblk, tn, odt):
    T, D = xb.shape
    tm = min(1024, T)
    return pl.pallas_call(
        functools.partial(_proj_kernel, tn=tn),
        grid=(nblk, T // tm),
        in_specs=[pl.BlockSpec((tm, D), lambda n, m: (m, 0)),
                  pl.BlockSpec((D, tn), lambda n, m: (0, blk0 + n))],
        out_specs=pl.BlockSpec((tn // LANES, tm, LANES), lambda n, m: (n, m, 0)),
        out_shape=jax.ShapeDtypeStruct((nblk * tn // LANES, T, LANES), odt),
        scratch_shapes=[pltpu.VMEM((D, tn), BF16)],
        compiler_params=_params(("arbitrary", "arbitrary"), 40),
        name="proj",
    )(xb, w)


def _attn_kernel(q_ref, k_ref, v_ref, band_ref, lam_ref, subw_ref, o_ref,
                 m1, l1, a1, m2, l2, a2, s_buf, bias_s, *, tq):
    qi = pl.program_id(1)

    @pl.when(qi == 0)
    def _():
        toep = pltpu.roll(jnp.broadcast_to(band_ref[0], (tq, 3 * tq)), 0, 1, stride=1, stride_axis=0)
        qpos = lax.broadcasted_iota(I32, (tq, tq), 0)
        kpos = lax.broadcasted_iota(I32, (tq, tq), 1)
        visible = (kpos // CHUNK) <= (qpos // CHUNK)
        bias_s[0] = jnp.where(visible, toep[:, 2 * tq:], NEG_BIG)
        bias_s[1] = toep[:, tq:2 * tq]

    q = q_ref[0].astype(F32) * (DIFF_HEAD_DIM ** -0.5 * LOG2E)
    ones = jnp.ones((tq, LANES), BF16)
    lane = lax.broadcasted_iota(I32, q.shape, 1)
    q1 = jnp.where(lane < DIFF_HEAD_DIM, q, 0.0).astype(BF16)
    q2 = jnp.where(lane >= DIFF_HEAD_DIM, q, 0.0).astype(BF16)
    for m, l, a in ((m1, l1, a1), (m2, l2, a2)):
        m[...] = jnp.full(m.shape, NEG_BIG, F32)
        l[...] = jnp.zeros(l.shape, F32)
        a[...] = jnp.zeros(a.shape, F32)

    def scores(j, slot):
        k = k_ref[0, pl.ds(pl.multiple_of(j * tq, tq), tq), :]
        s_buf[slot, 0] = _dot_nt(q1, k)
        s_buf[slot, 1] = _dot_nt(q2, k)

    def accumulate(j, slot, bias_idx):
        v = jnp.concatenate([v_ref[0, pl.ds(pl.multiple_of(j * tq, tq), tq), :], ones], axis=1)
        for i, (m, l, a) in enumerate(((m1, l1, a1), (m2, l2, a2))):
            s = s_buf[slot, i]
            if bias_idx is not None:
                s = s + bias_s[bias_idx]
            m_old = m[...]
            m_new = jnp.maximum(m_old, jnp.max(s, axis=1, keepdims=True))
            alpha = jnp.exp2(m_old - m_new)
            p = jnp.exp2(s - jnp.concatenate([m_new] * (tq // LANES), axis=1))
            pv = _dot(p.astype(BF16), v)
            l[...] = alpha * l[...] + pv[:, LANES:]
            a[...] = alpha * a[...] + pv[:, :LANES]
            m[...] = m_new

    n_far = jnp.maximum(qi - 1, 0)
    odd = lax.rem(n_far, 2)

    @pl.when(odd == 1)
    def _():
        scores(0, 1)
        scores(1, 0)
        accumulate(0, 1, None)

    @pl.when(odd == 0)
    def _():
        scores(0, 0)

    @pl.loop(0, n_far // 2)
    def _(t):
        base = odd + 2 * t
        scores(base + 1, 1)
        accumulate(base, 0, None)
        scores(base + 2, 0)
        accumulate(base + 1, 1, None)

    @pl.when(qi >= 1)
    def _():
        scores(qi, 1)
        accumulate(qi - 1, 0, 1)
        accumulate(qi, 1, 0)

    @pl.when(qi == 0)
    def _():
        accumulate(0, 0, 0)

    lf = lam_ref[...]
    lam = (jnp.exp(jnp.sum(lf[0:1] * lf[1:2], axis=1, keepdims=True))
           - jnp.exp(jnp.sum(lf[2:3] * lf[3:4], axis=1, keepdims=True)) + LAMBDA_INIT)
    o = a1[...] / l1[...] - lam * (a2[...] / l2[...])
    y = o * lax.rsqrt(jnp.mean(o * o, axis=1, keepdims=True) + NORM_EPS) * subw_ref[...] * (1.0 - LAMBDA_INIT)
    o_ref[...] = y.astype(o_ref.dtype)


def _t5_bucket(rel):
    nb = REL_BUCKETS // 2
    base = jnp.where(rel > 0, nb, 0)
    n = jnp.abs(rel)
    max_exact = nb // 2
    nf = jnp.maximum(n, 1).astype(F32)
    large = max_exact + (jnp.log(nf / max_exact) / math.log(REL_MAX_DISTANCE / max_exact)
                         * (nb - max_exact)).astype(I32)
    large = jnp.minimum(large, nb - 1)
    return base + jnp.where(n < max_exact, n, large)


def _attn_bias_band(rel_table, tq):
    span = 3 * tq
    rel = jnp.arange(span) - 2 * tq
    onehot = (_t5_bucket(rel)[:, None] == jnp.arange(REL_BUCKETS)[None, :]).astype(F32)
    far = rel_table[REL_BUCKETS // 2 - 1]
    band = (jnp.dot(onehot, rel_table, precision=lax.Precision.HIGHEST) - far).T
    return (band * LOG2E)[:, None, :]


def _attn(qkv, bias_band, lam_vecs, subln_w, tq):
    nb, T, _ = qkv.shape
    H = DIFF_HEADS
    return pl.pallas_call(
        functools.partial(_attn_kernel, tq=tq),
        grid=(H, T // tq),
        in_specs=[pl.BlockSpec((1, tq, LANES), lambda h, i: (h, i, 0)),
                  pl.BlockSpec((1, T, LANES), lambda h, i: (H + h, 0, 0)),
                  pl.BlockSpec((1, T, LANES), lambda h, i: (2 * H + h, 0, 0)),
                  pl.BlockSpec((1, 1, 3 * tq), lambda h, i: (h, 0, 0)),
                  pl.BlockSpec((4, DIFF_HEAD_DIM), lambda h, i: (0, 0)),
                  pl.BlockSpec((1, LANES), lambda h, i: (0, 0))],
        out_specs=pl.BlockSpec((tq, LANES), lambda h, i: (i, h)),
        out_shape=jax.ShapeDtypeStruct((T, H * LANES), BF16),
        scratch_shapes=[pltpu.VMEM((tq, LANES), F32)] * 6 + [pltpu.VMEM((2, 2, tq, tq), F32), pltpu.VMEM((2, tq, tq), F32)],
        compiler_params=_params(("arbitrary", "arbitrary"), 40),
        name="attn",
    )(qkv, qkv, qkv, bias_band, lam_vecs, subln_w.reshape(1, LANES))


def _gdn_kernel(g4_ref, gab_ref, convw_ref, alog_ref, dtb_ref, normw_ref, o_ref,
                s_ref, xbuf_ref):
    C = CHUNK
    H = GDN_HEADS
    n = pl.program_id(0)

    @pl.when(n == 0)
    def _():
        s_ref[...] = jnp.zeros(s_ref.shape, F32)
        xbuf_ref[...] = jnp.zeros(xbuf_ref.shape, F32)

    gab = gab_ref[0]
    sp = gab + dtb_ref[...]
    softplus = jnp.maximum(sp, 0.0) + jnp.log(1.0 + jnp.exp(-jnp.abs(sp)))
    g = -jnp.exp(alog_ref[...]) * softplus
    beta_all = _sigmoid(gab)
    row = lax.broadcasted_iota(I32, (C, LANES), 0)
    gc = g
    for sh in (1, 2, 4, 8, 16, 32):
        gc = gc + jnp.where(row >= sh, pltpu.roll(gc, sh, 0), 0.0)
    gc_t = gc.T

    ri = lax.broadcasted_iota(I32, (C, C), 0)
    ci = lax.broadcasted_iota(I32, (C, C), 1)
    tri = ri >= ci
    strict = ri > ci
    eye = (ri == ci).astype(F32)

    for c in range(3 * H):
        xbuf_ref[c, 8:8 + C, :] = g4_ref[c]

    def conv(c):
        acc = xbuf_ref[c, 8 - (CONV_WIDTH - 1):8 - (CONV_WIDTH - 1) + C, :] * convw_ref[0, c:c + 1, :]
        for j in range(1, CONV_WIDTH):
            lo = 8 - (CONV_WIDTH - 1) + j
            acc = acc + xbuf_ref[c, lo:lo + C, :] * convw_ref[j, c:c + 1, :]
        return _silu(acc)

    def l2n(x):
        return x * lax.rsqrt(jnp.sum(x * x, axis=1, keepdims=True) + NORM_EPS)

    scale = GDN_HEAD_DIM ** -0.5
    heads = range(H)
    q = [l2n(conv(h)) * scale for h in heads]
    k = [l2n(conv(H + h)) for h in heads]
    v = [conv(2 * H + h) for h in heads]
    beta = [beta_all[:, 8 + h:9 + h] for h in heads]
    gcol = [gc[:, h:h + 1] for h in heads]
    glast = [gc[C - 1:C, h:h + 1] for h in heads]
    eg = [jnp.exp(gcol[h]) for h in heads]
    decay = [jnp.exp(jnp.where(tri, gcol[h] - gc_t[h:h + 1, :], -jnp.inf)) for h in heads]
    kb = [k[h] * beta[h] for h in heads]
    prod = [_dot_nt(jnp.concatenate([kb[h], q[h]], axis=0).astype(BF16), k[h].astype(BF16))
            for h in heads]
    lmat = [jnp.where(strict, prod[h][:C] * decay[h], 0.0) for h in heads]
    aqk = [jnp.where(tri, prod[h][C:] * decay[h], 0.0).astype(BF16) for h in heads]
    tm = [eye - lmat[h] for h in heads]
    pw = lmat
    for _ in range(5):
        pwb = [pw[h].astype(BF16) for h in heads]
        pw = [_dot(pwb[h], pwb[h]) for h in heads]
        tm = [tm[h] + _dot(tm[h].astype(BF16), pw[h].astype(BF16)) for h in heads]
    uw = [_dot(tm[h].astype(BF16),
               jnp.concatenate([v[h] * beta[h], kb[h] * eg[h]], axis=1).astype(BF16)) for h in heads]
    s_old = [s_ref[h] for h in heads]
    sd = [_dot(jnp.concatenate([uw[h][:, LANES:], q[h] * eg[h]], axis=0).astype(BF16), s_old[h].astype(BF16))
          for h in heads]
    vnb = [(uw[h][:, :LANES] - sd[h][:C]).astype(BF16) for h in heads]
    o = [sd[h][C:] + _dot(aqk[h], vnb[h]) for h in heads]
    for h in heads:
        k_dec = (k[h] * jnp.exp(glast[h] - gcol[h])).astype(BF16)
        s_ref[h] = s_old[h] * jnp.exp(glast[h]) + lax.dot_general(
            k_dec, vnb[h], (((0,), (0,)), ((), ())), preferred_element_type=F32)
    for h in heads:
        on = o[h] * lax.rsqrt(jnp.mean(o[h] * o[h], axis=1, keepdims=True) + NORM_EPS) * normw_ref[...]
        o_ref[:, h * LANES:(h + 1) * LANES] = (on * _silu(g4_ref[3 * H + h])).astype(o_ref.dtype)

    for c in range(3 * H):
        xbuf_ref[c, 0:8, :] = g4_ref[c, C - 8:C, :]


def _gdn(g4, gab, conv_w, a_log, dt_bias, norm_w):
    nb, T, _ = g4.shape
    H = GDN_HEADS
    C = CHUNK
    pad = lambda v, off: jnp.zeros((1, LANES), F32).at[0, off:off + H].set(v.astype(F32))
    return pl.pallas_call(
        _gdn_kernel,
        grid=(T // C,),
        in_specs=[pl.BlockSpec((4 * H, C, LANES), lambda n: (0, n, 0)),
                  pl.BlockSpec((1, C, LANES), lambda n: (0, n, 0)),
                  pl.BlockSpec((CONV_WIDTH, 3 * H, LANES), lambda n: (0, 0, 0)),
                  pl.BlockSpec((1, LANES), lambda n: (0, 0)),
                  pl.BlockSpec((1, LANES), lambda n: (0, 0)),
                  pl.BlockSpec((1, LANES), lambda n: (0, 0))],
        out_specs=pl.BlockSpec((C, H * LANES), lambda n: (n, 0)),
        out_shape=jax.ShapeDtypeStruct((T, H * LANES), BF16),
        scratch_shapes=[pltpu.VMEM((H, GDN_HEAD_DIM, GDN_HEAD_DIM), F32),
                        pltpu.VMEM((3 * H, 8 + C, LANES), F32)],
        compiler_params=_params(("arbitrary",), 32),
        name="gdn",
    )(g4, gab, conv_w.reshape(CONV_WIDTH, 3 * H, LANES), pad(a_log, 0), pad(dt_bias, 0),
      norm_w.reshape(1, LANES).astype(F32))


def _merge_kernel(ya_ref, yb_ref, x_ref, wa_ref, wb_ref, wga_ref, wgb_ref, o_ref,
                  wa_s, wb_s, wga_s, wgb_s):
    @pl.when(pl.program_id(1) == 0)
    def _():
        wa_s[...] = wa_ref[...].astype(BF16)
        wb_s[...] = wb_ref[...].astype(BF16)
        wga_s[...] = wga_ref[...].astype(BF16)
        wgb_s[...] = wgb_ref[...].astype(BF16)

    x = x_ref[...]
    a = _dot(ya_ref[...], wa_s[...])
    b = _dot(yb_ref[...], wb_s[...])
    ga = _sigmoid(_dot(x, wga_s[...]))
    gb = _sigmoid(_dot(x, wgb_s[...]))
    o_ref[...] = (ga * a + gb * b).astype(o_ref.dtype)


def _merge(ya, yb, xb, w_a, w_b, w_ga, w_gb):
    T, D = xb.shape
    W = ya.shape[1]
    tm = min(512, T)
    tn = 512
    wspec = lambda rows: pl.BlockSpec((rows, tn), lambda n, m: (0, n))
    return pl.pallas_call(
        _merge_kernel,
        grid=(D // tn, T // tm),
        in_specs=[pl.BlockSpec((tm, W), lambda n, m: (m, 0)),
                  pl.BlockSpec((tm, W), lambda n, m: (m, 0)),
                  pl.BlockSpec((tm, D), lambda n, m: (m, 0)),
                  wspec(W), wspec(W), wspec(D), wspec(D)],
        out_specs=pl.BlockSpec((tm, tn), lambda n, m: (m, n)),
        out_shape=jax.ShapeDtypeStruct((T, D), BF16),
        scratch_shapes=[pltpu.VMEM((W, tn), BF16), pltpu.VMEM((W, tn), BF16),
                        pltpu.VMEM((D, tn), BF16), pltpu.VMEM((D, tn), BF16)],
        compiler_params=_params(("arbitrary", "arbitrary"), 48),
        name="merge",
    )(ya, yb, xb, w_a, w_b, w_ga, w_gb)


def _layer_norm(x, g, b):
    mu = jnp.mean(x, axis=1, keepdims=True)
    xc = x - mu
    var = jnp.mean(xc * xc, axis=1, keepdims=True)
    return xc * lax.rsqrt(var + LN_EPS) * g + b


def _outln_kernel(mg_ref, w_ref, x_ref, g_ref, b_ref, o_ref, ob_ref):
    h = _dot(mg_ref[...], w_ref[...])
    y = _layer_norm(DEEPNORM_ALPHA * x_ref[...] + h, g_ref[...], b_ref[...])
    o_ref[...] = y
    ob_ref[...] = y.astype(BF16)


def _outln(merged, w_out_bf, x, g, b):
    T, D = x.shape
    tm = min(512, T)
    row = lambda: pl.BlockSpec((tm, D), lambda m: (m, 0))
    vec = lambda: pl.BlockSpec((1, D), lambda m: (0, 0))
    return pl.pallas_call(
        _outln_kernel,
        grid=(T // tm,),
        in_specs=[row(), pl.BlockSpec((D, D), lambda m: (0, 0)), row(), vec(), vec()],
        out_specs=[row(), row()],
        out_shape=[jax.ShapeDtypeStruct((T, D), F32), jax.ShapeDtypeStruct((T, D), BF16)],
        compiler_params=_params(("arbitrary",), 48),
        name="outln",
    )(merged, w_out_bf, x, g.reshape(1, D), b.reshape(1, D))


def _router_kernel(x_ref, wr_ref, bias_ref, eidx_ref, rank_ref, gate_ref, cnt_ref,
                   run_ref, tri_ref, *, tm):
    G = N_GROUPS
    P = N_EXPERTS // N_GROUPS
    i = pl.program_id(0)

    @pl.when(i == 0)
    def _():
        run_ref[...] = jnp.zeros(run_ref.shape, F32)
        r = lax.broadcasted_iota(I32, (tm, tm), 0)
        c = lax.broadcasted_iota(I32, (tm, tm), 1)
        tri_ref[...] = jnp.where(r < c, 1.0, 0.0).astype(BF16)

    w3 = _split3(wr_ref[...])
    x3 = _split3(x_ref[...])
    logits = _dot_nt(w3[1], x3[1])
    for a, b in ((0, 2), (2, 0), (0, 1), (1, 0), (0, 0)):
        logits = logits + _dot_nt(w3[a], x3[b])
    scores = _sigmoid(logits)
    choice = scores + bias_ref[...]
    c3 = choice.reshape(G, P, tm)
    s3 = scores.reshape(G, P, tm)
    j_iota = lax.broadcasted_iota(I32, (G, P, tm), 1)
    g_iota = lax.broadcasted_iota(I32, (G, 1, tm), 0)
    e_iota = lax.broadcasted_iota(I32, (G, P, tm), 0) * P + j_iota
    ninf = -jnp.inf

    def amax3(v, idx, big):
        m = jnp.max(jnp.max(v, axis=1, keepdims=True), axis=0, keepdims=True)
        f = jnp.min(jnp.min(jnp.where(v == m, idx, big), axis=1, keepdims=True), axis=0, keepdims=True)
        return m, f

    m1 = jnp.max(c3, axis=1, keepdims=True)
    f1 = jnp.min(jnp.where(c3 == m1, j_iota, P), axis=1, keepdims=True)
    m2 = jnp.max(jnp.where(j_iota == f1, ninf, c3), axis=1, keepdims=True)
    gsc = m1 + m2
    gsel = jnp.zeros((G, 1, tm), jnp.bool_)
    for _ in range(TOPK_GROUPS):
        _, f = amax3(gsc, g_iota, G)
        hit = g_iota == f
        gsel = jnp.logical_or(gsel, hit)
        gsc = jnp.where(hit, ninf, gsc)
    cm = jnp.where(gsel, c3, ninf)

    picks = []
    wts = []
    msel = jnp.zeros((G, P, tm), F32)
    for _ in range(TOP_K):
        _, f = amax3(cm, e_iota, N_EXPERTS)
        hit = e_iota == f
        wts.append(jnp.sum(jnp.sum(jnp.where(hit, s3, 0.0), axis=1, keepdims=True), axis=0, keepdims=True))
        picks.append(f)
        msel = jnp.where(hit, 1.0, msel)
        cm = jnp.where(hit, ninf, cm)
    wsum = wts[0]
    for w in wts[1:]:
        wsum = wsum + w

    m2d = msel.reshape(N_EXPERTS, tm)
    before = run_ref[...] + _dot(m2d.astype(BF16), tri_ref[...])
    b3 = before.reshape(G, P, tm)
    for k in range(TOP_K):
        hit = e_iota == picks[k]
        rk = jnp.sum(jnp.sum(jnp.where(hit, b3, 0.0), axis=1, keepdims=True), axis=0, keepdims=True)
        eidx_ref[k:k + 1, :] = picks[k].reshape(1, tm)
        rank_ref[k:k + 1, :] = rk.reshape(1, tm).astype(I32)
        gate_ref[k:k + 1, :] = (wts[k] / wsum * ROUTED_SCALE).reshape(1, tm)
    run_ref[...] = run_ref[...] + jnp.sum(m2d, axis=1, keepdims=True)
    cnt_ref[...] = run_ref[...].astype(I32)


def _router(x1, w_router, router_bias):
    T, D = x1.shape
    tm = min(512, T)
    E = N_EXPERTS
    tok = lambda: pl.BlockSpec((TOP_K, tm), lambda i: (0, i))
    return pl.pallas_call(
        functools.partial(_router_kernel, tm=tm),
        grid=(T // tm,),
        in_specs=[pl.BlockSpec((tm, D), lambda i: (i, 0)),
                  pl.BlockSpec((E, D), lambda i: (0, 0)),
                  pl.BlockSpec((E, 1), lambda i: (0, 0))],
        out_specs=[tok(), tok(), tok(), pl.BlockSpec((E, 1), lambda i: (0, 0))],
        out_shape=[jax.ShapeDtypeStruct((TOP_K, T), I32), jax.ShapeDtypeStruct((TOP_K, T), I32),
                   jax.ShapeDtypeStruct((TOP_K, T), F32), jax.ShapeDtypeStruct((E, 1), I32)],
        scratch_shapes=[pltpu.VMEM((E, 1), F32), pltpu.VMEM((tm, tm), BF16)],
        compiler_params=_params(("arbitrary",), 32),
        name="router",
    )(x1, w_router.astype(F32).T, router_bias.astype(F32).reshape(E, 1))


def _dest_kernel(starts_ref, eidx_ref, rank_ref, o_ref):
    e = eidx_ref[...]
    acc = rank_ref[...]
    for j in range(N_EXPERTS):
        acc = acc + jnp.where(e == j, starts_ref[j], 0)
    o_ref[...] = acc


def _dest(starts, eidx, rank):
    full = lambda: pl.BlockSpec(eidx.shape, lambda i, s: (0, 0))
    return pl.pallas_call(
        _dest_kernel,
        grid_spec=pltpu.PrefetchScalarGridSpec(num_scalar_prefetch=1, grid=(1,),
                                               in_specs=[full(), full()], out_specs=full()),
        out_shape=jax.ShapeDtypeStruct(eidx.shape, I32),
        compiler_params=_params(("arbitrary",), 16),
        name="dest",
    )(starts, eidx, rank)


def _dispatch_kernel(dest_ref, x_ref, xs_hbm, sem, *, tm):
    def row_copy(t, k):
        return pltpu.make_async_copy(x_ref.at[pl.ds(t, 1)], xs_hbm.at[pl.ds(dest_ref[k, t], 1)], sem)

    def issue(t, carry):
        for k in range(TOP_K):
            row_copy(t, k).start()
        return carry

    def drain(t, carry):
        for k in range(TOP_K):
            row_copy(t, k).wait()
        return carry

    lax.fori_loop(0, tm, issue, 0)
    lax.fori_loop(0, tm, drain, 0)


def _dispatch(dest, x1):
    T, D = x1.shape
    tm = min(128, T)
    return pl.pallas_call(
        functools.partial(_dispatch_kernel, tm=tm),
        grid=(T // tm,),
        in_specs=[pl.BlockSpec((TOP_K, tm), lambda i: (0, i), memory_space=pltpu.SMEM),
                  pl.BlockSpec((tm, D), lambda i: (i, 0))],
        out_specs=pl.BlockSpec(memory_space=pl.ANY),
        out_shape=jax.ShapeDtypeStruct((T * TOP_K, D), x1.dtype),
        scratch_shapes=[pltpu.SemaphoreType.DMA(())],
        compiler_params=_params(("arbitrary",), 16),
        name="dispatch",
    )(dest, x1)


def _gmm_kernel(vblk, vexp, vfirst, vchg, vslot, vnext, gstart, nvis, xs_ref, wg_hbm, wu_hbm, wd_hbm, o_ref,
                wg_buf, wu_buf, wd_buf, wg_s, wu_s, wd_s, sem):
    i = pl.program_id(0)
    e = vexp[i]

    def weight_copies(expert, slot):
        return [pltpu.make_async_copy(w_hbm.at[0, expert], buf.at[slot], sem.at[slot, j])
                for j, (w_hbm, buf) in enumerate(((wg_hbm, wg_buf), (wu_hbm, wu_buf), (wd_hbm, wd_buf)))]

    @pl.when(i == 0)
    def _():
        for c in weight_copies(e, vslot[0]):
            c.start()

    @pl.when(vchg[i] == 1)
    def _():
        slot = vslot[i]
        for c in weight_copies(e, slot):
            c.wait()

        @pl.when(vnext[i] >= 0)
        def _():
            for c in weight_copies(vnext[i], 1 - slot):
                c.start()

        wg_s[...] = wg_buf[slot].astype(BF16)
        wu_s[...] = wu_buf[slot].astype(BF16)
        wd_s[...] = wd_buf[slot].astype(BF16)

    @pl.when(i < nvis[0])
    def _():
        x = xs_ref[...].astype(BF16)
        g = _dot(x, wg_s[...])
        u = _dot(x, wu_s[...])
        y = _dot((_silu(g) * u).astype(BF16), wd_s[...])
        row = vblk[i] * GMM_ROWS + lax.broadcasted_iota(I32, (GMM_ROWS, 1), 0)
        mine = jnp.logical_and(row >= gstart[e], row < gstart[e + 1])

        @pl.when(vfirst[i] == 1)
        def _():
            o_ref[...] = jnp.where(mine, y, 0.0)

        @pl.when(vfirst[i] == 0)
        def _():
            o_ref[...] = jnp.where(mine, y, o_ref[...])


def _gmm_plan(counts, n_rows):
    E = N_EXPERTS
    nblk = n_rows // GMM_ROWS
    nvis_max = nblk + E
    ends = jnp.cumsum(counts)
    gstart = jnp.concatenate([jnp.zeros((1,), I32), ends]).astype(I32)
    count_le = lambda table, x: jnp.sum((table[None, :] <= x[:, None]).astype(I32), axis=1)
    take = lambda table, idx: jnp.sum(jnp.where(idx[:, None] == jnp.arange(table.shape[0], dtype=I32)[None, :],
                                                table[None, :], 0), axis=1)
    blk_lo = jnp.arange(nblk, dtype=I32) * GMM_ROWS
    e_lo = count_le(ends, blk_lo)
    e_hi = count_le(ends, blk_lo + (GMM_ROWS - 1))
    per_blk = e_hi - e_lo + 1
    vend = jnp.cumsum(per_blk).astype(I32)
    vbeg = vend - per_blk
    nvis = vend[-1]
    vi = jnp.minimum(jnp.arange(nvis_max, dtype=I32), nvis - 1)
    vblk = count_le(vend, vi)
    vbeg_v = take(vbeg, vblk)
    vexp = jnp.minimum(take(e_lo, vblk) + (vi - vbeg_v), E - 1)
    vfirst = jnp.logical_and(vi == vbeg_v, jnp.arange(nvis_max) < nvis).astype(I32)
    vchg = jnp.concatenate([jnp.ones((1,), I32), (vexp[1:] != vexp[:-1]).astype(I32)])
    vslot = (jnp.cumsum(vchg) - 1) % 2
    pos = jnp.where(vchg == 1, jnp.arange(nvis_max, dtype=I32), nvis_max)
    nxt = jnp.concatenate([lax.cummin(pos[::-1])[::-1][1:], jnp.full((1,), nvis_max, I32)])
    vnext = jnp.where(nxt < nvis_max, take(vexp, jnp.minimum(nxt, nvis_max - 1)), -1)
    sched = (vblk, vexp, vfirst, vchg, vslot.astype(I32), vnext.astype(I32), gstart, nvis.reshape(1).astype(I32))
    return sched, nvis_max


def _gmm(xs, counts, w_gate, w_up, w_down):
    R, D = xs.shape
    FF = EXPERT_FF
    sched, nvis_max = _gmm_plan(counts, R)
    rows = lambda: pl.BlockSpec((GMM_ROWS, D), lambda i, vb, *_: (vb[i], 0))
    hbm = lambda: pl.BlockSpec(memory_space=pl.ANY)
    grid_spec = pltpu.PrefetchScalarGridSpec(
        num_scalar_prefetch=len(sched),
        grid=(nvis_max,),
        in_specs=[rows(), hbm(), hbm(), hbm()],
        out_specs=rows(),
        scratch_shapes=[pltpu.VMEM((2, D, FF), F32), pltpu.VMEM((2, D, FF), F32), pltpu.VMEM((2, FF, D), F32),
                        pltpu.VMEM((D, FF), BF16), pltpu.VMEM((D, FF), BF16), pltpu.VMEM((FF, D), BF16),
                        pltpu.SemaphoreType.DMA((2, 3))],
    )
    return pl.pallas_call(
        _gmm_kernel,
        grid_spec=grid_spec,
        out_shape=jax.ShapeDtypeStruct((R, D), F32),
        compiler_params=_params(("arbitrary",), 56),
        name="gmm",
    )(*sched, xs, w_gate, w_up, w_down)


def _combine_kernel(dest_ref, ys_hbm, gate_ref, x_ref, xb_ref, wsg_ref, wsu_ref, wsd_ref, g_ref, b_ref,
                    o_ref, buf_ref, sem, *, tm):
    def row_copy(t, k):
        return pltpu.make_async_copy(ys_hbm.at[pl.ds(dest_ref[k, t], 1)],
                                     buf_ref.at[k, pl.ds(t, 1)], sem)

    def issue(t, carry):
        for k in range(TOP_K):
            row_copy(t, k).start()
        return carry

    def drain(t, carry):
        for k in range(TOP_K):
            row_copy(t, k).wait()
        return carry

    lax.fori_loop(0, tm, issue, 0)
    xb = xb_ref[...]
    hidden = _silu(_dot(xb, wsg_ref[...])) * _dot(xb, wsu_ref[...])
    acc = DEEPNORM_ALPHA * x_ref[...] + _dot(hidden.astype(BF16), wsd_ref[...])
    lax.fori_loop(0, tm, drain, 0)
    gates = gate_ref[...]
    for k in range(TOP_K):
        acc = acc + gates[:, k:k + 1] * buf_ref[k]
    o_ref[...] = _layer_norm(acc, g_ref[...], b_ref[...])


def _combine(dest, ys, gates_t, x1, x1b, wsg, wsu, wsd, g, b):
    T, D = x1.shape
    tm = min(128, T)
    FF = wsg.shape[1]
    row = lambda: pl.BlockSpec((tm, D), lambda i: (i, 0))
    vec = lambda: pl.BlockSpec((1, D), lambda i: (0, 0))
    return pl.pallas_call(
        functools.partial(_combine_kernel, tm=tm),
        grid=(T // tm,),
        in_specs=[pl.BlockSpec((TOP_K, tm), lambda i: (0, i), memory_space=pltpu.SMEM),
                  pl.BlockSpec(memory_space=pl.ANY),
                  pl.BlockSpec((tm, TOP_K), lambda i: (i, 0)),
                  row(), row(),
                  pl.BlockSpec((D, FF), lambda i: (0, 0)),
                  pl.BlockSpec((D, FF), lambda i: (0, 0)),
                  pl.BlockSpec((FF, D), lambda i: (0, 0)),
                  vec(), vec()],
        out_specs=row(),
        out_shape=jax.ShapeDtypeStruct((T, D), F32),
        scratch_shapes=[pltpu.VMEM((TOP_K, tm, D), F32), pltpu.SemaphoreType.DMA(())],
        compiler_params=_params(("arbitrary",), 40),
        name="combine",
    )(dest, ys, gates_t, x1, x1b, wsg, wsu, wsd, g.reshape(1, D), b.reshape(1, D))


def kernel(x, w_in, conv_w, gdn_a_log, gdn_dt_bias, gdn_norm_w, diff_lambda, diff_subln_w, rel_bias_table,
           w_branch_a, w_branch_b, w_out, ln1_g, ln1_b, w_router, router_bias, w_gate, w_up, w_down,
           ws_gate, ws_up, ws_down, ln2_g, ln2_b):
    B, T, D = x.shape
    assert B == 1 and D == D_MODEL
    x2 = x.reshape(T, D)
    xb = x2.astype(BF16)
    w_in0 = w_in[0]
    H = DIFF_HEADS

    qkv = _proj(xb, w_in0, 0, 6, 512, BF16)
    g4 = _proj(xb, w_in0, 6, 8, 512, F32)
    gab = _proj(xb, w_in0, 56, 1, LANES, F32)
    gate_col = 7 * 1024 + 2 * GDN_HEADS
    w_ga = w_in0[:, gate_col:gate_col + D]
    w_gb = w_in0[:, gate_col + D:gate_col + 2 * D]

    tq = min(512, T)
    ya = _attn(qkv, _attn_bias_band(rel_bias_table.astype(F32), tq), diff_lambda[0].astype(F32),
               diff_subln_w[0].astype(F32), tq)
    yb = _gdn(g4, gab, conv_w[0].astype(F32), gdn_a_log[0], gdn_dt_bias[0], gdn_norm_w[0])
    merged = _merge(ya, yb, xb, w_branch_a[0], w_branch_b[0], w_ga, w_gb)
    x1, x1b = _outln(merged, w_out[0].astype(BF16), x2, ln1_g[0], ln1_b[0])

    eidx, rank, gates, counts = _router(x1, w_router[0], router_bias[0])
    counts = counts.reshape(N_EXPERTS)
    starts = jnp.cumsum(counts) - counts
    dest = _dest(starts.astype(I32), eidx, rank)
    xs = _dispatch(dest, x1)
    ys = _gmm(xs, counts, w_gate, w_up, w_down)
    out = _combine(dest, ys, gates.T, x1, x1b, ws_gate[0].astype(BF16), ws_up[0].astype(BF16),
                   ws_down[0].astype(BF16), ln2_g[0], ln2_b[0])
    return out.reshape(B, T, D)
```

```python
import functools
import math

import jax
import jax.numpy as jnp
from jax import lax
from jax.experimental import pallas as pl
from jax.experimental.pallas import tpu as pltpu

F32 = jnp.float32
BF16 = jnp.bfloat16
I32 = jnp.int32

D_MODEL = 2048
CHUNK = 64
DIFF_HEADS = 8
DIFF_HEAD_DIM = 64
GDN_HEADS = 8
GDN_HEAD_DIM = 128
CONV_WIDTH = 4
REL_BUCKETS = 32
REL_MAX_DISTANCE = 128
N_EXPERTS = 64
TOP_K = 8
N_GROUPS = 8
TOPK_GROUPS = 4
EXPERT_FF = 512
ROUTED_SCALE = 2.5
DEPTH = 1
DEEPNORM_ALPHA = (2 * DEPTH) ** 0.25
LN_EPS = 1e-5
NORM_EPS = 1e-6
LAMBDA_INIT = 0.8 - 0.6 * math.exp(-0.3 * 0)

LANES = 128
MIB = 1024 * 1024
NEG_BIG = -1e30
LOG2E = math.log2(math.e)
GMM_ROWS = 256


def _params(semantics, vmem_mib):
    return pltpu.CompilerParams(dimension_semantics=semantics, vmem_limit_bytes=vmem_mib * MIB)


def _sigmoid(x):
    return 1.0 / (1.0 + jnp.exp(-x))


def _silu(x):
    return x * _sigmoid(x)


def _dot(a, b):
    return jnp.dot(a, b, preferred_element_type=F32)


def _dot_nt(a, b):
    return lax.dot_general(a, b, (((1,), (1,)), ((), ())), preferred_element_type=F32)


def _pack_halves(y):
    n = y.shape[1] // 2
    bits = lambda v: lax.bitcast_convert_type(v.astype(BF16).astype(F32), jnp.uint32)
    return (bits(y[:, :n]) >> 16) | (bits(y[:, n:]) & jnp.uint32(0xFFFF0000))


def _unpack_halves(p):
    lo = lax.bitcast_convert_type(p << 16, F32)
    hi = lax.bitcast_convert_type(p & jnp.uint32(0xFFFF0000), F32)
    return jnp.concatenate([lo, hi], axis=1)


def _split3(a):
    hi = a.astype(BF16)
    r1 = a - hi.astype(F32)
    mid = r1.astype(BF16)
    lo = (r1 - mid.astype(F32)).astype(BF16)
    return hi, mid, lo


def _proj_kernel(x_ref, w_ref, o_ref, wbf_ref, *, tn):
    @pl.when(pl.program_id(1) == 0)
    def _():
        wbf_ref[...] = w_ref[...].astype(BF16)

    acc = _dot_nt(x_ref[...], wbf_ref[...])
    for j in range(tn // LANES):
        o_ref[j] = acc[:, j * LANES:(j + 1) * LANES].astype(o_ref.dtype)


def _proj(xb, wt, blk0, nblk, tn, odt):
    T, D = xb.shape
    tm = min(1024, T)
    return pl.pallas_call(
        functools.partial(_proj_kernel, tn=tn),
        grid=(nblk, T // tm),
        in_specs=[pl.BlockSpec((tm, D), lambda n, m: (m, 0)),
                  pl.BlockSpec((None, tn, D), lambda n, m: (0, blk0 + n, 0))],
        out_specs=pl.BlockSpec((tn // LANES, tm, LANES), lambda n, m: (n, m, 0)),
        out_shape=jax.ShapeDtypeStruct((nblk * tn // LANES, T, LANES), odt),
        scratch_shapes=[pltpu.VMEM((tn, D), BF16)],
        compiler_params=_params(("arbitrary", "arbitrary"), 40),
        name="proj",
    )(xb, wt)


def _attn_kernel(q_ref, k_ref, v_ref, band_ref, lam_ref, subw_ref, o_ref,
                 m1, l1, a1, m2, l2, a2, s_buf, bias_s, *, tq):
    qi = pl.program_id(1)

    @pl.when(qi == 0)
    def _():
        toep = pltpu.roll(jnp.broadcast_to(band_ref[0], (tq, 3 * tq)), 0, 1, stride=1, stride_axis=0)
        qpos = lax.broadcasted_iota(I32, (tq, tq), 0)
        kpos = lax.broadcasted_iota(I32, (tq, tq), 1)
        visible = (kpos // CHUNK) <= (qpos // CHUNK)
        bias_s[0] = jnp.where(visible, toep[:, 2 * tq:], NEG_BIG)
        bias_s[1] = toep[:, tq:2 * tq]

    q = q_ref[0].astype(F32) * (DIFF_HEAD_DIM ** -0.5 * LOG2E)
    ones = jnp.ones((tq, LANES), BF16)
    lane = lax.broadcasted_iota(I32, q.shape, 1)
    q1 = jnp.where(lane < DIFF_HEAD_DIM, q, 0.0).astype(BF16)
    q2 = jnp.where(lane >= DIFF_HEAD_DIM, q, 0.0).astype(BF16)
    for m, l, a in ((m1, l1, a1), (m2, l2, a2)):
        m[...] = jnp.full(m.shape, NEG_BIG, F32)
        l[...] = jnp.zeros(l.shape, F32)
        a[...] = jnp.zeros(a.shape, F32)

    def scores(j, slot):
        k = k_ref[0, pl.ds(pl.multiple_of(j * tq, tq), tq), :]
        s_buf[slot, 0] = _dot_nt(q1, k)
        s_buf[slot, 1] = _dot_nt(q2, k)

    def accumulate(j, slot, bias_idx):
        v = jnp.concatenate([v_ref[0, pl.ds(pl.multiple_of(j * tq, tq), tq), :], ones], axis=1)
        for i, (m, l, a) in enumerate(((m1, l1, a1), (m2, l2, a2))):
            s = s_buf[slot, i]
            if bias_idx is not None:
                s = s + bias_s[bias_idx]
            m_old = m[...]
            m_new = jnp.maximum(m_old, jnp.max(s, axis=1, keepdims=True))
            alpha = jnp.exp2(m_old - m_new)
            p = jnp.exp2(s - jnp.concatenate([m_new] * (tq // LANES), axis=1))
            pv = _dot(p.astype(BF16), v)
            l[...] = alpha * l[...] + pv[:, LANES:]
            a[...] = alpha * a[...] + pv[:, :LANES]
            m[...] = m_new

    n_far = jnp.maximum(qi - 1, 0)
    odd = lax.rem(n_far, 2)

    @pl.when(odd == 1)
    def _():
        scores(0, 1)
        scores(1, 0)
        accumulate(0, 1, None)

    @pl.when(odd == 0)
    def _():
        scores(0, 0)

    @pl.loop(0, n_far // 2)
    def _(t):
        base = odd + 2 * t
        scores(base + 1, 1)
        accumulate(base, 0, None)
        scores(base + 2, 0)
        accumulate(base + 1, 1, None)

    @pl.when(qi >= 1)
    def _():
        scores(qi, 1)
        accumulate(qi - 1, 0, 1)
        accumulate(qi, 1, 0)

    @pl.when(qi == 0)
    def _():
        accumulate(0, 0, 0)

    lf = lam_ref[...]
    lam = (jnp.exp(jnp.sum(lf[0:1] * lf[1:2], axis=1, keepdims=True))
           - jnp.exp(jnp.sum(lf[2:3] * lf[3:4], axis=1, keepdims=True)) + LAMBDA_INIT)
    o = a1[...] / l1[...] - lam * (a2[...] / l2[...])
    y = o * lax.rsqrt(jnp.mean(o * o, axis=1, keepdims=True) + NORM_EPS) * subw_ref[...] * (1.0 - LAMBDA_INIT)
    o_ref[...] = y.astype(o_ref.dtype)


def _t5_bucket(rel):
    nb = REL_BUCKETS // 2
    base = jnp.where(rel > 0, nb, 0)
    n = jnp.abs(rel)
    max_exact = nb // 2
    nf = jnp.maximum(n, 1).astype(F32)
    large = max_exact + (jnp.log(nf / max_exact) / math.log(REL_MAX_DISTANCE / max_exact)
                         * (nb - max_exact)).astype(I32)
    large = jnp.minimum(large, nb - 1)
    return base + jnp.where(n < max_exact, n, large)


def _attn_bias_band(rel_table, tq):
    span = 3 * tq
    rel = jnp.arange(span) - 2 * tq
    onehot = (_t5_bucket(rel)[:, None] == jnp.arange(REL_BUCKETS)[None, :]).astype(F32)
    far = rel_table[REL_BUCKETS // 2 - 1]
    band = (jnp.dot(onehot, rel_table, precision=lax.Precision.HIGHEST) - far).T
    return (band * LOG2E)[:, None, :]


def _attn(qkv, bias_band, lam_vecs, subln_w, tq):
    nb, T, _ = qkv.shape
    H = DIFF_HEADS
    return pl.pallas_call(
        functools.partial(_attn_kernel, tq=tq),
        grid=(H, T // tq),
        in_specs=[pl.BlockSpec((1, tq, LANES), lambda h, i: (h, i, 0)),
                  pl.BlockSpec((1, T, LANES), lambda h, i: (H + h, 0, 0)),
                  pl.BlockSpec((1, T, LANES), lambda h, i: (2 * H + h, 0, 0)),
                  pl.BlockSpec((1, 1, 3 * tq), lambda h, i: (h, 0, 0)),
                  pl.BlockSpec((4, DIFF_HEAD_DIM), lambda h, i: (0, 0)),
                  pl.BlockSpec((1, LANES), lambda h, i: (0, 0))],
        out_specs=pl.BlockSpec((tq, LANES), lambda h, i: (i, h)),
        out_shape=jax.ShapeDtypeStruct((T, H * LANES), BF16),
        scratch_shapes=[pltpu.VMEM((tq, LANES), F32)] * 6 + [pltpu.VMEM((2, 2, tq, tq), F32), pltpu.VMEM((2, tq, tq), F32)],
        compiler_params=_params(("arbitrary", "arbitrary"), 40),
        name="attn",
    )(qkv, qkv, qkv, bias_band, lam_vecs, subln_w.reshape(1, LANES))


def _gdn_kernel(g4_ref, gab_ref, convw_ref, alog_ref, dtb_ref, normw_ref, o_ref,
                s_ref, xbuf_ref):
    C = CHUNK
    H = GDN_HEADS
    n = pl.program_id(0)

    @pl.when(n == 0)
    def _():
        s_ref[...] = jnp.zeros(s_ref.shape, F32)
        xbuf_ref[...] = jnp.zeros(xbuf_ref.shape, F32)

    gab = gab_ref[0]
    sp = gab + dtb_ref[...]
    softplus = jnp.maximum(sp, 0.0) + jnp.log(1.0 + jnp.exp(-jnp.abs(sp)))
    g = -jnp.exp(alog_ref[...]) * softplus
    beta_all = _sigmoid(gab)
    row = lax.broadcasted_iota(I32, (C, LANES), 0)
    gc = g
    for sh in (1, 2, 4, 8, 16, 32):
        gc = gc + jnp.where(row >= sh, pltpu.roll(gc, sh, 0), 0.0)
    gc_t = gc.T

    ri = lax.broadcasted_iota(I32, (C, C), 0)
    ci = lax.broadcasted_iota(I32, (C, C), 1)
    tri = ri >= ci
    strict = ri > ci
    eye = (ri == ci).astype(F32)

    for c in range(3 * H):
        xbuf_ref[c, 8:8 + C, :] = g4_ref[c]

    def conv(c):
        acc = xbuf_ref[c, 8 - (CONV_WIDTH - 1):8 - (CONV_WIDTH - 1) + C, :] * convw_ref[0, c:c + 1, :]
        for j in range(1, CONV_WIDTH):
            lo = 8 - (CONV_WIDTH - 1) + j
            acc = acc + xbuf_ref[c, lo:lo + C, :] * convw_ref[j, c:c + 1, :]
        return _silu(acc)

    def l2n(x):
        return x * lax.rsqrt(jnp.sum(x * x, axis=1, keepdims=True) + NORM_EPS)

    scale = GDN_HEAD_DIM ** -0.5
    heads = range(H)
    q = [l2n(conv(h)) * scale for h in heads]
    k = [l2n(conv(H + h)) for h in heads]
    v = [conv(2 * H + h) for h in heads]
    beta = [beta_all[:, 8 + h:9 + h] for h in heads]
    gcol = [gc[:, h:h + 1] for h in heads]
    glast = [gc[C - 1:C, h:h + 1] for h in heads]
    eg = [jnp.exp(gcol[h]) for h in heads]
    decay = [jnp.exp(jnp.where(tri, gcol[h] - gc_t[h:h + 1, :], -jnp.inf)) for h in heads]
    kb = [k[h] * beta[h] for h in heads]
    prod = [_dot_nt(jnp.concatenate([kb[h], q[h]], axis=0).astype(BF16), k[h].astype(BF16))
            for h in heads]
    lmat = [jnp.where(strict, prod[h][:C] * decay[h], 0.0) for h in heads]
    aqk = [jnp.where(tri, prod[h][C:] * decay[h], 0.0).astype(BF16) for h in heads]
    tm = [eye - lmat[h] for h in heads]
    pw = lmat
    for _ in range(5):
        pwb = [pw[h].astype(BF16) for h in heads]
        pw = [_dot(pwb[h], pwb[h]) for h in heads]
        tm = [tm[h] + _dot(tm[h].astype(BF16), pw[h].astype(BF16)) for h in heads]
    uw = [_dot(tm[h].astype(BF16),
               jnp.concatenate([v[h] * beta[h], kb[h] * eg[h]], axis=1).astype(BF16)) for h in heads]
    s_old = [s_ref[h] for h in heads]
    sd = [_dot(jnp.concatenate([uw[h][:, LANES:], q[h] * eg[h]], axis=0).astype(BF16), s_old[h].astype(BF16))
          for h in heads]
    vnb = [(uw[h][:, :LANES] - sd[h][:C]).astype(BF16) for h in heads]
    o = [sd[h][C:] + _dot(aqk[h], vnb[h]) for h in heads]
    for h in heads:
        k_dec = (k[h] * jnp.exp(glast[h] - gcol[h])).astype(BF16)
        s_ref[h] = s_old[h] * jnp.exp(glast[h]) + lax.dot_general(
            k_dec, vnb[h], (((0,), (0,)), ((), ())), preferred_element_type=F32)
    for h in heads:
        on = o[h] * lax.rsqrt(jnp.mean(o[h] * o[h], axis=1, keepdims=True) + NORM_EPS) * normw_ref[...]
        o_ref[:, h * LANES:(h + 1) * LANES] = (on * _silu(g4_ref[3 * H + h])).astype(o_ref.dtype)

    for c in range(3 * H):
        xbuf_ref[c, 0:8, :] = g4_ref[c, C - 8:C, :]


def _gdn(g4, gab, conv_w, a_log, dt_bias, norm_w):
    nb, T, _ = g4.shape
    H = GDN_HEADS
    C = CHUNK
    pad = lambda v, off: jnp.zeros((1, LANES), F32).at[0, off:off + H].set(v.astype(F32))
    return pl.pallas_call(
        _gdn_kernel,
        grid=(T // C,),
        in_specs=[pl.BlockSpec((4 * H, C, LANES), lambda n: (0, n, 0)),
                  pl.BlockSpec((1, C, LANES), lambda n: (0, n, 0)),
                  pl.BlockSpec((CONV_WIDTH, 3 * H, LANES), lambda n: (0, 0, 0)),
                  pl.BlockSpec((1, LANES), lambda n: (0, 0)),
                  pl.BlockSpec((1, LANES), lambda n: (0, 0)),
                  pl.BlockSpec((1, LANES), lambda n: (0, 0))],
        out_specs=pl.BlockSpec((C, H * LANES), lambda n: (n, 0)),
        out_shape=jax.ShapeDtypeStruct((T, H * LANES), BF16),
        scratch_shapes=[pltpu.VMEM((H, GDN_HEAD_DIM, GDN_HEAD_DIM), F32),
                        pltpu.VMEM((3 * H, 8 + C, LANES), F32)],
        compiler_params=_params(("arbitrary",), 32),
        name="gdn",
    )(g4, gab, conv_w.reshape(CONV_WIDTH, 3 * H, LANES), pad(a_log, 0), pad(dt_bias, 0),
      norm_w.reshape(1, LANES).astype(F32))


GATE_COL = 7 * 1024 + 2 * GDN_HEADS


def _merge_kernel(ya_ref, yb_ref, x_ref, wa_ref, wb_ref, wga_ref, wgb_ref, o_ref,
                  wa_s, wb_s, wga_s, wgb_s):
    @pl.when(pl.program_id(1) == 0)
    def _():
        wa_s[...] = wa_ref[...].astype(BF16)
        wb_s[...] = wb_ref[...].astype(BF16)
        wga_s[...] = wga_ref[0].astype(BF16)
        wgb_s[...] = wgb_ref[0].astype(BF16)

    x = x_ref[...]
    a = _dot(ya_ref[...], wa_s[...])
    b = _dot(yb_ref[...], wb_s[...])
    ga = _sigmoid(_dot_nt(x, wga_s[...]))
    gb = _sigmoid(_dot_nt(x, wgb_s[...]))
    o_ref[...] = (ga * a + gb * b).astype(o_ref.dtype)


def _merge(ya, yb, xb, w_a, w_b, w_in_t):
    T, D = xb.shape
    W = ya.shape[1]
    tm = min(512, T)
    tn = 512
    wspec = lambda rows: pl.BlockSpec((rows, tn), lambda n, m: (0, n))
    gspec = lambda off: pl.BlockSpec((pl.Element(1), pl.Element(tn), pl.Element(D)),
                                     lambda n, m: (0, pl.multiple_of(GATE_COL + off + n * tn, 16), 0))
    return pl.pallas_call(
        _merge_kernel,
        grid=(D // tn, T // tm),
        in_specs=[pl.BlockSpec((tm, W), lambda n, m: (m, 0)),
                  pl.BlockSpec((tm, W), lambda n, m: (m, 0)),
                  pl.BlockSpec((tm, D), lambda n, m: (m, 0)),
                  wspec(W), wspec(W), gspec(0), gspec(D)],
        out_specs=pl.BlockSpec((tm, tn), lambda n, m: (m, n)),
        out_shape=jax.ShapeDtypeStruct((T, D), BF16),
        scratch_shapes=[pltpu.VMEM((W, tn), BF16), pltpu.VMEM((W, tn), BF16),
                        pltpu.VMEM((tn, D), BF16), pltpu.VMEM((tn, D), BF16)],
        compiler_params=_params(("arbitrary", "arbitrary"), 48),
        name="merge",
    )(ya, yb, xb, w_a, w_b, w_in_t, w_in_t)


def _layer_norm(x, g, b):
    mu = jnp.mean(x, axis=1, keepdims=True)
    xc = x - mu
    var = jnp.mean(xc * xc, axis=1, keepdims=True)
    return xc * lax.rsqrt(var + LN_EPS) * g + b


def _outln_kernel(mg_ref, w_ref, x_ref, g_ref, b_ref, o_ref, ob_ref, op_ref):
    h = _dot(mg_ref[...], w_ref[...])
    y = _layer_norm(DEEPNORM_ALPHA * x_ref[...] + h, g_ref[...], b_ref[...])
    o_ref[...] = y
    ob_ref[...] = y.astype(BF16)
    op_ref[...] = _pack_halves(y)


def _outln(merged, w_out_bf, x, g, b):
    T, D = x.shape
    tm = min(512, T)
    row = lambda: pl.BlockSpec((tm, D), lambda m: (m, 0))
    vec = lambda: pl.BlockSpec((1, D), lambda m: (0, 0))
    return pl.pallas_call(
        _outln_kernel,
        grid=(T // tm,),
        in_specs=[row(), pl.BlockSpec((D, D), lambda m: (0, 0)), row(), vec(), vec()],
        out_specs=[row(), row(), pl.BlockSpec((tm, D // 2), lambda m: (m, 0))],
        out_shape=[jax.ShapeDtypeStruct((T, D), F32), jax.ShapeDtypeStruct((T, D), BF16),
                   jax.ShapeDtypeStruct((T, D // 2), jnp.uint32)],
        compiler_params=_params(("arbitrary",), 48),
        name="outln",
    )(merged, w_out_bf, x, g.reshape(1, D), b.reshape(1, D))


def _router_kernel(x_ref, wr_ref, bias_ref, eidx_ref, rank_ref, gate_ref, cnt_ref,
                   run_ref, tri_ref, *, tm):
    G = N_GROUPS
    P = N_EXPERTS // N_GROUPS
    i = pl.program_id(0)

    @pl.when(i == 0)
    def _():
        run_ref[...] = jnp.zeros(run_ref.shape, F32)
        r = lax.broadcasted_iota(I32, (tm, tm), 0)
        c = lax.broadcasted_iota(I32, (tm, tm), 1)
        tri_ref[...] = jnp.where(r < c, 1.0, 0.0).astype(BF16)

    w3 = _split3(wr_ref[...])
    x3 = _split3(x_ref[...])
    logits = _dot_nt(w3[1], x3[1])
    for a, b in ((0, 2), (2, 0), (0, 1), (1, 0), (0, 0)):
        logits = logits + _dot_nt(w3[a], x3[b])
    scores = _sigmoid(logits)
    choice = scores + bias_ref[...]
    c3 = choice.reshape(G, P, tm)
    s3 = scores.reshape(G, P, tm)
    j_iota = lax.broadcasted_iota(I32, (G, P, tm), 1)
    g_iota = lax.broadcasted_iota(I32, (G, 1, tm), 0)
    e_iota = lax.broadcasted_iota(I32, (G, P, tm), 0) * P + j_iota
    ninf = -jnp.inf

    def amax3(v, idx, big):
        m = jnp.max(jnp.max(v, axis=1, keepdims=True), axis=0, keepdims=True)
        f = jnp.min(jnp.min(jnp.where(v == m, idx, big), axis=1, keepdims=True), axis=0, keepdims=True)
        return m, f

    m1 = jnp.max(c3, axis=1, keepdims=True)
    f1 = jnp.min(jnp.where(c3 == m1, j_iota, P), axis=1, keepdims=True)
    m2 = jnp.max(jnp.where(j_iota == f1, ninf, c3), axis=1, keepdims=True)
    gsc = m1 + m2
    gsel = jnp.zeros((G, 1, tm), jnp.bool_)
    for _ in range(TOPK_GROUPS):
        _, f = amax3(gsc, g_iota, G)
        hit = g_iota == f
        gsel = jnp.logical_or(gsel, hit)
        gsc = jnp.where(hit, ninf, gsc)
    cm = jnp.where(gsel, c3, ninf)

    picks = []
    wts = []
    msel = jnp.zeros((G, P, tm), F32)
    for _ in range(TOP_K):
        _, f = amax3(cm, e_iota, N_EXPERTS)
        hit = e_iota == f
        wts.append(jnp.sum(jnp.sum(jnp.where(hit, s3, 0.0), axis=1, keepdims=True), axis=0, keepdims=True))
        picks.append(f)
        msel = jnp.where(hit, 1.0, msel)
        cm = jnp.where(hit, ninf, cm)
    wsum = wts[0]
    for w in wts[1:]:
        wsum = wsum + w

    m2d = msel.reshape(N_EXPERTS, tm)
    before = run_ref[...] + _dot(m2d.astype(BF16), tri_ref[...])
    b3 = before.reshape(G, P, tm)
    for k in range(TOP_K):
        hit = e_iota == picks[k]
        rk = jnp.sum(jnp.sum(jnp.where(hit, b3, 0.0), axis=1, keepdims=True), axis=0, keepdims=True)
        eidx_ref[k:k + 1, :] = picks[k].reshape(1, tm)
        rank_ref[k:k + 1, :] = rk.reshape(1, tm).astype(I32)
        gate_ref[k:k + 1, :] = (wts[k] / wsum * ROUTED_SCALE).reshape(1, tm)
    run_ref[...] = run_ref[...] + jnp.sum(m2d, axis=1, keepdims=True)
    cnt_ref[...] = run_ref[...].astype(I32)


def _router(x1, w_router, router_bias):
    T, D = x1.shape
    tm = min(512, T)
    E = N_EXPERTS
    tok = lambda: pl.BlockSpec((TOP_K, tm), lambda i: (0, i))
    return pl.pallas_call(
        functools.partial(_router_kernel, tm=tm),
        grid=(T // tm,),
        in_specs=[pl.BlockSpec((tm, D), lambda i: (i, 0)),
                  pl.BlockSpec((E, D), lambda i: (0, 0)),
                  pl.BlockSpec((E, 1), lambda i: (0, 0))],
        out_specs=[tok(), tok(), tok(), pl.BlockSpec((E, 1), lambda i: (0, 0))],
        out_shape=[jax.ShapeDtypeStruct((TOP_K, T), I32), jax.ShapeDtypeStruct((TOP_K, T), I32),
                   jax.ShapeDtypeStruct((TOP_K, T), F32), jax.ShapeDtypeStruct((E, 1), I32)],
        scratch_shapes=[pltpu.VMEM((E, 1), F32), pltpu.VMEM((tm, tm), BF16)],
        compiler_params=_params(("arbitrary",), 32),
        name="router",
    )(x1, w_router.astype(F32).T, router_bias.astype(F32).reshape(E, 1))


def _dest_kernel(starts_ref, eidx_ref, rank_ref, o_ref):
    e = eidx_ref[...]
    acc = rank_ref[...]
    for j in range(N_EXPERTS):
        acc = acc + jnp.where(e == j, starts_ref[j], 0)
    o_ref[...] = acc


def _dest(starts, eidx, rank):
    full = lambda: pl.BlockSpec(eidx.shape, lambda i, s: (0, 0))
    return pl.pallas_call(
        _dest_kernel,
        grid_spec=pltpu.PrefetchScalarGridSpec(num_scalar_prefetch=1, grid=(1,),
                                               in_specs=[full(), full()], out_specs=full()),
        out_shape=jax.ShapeDtypeStruct(eidx.shape, I32),
        compiler_params=_params(("arbitrary",), 16),
        name="dest",
    )(starts, eidx, rank)


def _dispatch_kernel(dest_ref, x_ref, xs_hbm, sem, *, tm):
    def row_copy(t, k):
        return pltpu.make_async_copy(x_ref.at[pl.ds(t, 1)], xs_hbm.at[pl.ds(dest_ref[k, t], 1)], sem)

    def issue(t, carry):
        for k in range(TOP_K):
            row_copy(t, k).start()
        return carry

    def drain(t, carry):
        for k in range(TOP_K):
            row_copy(t, k).wait()
        return carry

    lax.fori_loop(0, tm, issue, 0)
    lax.fori_loop(0, tm, drain, 0)


def _dispatch(dest, x1):
    T, D = x1.shape
    tm = min(128, T)
    return pl.pallas_call(
        functools.partial(_dispatch_kernel, tm=tm),
        grid=(T // tm,),
        in_specs=[pl.BlockSpec((TOP_K, tm), lambda i: (0, i), memory_space=pltpu.SMEM),
                  pl.BlockSpec((tm, D), lambda i: (i, 0))],
        out_specs=pl.BlockSpec(memory_space=pl.ANY),
        out_shape=jax.ShapeDtypeStruct((T * TOP_K, D), x1.dtype),
        scratch_shapes=[pltpu.SemaphoreType.DMA(())],
        compiler_params=_params(("arbitrary",), 16),
        name="dispatch",
    )(dest, x1)


def _gmm_kernel(vblk, vexp, vfirst, vchg, vslot, vnext, gstart, nvis, xs_ref, wg_hbm, wu_hbm, wd_hbm, o_ref,
                wg_buf, wu_buf, wd_buf, wg_s, wu_s, wd_s, sem):
    i = pl.program_id(0)
    e = vexp[i]

    def weight_copies(expert, slot):
        return [pltpu.make_async_copy(w_hbm.at[0, expert], buf.at[slot], sem.at[slot, j])
                for j, (w_hbm, buf) in enumerate(((wg_hbm, wg_buf), (wu_hbm, wu_buf), (wd_hbm, wd_buf)))]

    @pl.when(i == 0)
    def _():
        for c in weight_copies(e, vslot[0]):
            c.start()

    @pl.when(vchg[i] == 1)
    def _():
        slot = vslot[i]
        for c in weight_copies(e, slot):
            c.wait()

        @pl.when(vnext[i] >= 0)
        def _():
            for c in weight_copies(vnext[i], 1 - slot):
                c.start()

        wg_s[...] = wg_buf[slot].astype(BF16)
        wu_s[...] = wu_buf[slot].astype(BF16)
        wd_s[...] = wd_buf[slot].astype(BF16)

    @pl.when(i < nvis[0])
    def _():
        x = _unpack_halves(xs_ref[...]).astype(BF16)
        g = _dot(x, wg_s[...])
        u = _dot(x, wu_s[...])
        y = _pack_halves(_dot((_silu(g) * u).astype(BF16), wd_s[...]))
        row = vblk[i] * GMM_ROWS + lax.broadcasted_iota(I32, (GMM_ROWS, 1), 0)
        mine = jnp.logical_and(row >= gstart[e], row < gstart[e + 1])

        @pl.when(vfirst[i] == 1)
        def _():
            o_ref[...] = jnp.where(mine, y, jnp.uint32(0))

        @pl.when(vfirst[i] == 0)
        def _():
            o_ref[...] = jnp.where(mine, y, o_ref[...])


def _gmm_plan(counts, n_rows):
    E = N_EXPERTS
    nblk = n_rows // GMM_ROWS
    nvis_max = nblk + E
    ends = jnp.cumsum(counts)
    gstart = jnp.concatenate([jnp.zeros((1,), I32), ends]).astype(I32)
    count_le = lambda table, x: jnp.sum((table[None, :] <= x[:, None]).astype(I32), axis=1)
    take = lambda table, idx: jnp.sum(jnp.where(idx[:, None] == jnp.arange(table.shape[0], dtype=I32)[None, :],
                                                table[None, :], 0), axis=1)
    blk_lo = jnp.arange(nblk, dtype=I32) * GMM_ROWS
    e_lo = count_le(ends, blk_lo)
    e_hi = count_le(ends, blk_lo + (GMM_ROWS - 1))
    per_blk = e_hi - e_lo + 1
    vend = jnp.cumsum(per_blk).astype(I32)
    vbeg = vend - per_blk
    nvis = vend[-1]
    vi = jnp.minimum(jnp.arange(nvis_max, dtype=I32), nvis - 1)
    vblk = count_le(vend, vi)
    vbeg_v = take(vbeg, vblk)
    vexp = jnp.minimum(take(e_lo, vblk) + (vi - vbeg_v), E - 1)
    vfirst = jnp.logical_and(vi == vbeg_v, jnp.arange(nvis_max) < nvis).astype(I32)
    vchg = jnp.concatenate([jnp.ones((1,), I32), (vexp[1:] != vexp[:-1]).astype(I32)])
    vslot = (jnp.cumsum(vchg) - 1) % 2
    pos = jnp.where(vchg == 1, jnp.arange(nvis_max, dtype=I32), nvis_max)
    nxt = jnp.concatenate([lax.cummin(pos[::-1])[::-1][1:], jnp.full((1,), nvis_max, I32)])
    vnext = jnp.where(nxt < nvis_max, take(vexp, jnp.minimum(nxt, nvis_max - 1)), -1)
    sched = (vblk, vexp, vfirst, vchg, vslot.astype(I32), vnext.astype(I32), gstart, nvis.reshape(1).astype(I32))
    return sched, nvis_max


def _gmm(xs, counts, w_gate, w_up, w_down):
    R, DP = xs.shape
    D = 2 * DP
    FF = EXPERT_FF
    sched, nvis_max = _gmm_plan(counts, R)
    rows = lambda: pl.BlockSpec((GMM_ROWS, DP), lambda i, vb, *_: (vb[i], 0))
    hbm = lambda: pl.BlockSpec(memory_space=pl.ANY)
    grid_spec = pltpu.PrefetchScalarGridSpec(
        num_scalar_prefetch=len(sched),
        grid=(nvis_max,),
        in_specs=[rows(), hbm(), hbm(), hbm()],
        out_specs=rows(),
        scratch_shapes=[pltpu.VMEM((2, D, FF), F32), pltpu.VMEM((2, D, FF), F32), pltpu.VMEM((2, FF, D), F32),
                        pltpu.VMEM((D, FF), BF16), pltpu.VMEM((D, FF), BF16), pltpu.VMEM((FF, D), BF16),
                        pltpu.SemaphoreType.DMA((2, 3))],
    )
    return pl.pallas_call(
        _gmm_kernel,
        grid_spec=grid_spec,
        out_shape=jax.ShapeDtypeStruct((R, DP), jnp.uint32),
        compiler_params=_params(("arbitrary",), 56),
        name="gmm",
    )(*sched, xs, w_gate, w_up, w_down)


def _combine_kernel(dest_ref, ys_hbm, gate_ref, x_ref, xb_ref, wsg_ref, wsu_ref, wsd_ref, g_ref, b_ref,
                    o_ref, buf_ref, sem, *, tm):
    def row_copy(t, k):
        return pltpu.make_async_copy(ys_hbm.at[pl.ds(dest_ref[k, t], 1)],
                                     buf_ref.at[k, pl.ds(t, 1)], sem)

    def issue(t, carry):
        for k in range(TOP_K):
            row_copy(t, k).start()
        return carry

    def drain(t, carry):
        for k in range(TOP_K):
            row_copy(t, k).wait()
        return carry

    lax.fori_loop(0, tm, issue, 0)
    xb = xb_ref[...]
    hidden = _silu(_dot(xb, wsg_ref[...])) * _dot(xb, wsu_ref[...])
    acc = DEEPNORM_ALPHA * x_ref[...] + _dot(hidden.astype(BF16), wsd_ref[...])
    lax.fori_loop(0, tm, drain, 0)
    gates = gate_ref[...]
    for k in range(TOP_K):
        acc = acc + gates[:, k:k + 1] * _unpack_halves(buf_ref[k])
    o_ref[...] = _layer_norm(acc, g_ref[...], b_ref[...])


def _combine(dest, ys, gates_t, x1, x1b, wsg, wsu, wsd, g, b):
    T, D = x1.shape
    tm = min(128, T)
    FF = wsg.shape[1]
    row = lambda: pl.BlockSpec((tm, D), lambda i: (i, 0))
    vec = lambda: pl.BlockSpec((1, D), lambda i: (0, 0))
    return pl.pallas_call(
        functools.partial(_combine_kernel, tm=tm),
        grid=(T // tm,),
        in_specs=[pl.BlockSpec((TOP_K, tm), lambda i: (0, i), memory_space=pltpu.SMEM),
                  pl.BlockSpec(memory_space=pl.ANY),
                  pl.BlockSpec((tm, TOP_K), lambda i: (i, 0)),
                  row(), row(),
                  pl.BlockSpec((D, FF), lambda i: (0, 0)),
                  pl.BlockSpec((D, FF), lambda i: (0, 0)),
                  pl.BlockSpec((FF, D), lambda i: (0, 0)),
                  vec(), vec()],
        out_specs=row(),
        out_shape=jax.ShapeDtypeStruct((T, D), F32),
        scratch_shapes=[pltpu.VMEM((TOP_K, tm, D // 2), jnp.uint32), pltpu.SemaphoreType.DMA(())],
        compiler_params=_params(("arbitrary",), 40),
        name="combine",
    )(dest, ys, gates_t, x1, x1b, wsg, wsu, wsd, g.reshape(1, D), b.reshape(1, D))


def kernel(x, w_in, conv_w, gdn_a_log, gdn_dt_bias, gdn_norm_w, diff_lambda, diff_subln_w, rel_bias_table,
           w_branch_a, w_branch_b, w_out, ln1_g, ln1_b, w_router, router_bias, w_gate, w_up, w_down,
           ws_gate, ws_up, ws_down, ln2_g, ln2_b):
    B, T, D = x.shape
    assert B == 1 and D == D_MODEL
    x2 = x.reshape(T, D)
    xb = x2.astype(BF16)
    w_in_t = jnp.swapaxes(w_in, 1, 2)
    qkv = _proj(xb, w_in_t, 0, 6, 512, BF16)
    g4 = _proj(xb, w_in_t, 6, 8, 512, F32)
    gab = _proj(xb, w_in_t, 56, 1, LANES, F32)

    tq = min(512, T)
    ya = _attn(qkv, _attn_bias_band(rel_bias_table.astype(F32), tq), diff_lambda[0].astype(F32),
               diff_subln_w[0].astype(F32), tq)
    yb = _gdn(g4, gab, conv_w[0].astype(F32), gdn_a_log[0], gdn_dt_bias[0], gdn_norm_w[0])
    merged = _merge(ya, yb, xb, w_branch_a[0], w_branch_b[0], w_in_t)
    x1, x1b, x1p = _outln(merged, w_out[0].astype(BF16), x2, ln1_g[0], ln1_b[0])

    eidx, rank, gates, counts = _router(x1, w_router[0], router_bias[0])
    counts = counts.reshape(N_EXPERTS)
    starts = jnp.cumsum(counts) - counts
    dest = _dest(starts.astype(I32), eidx, rank)
    xs = _dispatch(dest, x1p)
    ys = _gmm(xs, counts, w_gate, w_up, w_down)
    out = _combine(dest, ys, gates.T, x1, x1b, ws_gate[0].astype(BF16), ws_up[0].astype(BF16),
                   ws_down[0].astype(BF16), ln2_g[0], ln2_b[0])
    return out.reshape(B, T, D)
```

```python
import functools
import math

import jax
import jax.numpy as jnp
from jax import lax
from jax.experimental import pallas as pl
from jax.experimental.pallas import tpu as pltpu

F32 = jnp.float32
BF16 = jnp.bfloat16
I32 = jnp.int32

D_MODEL = 2048
CHUNK = 64
DIFF_HEADS = 8
DIFF_HEAD_DIM = 64
GDN_HEADS = 8
GDN_HEAD_DIM = 128
CONV_WIDTH = 4
REL_BUCKETS = 32
REL_MAX_DISTANCE = 128
N_EXPERTS = 64
TOP_K = 8
N_GROUPS = 8
TOPK_GROUPS = 4
EXPERT_FF = 512
ROUTED_SCALE = 2.5
DEPTH = 1
DEEPNORM_ALPHA = (2 * DEPTH) ** 0.25
LN_EPS = 1e-5
NORM_EPS = 1e-6
LAMBDA_INIT = 0.8 - 0.6 * math.exp(-0.3 * 0)

LANES = 128
MIB = 1024 * 1024
NEG_BIG = -1e30
LOG2E = math.log2(math.e)
GMM_ROWS = 256
ROW_TILE = (D_MODEL // 2 // LANES, LANES)


def _params(semantics, vmem_mib):
    return pltpu.CompilerParams(dimension_semantics=semantics, vmem_limit_bytes=vmem_mib * MIB)


def _sigmoid(x):
    return 1.0 / (1.0 + jnp.exp(-x))


def _silu(x):
    return x * _sigmoid(x)


def _dot(a, b):
    return jnp.dot(a, b, preferred_element_type=F32)


def _dot_nt(a, b):
    return lax.dot_general(a, b, (((1,), (1,)), ((), ())), preferred_element_type=F32)


def _pack_halves(y):
    n = y.shape[1] // 2
    bits = lambda v: lax.bitcast_convert_type(v.astype(BF16).astype(F32), jnp.uint32)
    return (bits(y[:, :n]) >> 16) | (bits(y[:, n:]) & jnp.uint32(0xFFFF0000))


def _unpack_halves(p):
    lo = lax.bitcast_convert_type(p << 16, F32)
    hi = lax.bitcast_convert_type(p & jnp.uint32(0xFFFF0000), F32)
    return jnp.concatenate([lo, hi], axis=1)


def _store_row_tiles(ref, packed):
    ref[...] = packed.reshape(ref.shape)


def _load_row_tiles(ref):
    return ref[...].reshape(ref.shape[0], -1)


def _split3(a):
    hi = a.astype(BF16)
    r1 = a - hi.astype(F32)
    mid = r1.astype(BF16)
    lo = (r1 - mid.astype(F32)).astype(BF16)
    return hi, mid, lo


def _proj_kernel(x_ref, w_ref, o_ref, wbf_ref, *, tn):
    @pl.when(pl.program_id(1) == 0)
    def _():
        wbf_ref[...] = w_ref[...].astype(BF16)

    acc = _dot_nt(x_ref[...], wbf_ref[...])
    for j in range(tn // LANES):
        o_ref[j] = acc[:, j * LANES:(j + 1) * LANES].astype(o_ref.dtype)


def _proj(xb, wt, blk0, nblk, tn, odt):
    T, D = xb.shape
    tm = min(1024, T)
    return pl.pallas_call(
        functools.partial(_proj_kernel, tn=tn),
        grid=(nblk, T // tm),
        in_specs=[pl.BlockSpec((tm, D), lambda n, m: (m, 0)),
                  pl.BlockSpec((None, tn, D), lambda n, m: (0, blk0 + n, 0))],
        out_specs=pl.BlockSpec((tn // LANES, tm, LANES), lambda n, m: (n, m, 0)),
        out_shape=jax.ShapeDtypeStruct((nblk * tn // LANES, T, LANES), odt),
        scratch_shapes=[pltpu.VMEM((tn, D), BF16)],
        compiler_params=_params(("arbitrary", "arbitrary"), 48),
        name="proj",
    )(xb, wt)


def _attn_kernel(q_ref, k_ref, v_ref, band_ref, lam_ref, subw_ref, o_ref,
                 m1, l1, a1, m2, l2, a2, s_buf, bias_s, *, tq):
    qi = pl.program_id(1)

    @pl.when(qi == 0)
    def _():
        toep = pltpu.roll(jnp.broadcast_to(band_ref[0], (tq, 3 * tq)), 0, 1, stride=1, stride_axis=0)
        qpos = lax.broadcasted_iota(I32, (tq, tq), 0)
        kpos = lax.broadcasted_iota(I32, (tq, tq), 1)
        visible = (kpos // CHUNK) <= (qpos // CHUNK)
        bias_s[0] = jnp.where(visible, toep[:, 2 * tq:], NEG_BIG)
        bias_s[1] = toep[:, tq:2 * tq]

    q = q_ref[0].astype(F32) * (DIFF_HEAD_DIM ** -0.5 * LOG2E)
    ones = jnp.ones((tq, LANES), BF16)
    lane = lax.broadcasted_iota(I32, q.shape, 1)
    q1 = jnp.where(lane < DIFF_HEAD_DIM, q, 0.0).astype(BF16)
    q2 = jnp.where(lane >= DIFF_HEAD_DIM, q, 0.0).astype(BF16)
    for m, l, a in ((m1, l1, a1), (m2, l2, a2)):
        m[...] = jnp.full(m.shape, NEG_BIG, F32)
        l[...] = jnp.zeros(l.shape, F32)
        a[...] = jnp.zeros(a.shape, F32)

    def scores(j, slot):
        k = k_ref[0, pl.ds(pl.multiple_of(j * tq, tq), tq), :]
        s_buf[slot, 0] = _dot_nt(q1, k)
        s_buf[slot, 1] = _dot_nt(q2, k)

    def accumulate(j, slot, bias_idx):
        v = jnp.concatenate([v_ref[0, pl.ds(pl.multiple_of(j * tq, tq), tq), :], ones], axis=1)
        for i, (m, l, a) in enumerate(((m1, l1, a1), (m2, l2, a2))):
            s = s_buf[slot, i]
            if bias_idx is not None:
                s = s + bias_s[bias_idx]
            m_old = m[...]
            m_new = jnp.maximum(m_old, jnp.max(s, axis=1, keepdims=True))
            alpha = jnp.exp2(m_old - m_new)
            p = jnp.exp2(s - jnp.concatenate([m_new] * (tq // LANES), axis=1))
            pv = _dot(p.astype(BF16), v)
            l[...] = alpha * l[...] + pv[:, LANES:]
            a[...] = alpha * a[...] + pv[:, :LANES]
            m[...] = m_new

    n_far = jnp.maximum(qi - 1, 0)
    odd = lax.rem(n_far, 2)

    @pl.when(odd == 1)
    def _():
        scores(0, 1)
        scores(1, 0)
        accumulate(0, 1, None)

    @pl.when(odd == 0)
    def _():
        scores(0, 0)

    @pl.loop(0, n_far // 2)
    def _(t):
        base = odd + 2 * t
        scores(base + 1, 1)
        accumulate(base, 0, None)
        scores(base + 2, 0)
        accumulate(base + 1, 1, None)

    @pl.when(qi >= 1)
    def _():
        scores(qi, 1)
        accumulate(qi - 1, 0, 1)
        accumulate(qi, 1, 0)

    @pl.when(qi == 0)
    def _():
        accumulate(0, 0, 0)

    lf = lam_ref[...]
    lam = (jnp.exp(jnp.sum(lf[0:1] * lf[1:2], axis=1, keepdims=True))
           - jnp.exp(jnp.sum(lf[2:3] * lf[3:4], axis=1, keepdims=True)) + LAMBDA_INIT)
    o = a1[...] / l1[...] - lam * (a2[...] / l2[...])
    y = o * lax.rsqrt(jnp.mean(o * o, axis=1, keepdims=True) + NORM_EPS) * subw_ref[...] * (1.0 - LAMBDA_INIT)
    o_ref[...] = y.astype(o_ref.dtype)


def _t5_bucket(rel):
    nb = REL_BUCKETS // 2
    base = jnp.where(rel > 0, nb, 0)
    n = jnp.abs(rel)
    max_exact = nb // 2
    nf = jnp.maximum(n, 1).astype(F32)
    large = max_exact + (jnp.log(nf / max_exact) / math.log(REL_MAX_DISTANCE / max_exact)
                         * (nb - max_exact)).astype(I32)
    large = jnp.minimum(large, nb - 1)
    return base + jnp.where(n < max_exact, n, large)


def _attn_bias_band(rel_table, tq):
    span = 3 * tq
    rel = jnp.arange(span) - 2 * tq
    onehot = (_t5_bucket(rel)[:, None] == jnp.arange(REL_BUCKETS)[None, :]).astype(F32)
    far = rel_table[REL_BUCKETS // 2 - 1]
    band = (jnp.dot(onehot, rel_table, precision=lax.Precision.HIGHEST) - far).T
    return (band * LOG2E)[:, None, :]


def _attn(qkv, bias_band, lam_vecs, subln_w, tq):
    nb, T, _ = qkv.shape
    H = DIFF_HEADS
    return pl.pallas_call(
        functools.partial(_attn_kernel, tq=tq),
        grid=(H, T // tq),
        in_specs=[pl.BlockSpec((1, tq, LANES), lambda h, i: (h, i, 0)),
                  pl.BlockSpec((1, T, LANES), lambda h, i: (H + h, 0, 0)),
                  pl.BlockSpec((1, T, LANES), lambda h, i: (2 * H + h, 0, 0)),
                  pl.BlockSpec((1, 1, 3 * tq), lambda h, i: (h, 0, 0)),
                  pl.BlockSpec((4, DIFF_HEAD_DIM), lambda h, i: (0, 0)),
                  pl.BlockSpec((1, LANES), lambda h, i: (0, 0))],
        out_specs=pl.BlockSpec((tq, LANES), lambda h, i: (i, h)),
        out_shape=jax.ShapeDtypeStruct((T, H * LANES), BF16),
        scratch_shapes=[pltpu.VMEM((tq, LANES), F32)] * 6 + [pltpu.VMEM((2, 2, tq, tq), F32), pltpu.VMEM((2, tq, tq), F32)],
        compiler_params=_params(("arbitrary", "arbitrary"), 40),
        name="attn",
    )(qkv, qkv, qkv, bias_band, lam_vecs, subln_w.reshape(1, LANES))


def _gdn_kernel(g4_ref, gab_ref, convw_ref, alog_ref, dtb_ref, normw_ref, o_ref,
                s_ref, xbuf_ref):
    C = CHUNK
    H = GDN_HEADS
    n = pl.program_id(0)

    @pl.when(n == 0)
    def _():
        s_ref[...] = jnp.zeros(s_ref.shape, F32)
        xbuf_ref[...] = jnp.zeros(xbuf_ref.shape, F32)

    gab = gab_ref[0]
    sp = gab + dtb_ref[...]
    softplus = jnp.maximum(sp, 0.0) + jnp.log(1.0 + jnp.exp(-jnp.abs(sp)))
    g = -jnp.exp(alog_ref[...]) * softplus
    beta_all = _sigmoid(gab)
    row = lax.broadcasted_iota(I32, (C, LANES), 0)
    gc = g
    for sh in (1, 2, 4, 8, 16, 32):
        gc = gc + jnp.where(row >= sh, pltpu.roll(gc, sh, 0), 0.0)
    gc_t = gc.T

    ri = lax.broadcasted_iota(I32, (C, C), 0)
    ci = lax.broadcasted_iota(I32, (C, C), 1)
    tri = ri >= ci
    strict = ri > ci
    eye = (ri == ci).astype(F32)

    for c in range(3 * H):
        xbuf_ref[c, 8:8 + C, :] = g4_ref[c]

    def conv(c):
        acc = xbuf_ref[c, 8 - (CONV_WIDTH - 1):8 - (CONV_WIDTH - 1) + C, :] * convw_ref[0, c:c + 1, :]
        for j in range(1, CONV_WIDTH):
            lo = 8 - (CONV_WIDTH - 1) + j
            acc = acc + xbuf_ref[c, lo:lo + C, :] * convw_ref[j, c:c + 1, :]
        return _silu(acc)

    def l2n(x):
        return x * lax.rsqrt(jnp.sum(x * x, axis=1, keepdims=True) + NORM_EPS)

    scale = GDN_HEAD_DIM ** -0.5
    heads = range(H)
    q = [l2n(conv(h)) * scale for h in heads]
    k = [l2n(conv(H + h)) for h in heads]
    v = [conv(2 * H + h) for h in heads]
    beta = [beta_all[:, 8 + h:9 + h] for h in heads]
    gcol = [gc[:, h:h + 1] for h in heads]
    glast = [gc[C - 1:C, h:h + 1] for h in heads]
    eg = [jnp.exp(gcol[h]) for h in heads]
    decay = [jnp.exp(jnp.where(tri, gcol[h] - gc_t[h:h + 1, :], -jnp.inf)) for h in heads]
    kb = [k[h] * beta[h] for h in heads]
    prod = [_dot_nt(jnp.concatenate([kb[h], q[h]], axis=0).astype(BF16), k[h].astype(BF16))
            for h in heads]
    lmat = [jnp.where(strict, prod[h][:C] * decay[h], 0.0) for h in heads]
    aqk = [jnp.where(tri, prod[h][C:] * decay[h], 0.0).astype(BF16) for h in heads]
    tm = [eye - lmat[h] for h in heads]
    pw = lmat
    for _ in range(5):
        pwb = [pw[h].astype(BF16) for h in heads]
        pw = [_dot(pwb[h], pwb[h]) for h in heads]
        tm = [tm[h] + _dot(tm[h].astype(BF16), pw[h].astype(BF16)) for h in heads]
    uw = [_dot(tm[h].astype(BF16),
               jnp.concatenate([v[h] * beta[h], kb[h] * eg[h]], axis=1).astype(BF16)) for h in heads]
    s_old = [s_ref[h] for h in heads]
    sd = [_dot(jnp.concatenate([uw[h][:, LANES:], q[h] * eg[h]], axis=0).astype(BF16), s_old[h].astype(BF16))
          for h in heads]
    vnb = [(uw[h][:, :LANES] - sd[h][:C]).astype(BF16) for h in heads]
    o = [sd[h][C:] + _dot(aqk[h], vnb[h]) for h in heads]
    for h in heads:
        k_dec = (k[h] * jnp.exp(glast[h] - gcol[h])).astype(BF16)
        s_ref[h] = s_old[h] * jnp.exp(glast[h]) + lax.dot_general(
            k_dec, vnb[h], (((0,), (0,)), ((), ())), preferred_element_type=F32)
    for h in heads:
        on = o[h] * lax.rsqrt(jnp.mean(o[h] * o[h], axis=1, keepdims=True) + NORM_EPS) * normw_ref[...]
        o_ref[:, h * LANES:(h + 1) * LANES] = (on * _silu(g4_ref[3 * H + h])).astype(o_ref.dtype)

    for c in range(3 * H):
        xbuf_ref[c, 0:8, :] = g4_ref[c, C - 8:C, :]


def _gdn(g4, gab, conv_w, a_log, dt_bias, norm_w):
    nb, T, _ = g4.shape
    H = GDN_HEADS
    C = CHUNK
    pad = lambda v, off: jnp.zeros((1, LANES), F32).at[0, off:off + H].set(v.astype(F32))
    return pl.pallas_call(
        _gdn_kernel,
        grid=(T // C,),
        in_specs=[pl.BlockSpec((4 * H, C, LANES), lambda n: (0, n, 0)),
                  pl.BlockSpec((1, C, LANES), lambda n: (0, n, 0)),
                  pl.BlockSpec((CONV_WIDTH, 3 * H, LANES), lambda n: (0, 0, 0)),
                  pl.BlockSpec((1, LANES), lambda n: (0, 0)),
                  pl.BlockSpec((1, LANES), lambda n: (0, 0)),
                  pl.BlockSpec((1, LANES), lambda n: (0, 0))],
        out_specs=pl.BlockSpec((C, H * LANES), lambda n: (n, 0)),
        out_shape=jax.ShapeDtypeStruct((T, H * LANES), BF16),
        scratch_shapes=[pltpu.VMEM((H, GDN_HEAD_DIM, GDN_HEAD_DIM), F32),
                        pltpu.VMEM((3 * H, 8 + C, LANES), F32)],
        compiler_params=_params(("arbitrary",), 32),
        name="gdn",
    )(g4, gab, conv_w.reshape(CONV_WIDTH, 3 * H, LANES), pad(a_log, 0), pad(dt_bias, 0),
      norm_w.reshape(1, LANES).astype(F32))


GATE_COL = 7 * 1024 + 2 * GDN_HEADS


def _merge_kernel(ya_ref, yb_ref, x_ref, wa_ref, wb_ref, wga_ref, wgb_ref, o_ref,
                  wa_s, wb_s, wga_s, wgb_s):
    @pl.when(pl.program_id(1) == 0)
    def _():
        wa_s[...] = wa_ref[...].astype(BF16)
        wb_s[...] = wb_ref[...].astype(BF16)
        wga_s[...] = wga_ref[0].astype(BF16)
        wgb_s[...] = wgb_ref[0].astype(BF16)

    x = x_ref[...]
    a = _dot(ya_ref[...], wa_s[...])
    b = _dot(yb_ref[...], wb_s[...])
    ga = _sigmoid(_dot_nt(x, wga_s[...]))
    gb = _sigmoid(_dot_nt(x, wgb_s[...]))
    o_ref[...] = (ga * a + gb * b).astype(o_ref.dtype)


def _merge(ya, yb, xb, w_a, w_b, w_in_t):
    T, D = xb.shape
    W = ya.shape[1]
    tm = min(512, T)
    tn = 512
    wspec = lambda rows: pl.BlockSpec((rows, tn), lambda n, m: (0, n))
    gspec = lambda off: pl.BlockSpec((pl.Element(1), pl.Element(tn), pl.Element(D)),
                                     lambda n, m: (0, pl.multiple_of(GATE_COL + off + n * tn, 16), 0))
    return pl.pallas_call(
        _merge_kernel,
        grid=(D // tn, T // tm),
        in_specs=[pl.BlockSpec((tm, W), lambda n, m: (m, 0)),
                  pl.BlockSpec((tm, W), lambda n, m: (m, 0)),
                  pl.BlockSpec((tm, D), lambda n, m: (m, 0)),
                  wspec(W), wspec(W), gspec(0), gspec(D)],
        out_specs=pl.BlockSpec((tm, tn), lambda n, m: (m, n)),
        out_shape=jax.ShapeDtypeStruct((T, D), BF16),
        scratch_shapes=[pltpu.VMEM((W, tn), BF16), pltpu.VMEM((W, tn), BF16),
                        pltpu.VMEM((tn, D), BF16), pltpu.VMEM((tn, D), BF16)],
        compiler_params=_params(("arbitrary", "arbitrary"), 48),
        name="merge",
    )(ya, yb, xb, w_a, w_b, w_in_t, w_in_t)


def _layer_norm(x, g, b):
    mu = jnp.mean(x, axis=1, keepdims=True)
    xc = x - mu
    var = jnp.mean(xc * xc, axis=1, keepdims=True)
    return xc * lax.rsqrt(var + LN_EPS) * g + b


def _outln_kernel(mg_ref, w_ref, x_ref, g_ref, b_ref, o_ref, ob_ref, op_ref):
    h = _dot(mg_ref[...], w_ref[...])
    y = _layer_norm(DEEPNORM_ALPHA * x_ref[...] + h, g_ref[...], b_ref[...])
    o_ref[...] = y
    ob_ref[...] = y.astype(BF16)
    _store_row_tiles(op_ref, _pack_halves(y))


def _outln(merged, w_out_bf, x, g, b):
    T, D = x.shape
    tm = min(512, T)
    row = lambda: pl.BlockSpec((tm, D), lambda m: (m, 0))
    vec = lambda: pl.BlockSpec((1, D), lambda m: (0, 0))
    return pl.pallas_call(
        _outln_kernel,
        grid=(T // tm,),
        in_specs=[row(), pl.BlockSpec((D, D), lambda m: (0, 0)), row(), vec(), vec()],
        out_specs=[row(), row(), pl.BlockSpec((tm,) + ROW_TILE, lambda m: (m, 0, 0))],
        out_shape=[jax.ShapeDtypeStruct((T, D), F32), jax.ShapeDtypeStruct((T, D), BF16),
                   jax.ShapeDtypeStruct((T,) + ROW_TILE, jnp.uint32)],
        compiler_params=_params(("arbitrary",), 48),
        name="outln",
    )(merged, w_out_bf, x, g.reshape(1, D), b.reshape(1, D))


def _router_kernel(x_ref, wr_ref, bias_ref, eidx_ref, rank_ref, gate_ref, cnt_ref,
                   run_ref, tri_ref, *, tm):
    G = N_GROUPS
    P = N_EXPERTS // N_GROUPS
    i = pl.program_id(0)

    @pl.when(i == 0)
    def _():
        run_ref[...] = jnp.zeros(run_ref.shape, F32)
        r = lax.broadcasted_iota(I32, (tm, tm), 0)
        c = lax.broadcasted_iota(I32, (tm, tm), 1)
        tri_ref[...] = jnp.where(r < c, 1.0, 0.0).astype(BF16)

    w3 = _split3(wr_ref[...])
    x3 = _split3(x_ref[...])
    logits = _dot_nt(w3[1], x3[1])
    for a, b in ((0, 2), (2, 0), (0, 1), (1, 0), (0, 0)):
        logits = logits + _dot_nt(w3[a], x3[b])
    scores = _sigmoid(logits)
    choice = scores + bias_ref[...]
    c3 = choice.reshape(G, P, tm)
    s3 = scores.reshape(G, P, tm)
    j_iota = lax.broadcasted_iota(I32, (G, P, tm), 1)
    g_iota = lax.broadcasted_iota(I32, (G, 1, tm), 0)
    e_iota = lax.broadcasted_iota(I32, (G, P, tm), 0) * P + j_iota
    ninf = -jnp.inf

    def amax3(v, idx, big):
        m = jnp.max(jnp.max(v, axis=1, keepdims=True), axis=0, keepdims=True)
        f = jnp.min(jnp.min(jnp.where(v == m, idx, big), axis=1, keepdims=True), axis=0, keepdims=True)
        return m, f

    m1 = jnp.max(c3, axis=1, keepdims=True)
    f1 = jnp.min(jnp.where(c3 == m1, j_iota, P), axis=1, keepdims=True)
    m2 = jnp.max(jnp.where(j_iota == f1, ninf, c3), axis=1, keepdims=True)
    gsc = m1 + m2
    gsel = jnp.zeros((G, 1, tm), jnp.bool_)
    for _ in range(TOPK_GROUPS):
        _, f = amax3(gsc, g_iota, G)
        hit = g_iota == f
        gsel = jnp.logical_or(gsel, hit)
        gsc = jnp.where(hit, ninf, gsc)
    cm = jnp.where(gsel, c3, ninf)

    picks = []
    wts = []
    msel = jnp.zeros((G, P, tm), F32)
    for _ in range(TOP_K):
        _, f = amax3(cm, e_iota, N_EXPERTS)
        hit = e_iota == f
        wts.append(jnp.sum(jnp.sum(jnp.where(hit, s3, 0.0), axis=1, keepdims=True), axis=0, keepdims=True))
        picks.append(f)
        msel = jnp.where(hit, 1.0, msel)
        cm = jnp.where(hit, ninf, cm)
    wsum = wts[0]
    for w in wts[1:]:
        wsum = wsum + w

    m2d = msel.reshape(N_EXPERTS, tm)
    before = run_ref[...] + _dot(m2d.astype(BF16), tri_ref[...])
    b3 = before.reshape(G, P, tm)
    for k in range(TOP_K):
        hit = e_iota == picks[k]
        rk = jnp.sum(jnp.sum(jnp.where(hit, b3, 0.0), axis=1, keepdims=True), axis=0, keepdims=True)
        eidx_ref[k:k + 1, :] = picks[k].reshape(1, tm)
        rank_ref[k:k + 1, :] = rk.reshape(1, tm).astype(I32)
        gate_ref[k:k + 1, :] = (wts[k] / wsum * ROUTED_SCALE).reshape(1, tm)
    run_ref[...] = run_ref[...] + jnp.sum(m2d, axis=1, keepdims=True)
    cnt_ref[...] = run_ref[...].astype(I32)


def _router(x1, w_router, router_bias):
    T, D = x1.shape
    tm = min(512, T)
    E = N_EXPERTS
    tok = lambda: pl.BlockSpec((TOP_K, tm), lambda i: (0, i))
    return pl.pallas_call(
        functools.partial(_router_kernel, tm=tm),
        grid=(T // tm,),
        in_specs=[pl.BlockSpec((tm, D), lambda i: (i, 0)),
                  pl.BlockSpec((E, D), lambda i: (0, 0)),
                  pl.BlockSpec((E, 1), lambda i: (0, 0))],
        out_specs=[tok(), tok(), tok(), pl.BlockSpec((E, 1), lambda i: (0, 0))],
        out_shape=[jax.ShapeDtypeStruct((TOP_K, T), I32), jax.ShapeDtypeStruct((TOP_K, T), I32),
                   jax.ShapeDtypeStruct((TOP_K, T), F32), jax.ShapeDtypeStruct((E, 1), I32)],
        scratch_shapes=[pltpu.VMEM((E, 1), F32), pltpu.VMEM((tm, tm), BF16)],
        compiler_params=_params(("arbitrary",), 32),
        name="router",
    )(x1, w_router.astype(F32).T, router_bias.astype(F32).reshape(E, 1))


def _dest_kernel(starts_ref, eidx_ref, rank_ref, o_ref):
    e = eidx_ref[...]
    acc = rank_ref[...]
    for j in range(N_EXPERTS):
        acc = acc + jnp.where(e == j, starts_ref[j], 0)
    o_ref[...] = acc


def _dest(starts, eidx, rank):
    full = lambda: pl.BlockSpec(eidx.shape, lambda i, s: (0, 0))
    return pl.pallas_call(
        _dest_kernel,
        grid_spec=pltpu.PrefetchScalarGridSpec(num_scalar_prefetch=1, grid=(1,),
                                               in_specs=[full(), full()], out_specs=full()),
        out_shape=jax.ShapeDtypeStruct(eidx.shape, I32),
        compiler_params=_params(("arbitrary",), 16),
        name="dest",
    )(starts, eidx, rank)


def _dispatch_kernel(dest_ref, x_ref, xs_hbm, sem, *, tm):
    def row_copy(t, k):
        return pltpu.make_async_copy(x_ref.at[pl.ds(t, 1)], xs_hbm.at[pl.ds(dest_ref[k, t], 1)], sem)

    def issue(t, carry):
        for k in range(TOP_K):
            row_copy(t, k).start()
        return carry

    def drain(t, carry):
        for k in range(TOP_K):
            row_copy(t, k).wait()
        return carry

    lax.fori_loop(0, tm, issue, 0)
    lax.fori_loop(0, tm, drain, 0)


def _dispatch(dest, x1p):
    T = x1p.shape[0]
    tm = min(128, T)
    return pl.pallas_call(
        functools.partial(_dispatch_kernel, tm=tm),
        grid=(T // tm,),
        in_specs=[pl.BlockSpec((TOP_K, tm), lambda i: (0, i), memory_space=pltpu.SMEM),
                  pl.BlockSpec((tm,) + ROW_TILE, lambda i: (i, 0, 0))],
        out_specs=pl.BlockSpec(memory_space=pl.ANY),
        out_shape=jax.ShapeDtypeStruct((T * TOP_K,) + ROW_TILE, x1p.dtype),
        scratch_shapes=[pltpu.SemaphoreType.DMA(())],
        compiler_params=_params(("arbitrary",), 16),
        name="dispatch",
    )(dest, x1p)


def _gmm_kernel(vblk, vexp, vfirst, vchg, vslot, vnext, gstart, nvis, xs_ref, wg_hbm, wu_hbm, wd_hbm, o_ref,
                wg_buf, wu_buf, wd_buf, wg_s, wu_s, wd_s, sem):
    i = pl.program_id(0)
    e = vexp[i]

    def weight_copies(expert, slot):
        return [pltpu.make_async_copy(w_hbm.at[0, expert], buf.at[slot], sem.at[slot, j])
                for j, (w_hbm, buf) in enumerate(((wg_hbm, wg_buf), (wu_hbm, wu_buf), (wd_hbm, wd_buf)))]

    @pl.when(i == 0)
    def _():
        for c in weight_copies(e, vslot[0]):
            c.start()

    @pl.when(vchg[i] == 1)
    def _():
        slot = vslot[i]
        for c in weight_copies(e, slot):
            c.wait()

        @pl.when(vnext[i] >= 0)
        def _():
            for c in weight_copies(vnext[i], 1 - slot):
                c.start()

        wg_s[...] = wg_buf[slot].astype(BF16)
        wu_s[...] = wu_buf[slot].astype(BF16)
        wd_s[...] = wd_buf[slot].astype(BF16)

    @pl.when(i < nvis[0])
    def _():
        x = _unpack_halves(_load_row_tiles(xs_ref)).astype(BF16)
        g = _dot(x, wg_s[...])
        u = _dot(x, wu_s[...])
        y = _pack_halves(_dot((_silu(g) * u).astype(BF16), wd_s[...]))
        row = vblk[i] * GMM_ROWS + lax.broadcasted_iota(I32, (GMM_ROWS, 1), 0)
        mine = jnp.logical_and(row >= gstart[e], row < gstart[e + 1])

        @pl.when(vfirst[i] == 1)
        def _():
            _store_row_tiles(o_ref, jnp.where(mine, y, jnp.uint32(0)))

        @pl.when(vfirst[i] == 0)
        def _():
            _store_row_tiles(o_ref, jnp.where(mine, y, _load_row_tiles(o_ref)))


def _gmm_plan(counts, n_rows):
    E = N_EXPERTS
    nblk = n_rows // GMM_ROWS
    nvis_max = nblk + E
    ends = jnp.cumsum(counts)
    gstart = jnp.concatenate([jnp.zeros((1,), I32), ends]).astype(I32)
    count_le = lambda table, x: jnp.sum((table[None, :] <= x[:, None]).astype(I32), axis=1)
    take = lambda table, idx: jnp.sum(jnp.where(idx[:, None] == jnp.arange(table.shape[0], dtype=I32)[None, :],
                                                table[None, :], 0), axis=1)
    blk_lo = jnp.arange(nblk, dtype=I32) * GMM_ROWS
    e_lo = count_le(ends, blk_lo)
    e_hi = count_le(ends, blk_lo + (GMM_ROWS - 1))
    per_blk = e_hi - e_lo + 1
    vend = jnp.cumsum(per_blk).astype(I32)
    vbeg = vend - per_blk
    nvis = vend[-1]
    vi = jnp.minimum(jnp.arange(nvis_max, dtype=I32), nvis - 1)
    vblk = count_le(vend, vi)
    vbeg_v = take(vbeg, vblk)
    vexp = jnp.minimum(take(e_lo, vblk) + (vi - vbeg_v), E - 1)
    vfirst = jnp.logical_and(vi == vbeg_v, jnp.arange(nvis_max) < nvis).astype(I32)
    vchg = jnp.concatenate([jnp.ones((1,), I32), (vexp[1:] != vexp[:-1]).astype(I32)])
    vslot = (jnp.cumsum(vchg) - 1) % 2
    pos = jnp.where(vchg == 1, jnp.arange(nvis_max, dtype=I32), nvis_max)
    nxt = jnp.concatenate([lax.cummin(pos[::-1])[::-1][1:], jnp.full((1,), nvis_max, I32)])
    vnext = jnp.where(nxt < nvis_max, take(vexp, jnp.minimum(nxt, nvis_max - 1)), -1)
    sched = (vblk, vexp, vfirst, vchg, vslot.astype(I32), vnext.astype(I32), gstart, nvis.reshape(1).astype(I32))
    return sched, nvis_max


def _gmm(xs, counts, w_gate, w_up, w_down):
    R = xs.shape[0]
    D = D_MODEL
    FF = EXPERT_FF
    sched, nvis_max = _gmm_plan(counts, R)
    rows = lambda: pl.BlockSpec((GMM_ROWS,) + ROW_TILE, lambda i, vb, *_: (vb[i], 0, 0))
    hbm = lambda: pl.BlockSpec(memory_space=pl.ANY)
    grid_spec = pltpu.PrefetchScalarGridSpec(
        num_scalar_prefetch=len(sched),
        grid=(nvis_max,),
        in_specs=[rows(), hbm(), hbm(), hbm()],
        out_specs=rows(),
        scratch_shapes=[pltpu.VMEM((2, D, FF), F32), pltpu.VMEM((2, D, FF), F32), pltpu.VMEM((2, FF, D), F32),
                        pltpu.VMEM((D, FF), BF16), pltpu.VMEM((D, FF), BF16), pltpu.VMEM((FF, D), BF16),
                        pltpu.SemaphoreType.DMA((2, 3))],
    )
    return pl.pallas_call(
        _gmm_kernel,
        grid_spec=grid_spec,
        out_shape=jax.ShapeDtypeStruct((R,) + ROW_TILE, jnp.uint32),
        compiler_params=_params(("arbitrary",), 56),
        name="gmm",
    )(*sched, xs, w_gate, w_up, w_down)


def _combine_kernel(dest_ref, ys_hbm, gate_ref, x_ref, xb_ref, wsg_ref, wsu_ref, wsd_ref, g_ref, b_ref,
                    o_ref, buf_ref, sem, *, tm):
    def row_copy(t, k):
        return pltpu.make_async_copy(ys_hbm.at[pl.ds(dest_ref[k, t], 1)],
                                     buf_ref.at[k, pl.ds(t, 1)], sem)

    def issue(t, carry):
        for k in range(TOP_K):
            row_copy(t, k).start()
        return carry

    def drain(t, carry):
        for k in range(TOP_K):
            row_copy(t, k).wait()
        return carry

    lax.fori_loop(0, tm, issue, 0)
    xb = xb_ref[...]
    hidden = _silu(_dot(xb, wsg_ref[...])) * _dot(xb, wsu_ref[...])
    acc = DEEPNORM_ALPHA * x_ref[...] + _dot(hidden.astype(BF16), wsd_ref[...])
    lax.fori_loop(0, tm, drain, 0)
    gates = gate_ref[...]
    for k in range(TOP_K):
        acc = acc + gates[:, k:k + 1] * _unpack_halves(_load_row_tiles(buf_ref.at[k]))
    o_ref[...] = _layer_norm(acc, g_ref[...], b_ref[...])


def _combine(dest, ys, gates_t, x1, x1b, wsg, wsu, wsd, g, b):
    T, D = x1.shape
    tm = min(128, T)
    FF = wsg.shape[1]
    row = lambda: pl.BlockSpec((tm, D), lambda i: (i, 0))
    vec = lambda: pl.BlockSpec((1, D), lambda i: (0, 0))
    return pl.pallas_call(
        functools.partial(_combine_kernel, tm=tm),
        grid=(T // tm,),
        in_specs=[pl.BlockSpec((TOP_K, tm), lambda i: (0, i), memory_space=pltpu.SMEM),
                  pl.BlockSpec(memory_space=pl.ANY),
                  pl.BlockSpec((tm, TOP_K), lambda i: (i, 0)),
                  row(), row(),
                  pl.BlockSpec((D, FF), lambda i: (0, 0)),
                  pl.BlockSpec((D, FF), lambda i: (0, 0)),
                  pl.BlockSpec((FF, D), lambda i: (0, 0)),
                  vec(), vec()],
        out_specs=row(),
        out_shape=jax.ShapeDtypeStruct((T, D), F32),
        scratch_shapes=[pltpu.VMEM((TOP_K, tm) + ROW_TILE, jnp.uint32), pltpu.SemaphoreType.DMA(())],
        compiler_params=_params(("arbitrary",), 40),
        name="combine",
    )(dest, ys, gates_t, x1, x1b, wsg, wsu, wsd, g.reshape(1, D), b.reshape(1, D))


def kernel(x, w_in, conv_w, gdn_a_log, gdn_dt_bias, gdn_norm_w, diff_lambda, diff_subln_w, rel_bias_table,
           w_branch_a, w_branch_b, w_out, ln1_g, ln1_b, w_router, router_bias, w_gate, w_up, w_down,
           ws_gate, ws_up, ws_down, ln2_g, ln2_b):
    B, T, D = x.shape
    assert B == 1 and D == D_MODEL
    x2 = x.reshape(T, D)
    xb = x2.astype(BF16)
    w_in_t = jnp.swapaxes(w_in, 1, 2)
    qkv = _proj(xb, w_in_t, 0, 3, 1024, BF16)
    g4 = _proj(xb, w_in_t, 3, 4, 1024, F32)
    gab = _proj(xb, w_in_t, 56, 1, LANES, F32)

    tq = min(512, T)
    ya = _attn(qkv, _attn_bias_band(rel_bias_table.astype(F32), tq), diff_lambda[0].astype(F32),
               diff_subln_w[0].astype(F32), tq)
    yb = _gdn(g4, gab, conv_w[0].astype(F32), gdn_a_log[0], gdn_dt_bias[0], gdn_norm_w[0])
    merged = _merge(ya, yb, xb, w_branch_a[0], w_branch_b[0], w_in_t)
    x1, x1b, x1p = _outln(merged, w_out[0].astype(BF16), x2, ln1_g[0], ln1_b[0])

    eidx, rank, gates, counts = _router(x1, w_router[0], router_bias[0])
    counts = counts.reshape(N_EXPERTS)
    starts = jnp.cumsum(counts) - counts
    dest = _dest(starts.astype(I32), eidx, rank)
    xs = _dispatch(dest, x1p)
    ys = _gmm(xs, counts, w_gate, w_up, w_down)
    out = _combine(dest, ys, gates.T, x1, x1b, ws_gate[0].astype(BF16), ws_up[0].astype(BF16),
                   ws_down[0].astype(BF16), ln2_g[0], ln2_b[0])
    return out.reshape(B, T, D)
```

```python
import functools
import math

import jax
import jax.numpy as jnp
from jax import lax
from jax.experimental import pallas as pl
from jax.experimental.pallas import tpu as pltpu

F32 = jnp.float32
BF16 = jnp.bfloat16
I32 = jnp.int32

D_MODEL = 2048
CHUNK = 64
DIFF_HEADS = 8
DIFF_HEAD_DIM = 64
GDN_HEADS = 8
GDN_HEAD_DIM = 128
CONV_WIDTH = 4
REL_BUCKETS = 32
REL_MAX_DISTANCE = 128
N_EXPERTS = 64
TOP_K = 8
N_GROUPS = 8
TOPK_GROUPS = 4
EXPERT_FF = 512
ROUTED_SCALE = 2.5
DEPTH = 1
DEEPNORM_ALPHA = (2 * DEPTH) ** 0.25
LN_EPS = 1e-5
NORM_EPS = 1e-6
LAMBDA_INIT = 0.8 - 0.6 * math.exp(-0.3 * 0)

LANES = 128
MIB = 1024 * 1024
NEG_BIG = -1e30
LOG2E = math.log2(math.e)
GMM_ROWS = 256
ROW_TILE = (D_MODEL // 2 // LANES, LANES)


def _params(semantics, vmem_mib):
    return pltpu.CompilerParams(dimension_semantics=semantics, vmem_limit_bytes=vmem_mib * MIB)


def _sigmoid(x):
    return 1.0 / (1.0 + jnp.exp(-x))


def _silu(x):
    return x * _sigmoid(x)


def _dot(a, b):
    return jnp.dot(a, b, preferred_element_type=F32)


def _dot_nt(a, b):
    return lax.dot_general(a, b, (((1,), (1,)), ((), ())), preferred_element_type=F32)


def _pack_halves(y):
    n = y.shape[1] // 2
    bits = lambda v: lax.bitcast_convert_type(v.astype(BF16).astype(F32), jnp.uint32)
    return (bits(y[:, :n]) >> 16) | (bits(y[:, n:]) & jnp.uint32(0xFFFF0000))


def _unpack_halves(p):
    lo = lax.bitcast_convert_type(p << 16, F32)
    hi = lax.bitcast_convert_type(p & jnp.uint32(0xFFFF0000), F32)
    return jnp.concatenate([lo, hi], axis=1)


def _store_row_tiles(ref, packed):
    ref[...] = packed.reshape(ref.shape)


def _load_row_tiles(ref):
    return ref[...].reshape(ref.shape[0], -1)


def _split3(a):
    hi = a.astype(BF16)
    r1 = a - hi.astype(F32)
    mid = r1.astype(BF16)
    lo = (r1 - mid.astype(F32)).astype(BF16)
    return hi, mid, lo


def _proj_kernel(x_ref, w_ref, o_ref, wbf_ref, *, tn):
    @pl.when(pl.program_id(1) == 0)
    def _():
        wbf_ref[...] = w_ref[...].astype(BF16)

    acc = _dot_nt(x_ref[...], wbf_ref[...])
    for j in range(tn // LANES):
        o_ref[j] = acc[:, j * LANES:(j + 1) * LANES].astype(o_ref.dtype)


def _proj(xb, wt, blk0, nblk, tn, odt):
    T, D = xb.shape
    tm = min(1024, T)
    return pl.pallas_call(
        functools.partial(_proj_kernel, tn=tn),
        grid=(nblk, T // tm),
        in_specs=[pl.BlockSpec((tm, D), lambda n, m: (m, 0)),
                  pl.BlockSpec((None, tn, D), lambda n, m: (0, blk0 + n, 0))],
        out_specs=pl.BlockSpec((tn // LANES, tm, LANES), lambda n, m: (n, m, 0)),
        out_shape=jax.ShapeDtypeStruct((nblk * tn // LANES, T, LANES), odt),
        scratch_shapes=[pltpu.VMEM((tn, D), BF16)],
        compiler_params=_params(("arbitrary", "arbitrary"), 48),
        name="proj",
    )(xb, wt)


def _attn_kernel(q_ref, k_ref, v_ref, band_ref, lam_ref, subw_ref, o_ref,
                 m1, l1, a1, m2, l2, a2, s_buf, bias_s, *, tq):
    qi = pl.program_id(1)

    @pl.when(qi == 0)
    def _():
        toep = pltpu.roll(jnp.broadcast_to(band_ref[0], (tq, 3 * tq)), 0, 1, stride=1, stride_axis=0)
        qpos = lax.broadcasted_iota(I32, (tq, tq), 0)
        kpos = lax.broadcasted_iota(I32, (tq, tq), 1)
        visible = (kpos // CHUNK) <= (qpos // CHUNK)
        bias_s[0] = jnp.where(visible, toep[:, 2 * tq:], NEG_BIG)
        bias_s[1] = toep[:, tq:2 * tq]

    q = q_ref[0].astype(F32) * (DIFF_HEAD_DIM ** -0.5 * LOG2E)
    ones = jnp.ones((tq, LANES), BF16)
    lane = lax.broadcasted_iota(I32, q.shape, 1)
    q1 = jnp.where(lane < DIFF_HEAD_DIM, q, 0.0).astype(BF16)
    q2 = jnp.where(lane >= DIFF_HEAD_DIM, q, 0.0).astype(BF16)
    for m, l, a in ((m1, l1, a1), (m2, l2, a2)):
        m[...] = jnp.full(m.shape, NEG_BIG, F32)
        l[...] = jnp.zeros(l.shape, F32)
        a[...] = jnp.zeros(a.shape, F32)

    def scores(j, slot):
        k = k_ref[0, pl.ds(pl.multiple_of(j * tq, tq), tq), :]
        s_buf[slot, 0] = _dot_nt(q1, k)
        s_buf[slot, 1] = _dot_nt(q2, k)

    def accumulate(j, slot, bias_idx):
        v = jnp.concatenate([v_ref[0, pl.ds(pl.multiple_of(j * tq, tq), tq), :], ones], axis=1)
        for i, (m, l, a) in enumerate(((m1, l1, a1), (m2, l2, a2))):
            s = s_buf[slot, i]
            if bias_idx is not None:
                s = s + bias_s[bias_idx]
            m_old = m[...]
            m_new = jnp.maximum(m_old, jnp.max(s, axis=1, keepdims=True))
            alpha = jnp.exp2(m_old - m_new)
            p = jnp.exp2(s - jnp.concatenate([m_new] * (tq // LANES), axis=1))
            pv = _dot(p.astype(BF16), v)
            l[...] = alpha * l[...] + pv[:, LANES:]
            a[...] = alpha * a[...] + pv[:, :LANES]
            m[...] = m_new

    n_far = jnp.maximum(qi - 1, 0)
    odd = lax.rem(n_far, 2)

    @pl.when(odd == 1)
    def _():
        scores(0, 1)
        scores(1, 0)
        accumulate(0, 1, None)

    @pl.when(odd == 0)
    def _():
        scores(0, 0)

    @pl.loop(0, n_far // 2)
    def _(t):
        base = odd + 2 * t
        scores(base + 1, 1)
        accumulate(base, 0, None)
        scores(base + 2, 0)
        accumulate(base + 1, 1, None)

    @pl.when(qi >= 1)
    def _():
        scores(qi, 1)
        accumulate(qi - 1, 0, 1)
        accumulate(qi, 1, 0)

    @pl.when(qi == 0)
    def _():
        accumulate(0, 0, 0)

    lf = lam_ref[...]
    lam = (jnp.exp(jnp.sum(lf[0:1] * lf[1:2], axis=1, keepdims=True))
           - jnp.exp(jnp.sum(lf[2:3] * lf[3:4], axis=1, keepdims=True)) + LAMBDA_INIT)
    o = a1[...] / l1[...] - lam * (a2[...] / l2[...])
    y = o * lax.rsqrt(jnp.mean(o * o, axis=1, keepdims=True) + NORM_EPS) * subw_ref[...] * (1.0 - LAMBDA_INIT)
    o_ref[...] = y.astype(o_ref.dtype)


def _t5_bucket(rel):
    nb = REL_BUCKETS // 2
    base = jnp.where(rel > 0, nb, 0)
    n = jnp.abs(rel)
    max_exact = nb // 2
    nf = jnp.maximum(n, 1).astype(F32)
    large = max_exact + (jnp.log(nf / max_exact) / math.log(REL_MAX_DISTANCE / max_exact)
                         * (nb - max_exact)).astype(I32)
    large = jnp.minimum(large, nb - 1)
    return base + jnp.where(n < max_exact, n, large)


def _attn_bias_band(rel_table, tq):
    span = 3 * tq
    rel = jnp.arange(span) - 2 * tq
    onehot = (_t5_bucket(rel)[:, None] == jnp.arange(REL_BUCKETS)[None, :]).astype(F32)
    far = rel_table[REL_BUCKETS // 2 - 1]
    band = (jnp.dot(onehot, rel_table, precision=lax.Precision.HIGHEST) - far).T
    return (band * LOG2E)[:, None, :]


def _attn(qkv, bias_band, lam_vecs, subln_w, tq):
    nb, T, _ = qkv.shape
    H = DIFF_HEADS
    return pl.pallas_call(
        functools.partial(_attn_kernel, tq=tq),
        grid=(H, T // tq),
        in_specs=[pl.BlockSpec((1, tq, LANES), lambda h, i: (h, i, 0)),
                  pl.BlockSpec((1, T, LANES), lambda h, i: (H + h, 0, 0)),
                  pl.BlockSpec((1, T, LANES), lambda h, i: (2 * H + h, 0, 0)),
                  pl.BlockSpec((1, 1, 3 * tq), lambda h, i: (h, 0, 0)),
                  pl.BlockSpec((4, DIFF_HEAD_DIM), lambda h, i: (0, 0)),
                  pl.BlockSpec((1, LANES), lambda h, i: (0, 0))],
        out_specs=pl.BlockSpec((tq, LANES), lambda h, i: (i, h)),
        out_shape=jax.ShapeDtypeStruct((T, H * LANES), BF16),
        scratch_shapes=[pltpu.VMEM((tq, LANES), F32)] * 6 + [pltpu.VMEM((2, 2, tq, tq), F32), pltpu.VMEM((2, tq, tq), F32)],
        compiler_params=_params(("arbitrary", "arbitrary"), 40),
        name="attn",
    )(qkv, qkv, qkv, bias_band, lam_vecs, subln_w.reshape(1, LANES))


def _gdn_kernel(g4_ref, gab_ref, convw_ref, alog_ref, dtb_ref, normw_ref, o_ref,
                s_ref, xbuf_ref):
    C = CHUNK
    H = GDN_HEADS
    n = pl.program_id(0)

    @pl.when(n == 0)
    def _():
        s_ref[...] = jnp.zeros(s_ref.shape, F32)
        xbuf_ref[...] = jnp.zeros(xbuf_ref.shape, F32)

    gab = gab_ref[0]
    sp = gab + dtb_ref[...]
    softplus = jnp.maximum(sp, 0.0) + jnp.log(1.0 + jnp.exp(-jnp.abs(sp)))
    g = -jnp.exp(alog_ref[...]) * softplus
    beta_all = _sigmoid(gab)
    row = lax.broadcasted_iota(I32, (C, LANES), 0)
    gc = g
    for sh in (1, 2, 4, 8, 16, 32):
        gc = gc + jnp.where(row >= sh, pltpu.roll(gc, sh, 0), 0.0)
    gc_t = gc.T

    ri = lax.broadcasted_iota(I32, (C, C), 0)
    ci = lax.broadcasted_iota(I32, (C, C), 1)
    tri = ri >= ci
    strict = ri > ci
    eye = (ri == ci).astype(F32)

    for c in range(3 * H):
        xbuf_ref[c, 8:8 + C, :] = g4_ref[c]

    def conv(c):
        acc = xbuf_ref[c, 8 - (CONV_WIDTH - 1):8 - (CONV_WIDTH - 1) + C, :] * convw_ref[0, c:c + 1, :]
        for j in range(1, CONV_WIDTH):
            lo = 8 - (CONV_WIDTH - 1) + j
            acc = acc + xbuf_ref[c, lo:lo + C, :] * convw_ref[j, c:c + 1, :]
        return _silu(acc)

    def l2n(x):
        return x * lax.rsqrt(jnp.sum(x * x, axis=1, keepdims=True) + NORM_EPS)

    scale = GDN_HEAD_DIM ** -0.5
    heads = range(H)
    q = [l2n(conv(h)) * scale for h in heads]
    k = [l2n(conv(H + h)) for h in heads]
    v = [conv(2 * H + h) for h in heads]
    beta = [beta_all[:, 8 + h:9 + h] for h in heads]
    gcol = [gc[:, h:h + 1] for h in heads]
    glast = [gc[C - 1:C, h:h + 1] for h in heads]
    eg = [jnp.exp(gcol[h]) for h in heads]
    decay = [jnp.exp(jnp.where(tri, gcol[h] - gc_t[h:h + 1, :], -jnp.inf)) for h in heads]
    kb = [k[h] * beta[h] for h in heads]
    prod = [_dot_nt(jnp.concatenate([kb[h], q[h]], axis=0).astype(BF16), k[h].astype(BF16))
            for h in heads]
    lmat = [jnp.where(strict, prod[h][:C] * decay[h], 0.0) for h in heads]
    aqk = [jnp.where(tri, prod[h][C:] * decay[h], 0.0).astype(BF16) for h in heads]
    tm = [eye - lmat[h] for h in heads]
    pw = lmat
    for _ in range(5):
        pwb = [pw[h].astype(BF16) for h in heads]
        pw = [_dot(pwb[h], pwb[h]) for h in heads]
        tm = [tm[h] + _dot(tm[h].astype(BF16), pw[h].astype(BF16)) for h in heads]
    uw = [_dot(tm[h].astype(BF16),
               jnp.concatenate([v[h] * beta[h], kb[h] * eg[h]], axis=1).astype(BF16)) for h in heads]
    s_old = [s_ref[h] for h in heads]
    sd = [_dot(jnp.concatenate([uw[h][:, LANES:], q[h] * eg[h]], axis=0).astype(BF16), s_old[h].astype(BF16))
          for h in heads]
    vnb = [(uw[h][:, :LANES] - sd[h][:C]).astype(BF16) for h in heads]
    o = [sd[h][C:] + _dot(aqk[h], vnb[h]) for h in heads]
    for h in heads:
        k_dec = (k[h] * jnp.exp(glast[h] - gcol[h])).astype(BF16)
        s_ref[h] = s_old[h] * jnp.exp(glast[h]) + lax.dot_general(
            k_dec, vnb[h], (((0,), (0,)), ((), ())), preferred_element_type=F32)
    for h in heads:
        on = o[h] * lax.rsqrt(jnp.mean(o[h] * o[h], axis=1, keepdims=True) + NORM_EPS) * normw_ref[...]
        o_ref[:, h * LANES:(h + 1) * LANES] = (on * _silu(g4_ref[3 * H + h])).astype(o_ref.dtype)

    for c in range(3 * H):
        xbuf_ref[c, 0:8, :] = g4_ref[c, C - 8:C, :]


def _gdn(g4, gab, conv_w, a_log, dt_bias, norm_w):
    nb, T, _ = g4.shape
    H = GDN_HEADS
    C = CHUNK
    pad = lambda v, off: jnp.zeros((1, LANES), F32).at[0, off:off + H].set(v.astype(F32))
    return pl.pallas_call(
        _gdn_kernel,
        grid=(T // C,),
        in_specs=[pl.BlockSpec((4 * H, C, LANES), lambda n: (0, n, 0)),
                  pl.BlockSpec((1, C, LANES), lambda n: (0, n, 0)),
                  pl.BlockSpec((CONV_WIDTH, 3 * H, LANES), lambda n: (0, 0, 0)),
                  pl.BlockSpec((1, LANES), lambda n: (0, 0)),
                  pl.BlockSpec((1, LANES), lambda n: (0, 0)),
                  pl.BlockSpec((1, LANES), lambda n: (0, 0))],
        out_specs=pl.BlockSpec((C, H * LANES), lambda n: (n, 0)),
        out_shape=jax.ShapeDtypeStruct((T, H * LANES), BF16),
        scratch_shapes=[pltpu.VMEM((H, GDN_HEAD_DIM, GDN_HEAD_DIM), F32),
                        pltpu.VMEM((3 * H, 8 + C, LANES), F32)],
        compiler_params=_params(("arbitrary",), 32),
        name="gdn",
    )(g4, gab, conv_w.reshape(CONV_WIDTH, 3 * H, LANES), pad(a_log, 0), pad(dt_bias, 0),
      norm_w.reshape(1, LANES).astype(F32))


GATE_COL = 7 * 1024 + 2 * GDN_HEADS


def _merge_kernel(ya_ref, yb_ref, x_ref, wa_ref, wb_ref, wga_ref, wgb_ref, o_ref,
                  wa_s, wb_s, wga_s, wgb_s):
    @pl.when(pl.program_id(1) == 0)
    def _():
        wa_s[...] = wa_ref[...].astype(BF16)
        wb_s[...] = wb_ref[...].astype(BF16)
        wga_s[...] = wga_ref[0].astype(BF16)
        wgb_s[...] = wgb_ref[0].astype(BF16)

    x = x_ref[...]
    a = _dot(ya_ref[...], wa_s[...])
    b = _dot(yb_ref[...], wb_s[...])
    ga = _sigmoid(_dot_nt(x, wga_s[...]))
    gb = _sigmoid(_dot_nt(x, wgb_s[...]))
    o_ref[...] = (ga * a + gb * b).astype(o_ref.dtype)


def _merge(ya, yb, xb, w_a, w_b, w_in_t):
    T, D = xb.shape
    W = ya.shape[1]
    tm = min(512, T)
    tn = 512
    wspec = lambda rows: pl.BlockSpec((rows, tn), lambda n, m: (0, n))
    gspec = lambda off: pl.BlockSpec((pl.Element(1), pl.Element(tn), pl.Element(D)),
                                     lambda n, m: (0, pl.multiple_of(GATE_COL + off + n * tn, 16), 0))
    return pl.pallas_call(
        _merge_kernel,
        grid=(D // tn, T // tm),
        in_specs=[pl.BlockSpec((tm, W), lambda n, m: (m, 0)),
                  pl.BlockSpec((tm, W), lambda n, m: (m, 0)),
                  pl.BlockSpec((tm, D), lambda n, m: (m, 0)),
                  wspec(W), wspec(W), gspec(0), gspec(D)],
        out_specs=pl.BlockSpec((tm, tn), lambda n, m: (m, n)),
        out_shape=jax.ShapeDtypeStruct((T, D), BF16),
        scratch_shapes=[pltpu.VMEM((W, tn), BF16), pltpu.VMEM((W, tn), BF16),
                        pltpu.VMEM((tn, D), BF16), pltpu.VMEM((tn, D), BF16)],
        compiler_params=_params(("arbitrary", "arbitrary"), 48),
        name="merge",
    )(ya, yb, xb, w_a, w_b, w_in_t, w_in_t)


def _layer_norm(x, g, b):
    mu = jnp.mean(x, axis=1, keepdims=True)
    xc = x - mu
    var = jnp.mean(xc * xc, axis=1, keepdims=True)
    return xc * lax.rsqrt(var + LN_EPS) * g + b


def _outln_kernel(mg_ref, w_ref, x_ref, g_ref, b_ref, o_ref, ob_ref, op_ref):
    h = _dot(mg_ref[...], w_ref[...])
    y = _layer_norm(DEEPNORM_ALPHA * x_ref[...] + h, g_ref[...], b_ref[...])
    o_ref[...] = y
    ob_ref[...] = y.astype(BF16)
    _store_row_tiles(op_ref, _pack_halves(y))


def _outln(merged, w_out_bf, x, g, b):
    T, D = x.shape
    tm = min(512, T)
    row = lambda: pl.BlockSpec((tm, D), lambda m: (m, 0))
    vec = lambda: pl.BlockSpec((1, D), lambda m: (0, 0))
    return pl.pallas_call(
        _outln_kernel,
        grid=(T // tm,),
        in_specs=[row(), pl.BlockSpec((D, D), lambda m: (0, 0)), row(), vec(), vec()],
        out_specs=[row(), row(), pl.BlockSpec((tm,) + ROW_TILE, lambda m: (m, 0, 0))],
        out_shape=[jax.ShapeDtypeStruct((T, D), F32), jax.ShapeDtypeStruct((T, D), BF16),
                   jax.ShapeDtypeStruct((T,) + ROW_TILE, jnp.uint32)],
        compiler_params=_params(("arbitrary",), 48),
        name="outln",
    )(merged, w_out_bf, x, g.reshape(1, D), b.reshape(1, D))


def _router_kernel(x_ref, wr_ref, bias_ref, eidx_ref, rank_ref, gate_ref, cnt_ref,
                   run_ref, tri_ref, *, tm):
    G = N_GROUPS
    P = N_EXPERTS // N_GROUPS
    i = pl.program_id(0)

    @pl.when(i == 0)
    def _():
        run_ref[...] = jnp.zeros(run_ref.shape, F32)
        r = lax.broadcasted_iota(I32, (tm, tm), 0)
        c = lax.broadcasted_iota(I32, (tm, tm), 1)
        tri_ref[...] = jnp.where(r < c, 1.0, 0.0).astype(BF16)

    w3 = _split3(wr_ref[...])
    x3 = _split3(x_ref[...])
    logits = _dot_nt(w3[1], x3[1])
    for a, b in ((0, 2), (2, 0), (0, 1), (1, 0), (0, 0)):
        logits = logits + _dot_nt(w3[a], x3[b])
    scores = _sigmoid(logits)
    choice = scores + bias_ref[...]
    c3 = choice.reshape(G, P, tm)
    s3 = scores.reshape(G, P, tm)
    j_iota = lax.broadcasted_iota(I32, (G, P, tm), 1)
    g_iota = lax.broadcasted_iota(I32, (G, 1, tm), 0)
    e_iota = lax.broadcasted_iota(I32, (G, P, tm), 0) * P + j_iota
    ninf = -jnp.inf

    def amax3(v, idx, big):
        m = jnp.max(jnp.max(v, axis=1, keepdims=True), axis=0, keepdims=True)
        f = jnp.min(jnp.min(jnp.where(v == m, idx, big), axis=1, keepdims=True), axis=0, keepdims=True)
        return m, f

    m1 = jnp.max(c3, axis=1, keepdims=True)
    f1 = jnp.min(jnp.where(c3 == m1, j_iota, P), axis=1, keepdims=True)
    m2 = jnp.max(jnp.where(j_iota == f1, ninf, c3), axis=1, keepdims=True)
    gsc = m1 + m2
    gsel = jnp.zeros((G, 1, tm), jnp.bool_)
    for _ in range(TOPK_GROUPS):
        _, f = amax3(gsc, g_iota, G)
        hit = g_iota == f
        gsel = jnp.logical_or(gsel, hit)
        gsc = jnp.where(hit, ninf, gsc)
    cm = jnp.where(gsel, c3, ninf)

    picks = []
    wts = []
    msel = jnp.zeros((G, P, tm), F32)
    for _ in range(TOP_K):
        _, f = amax3(cm, e_iota, N_EXPERTS)
        hit = e_iota == f
        wts.append(jnp.sum(jnp.sum(jnp.where(hit, s3, 0.0), axis=1, keepdims=True), axis=0, keepdims=True))
        picks.append(f)
        msel = jnp.where(hit, 1.0, msel)
        cm = jnp.where(hit, ninf, cm)
    wsum = wts[0]
    for w in wts[1:]:
        wsum = wsum + w

    m2d = msel.reshape(N_EXPERTS, tm)
    before = run_ref[...] + _dot(m2d.astype(BF16), tri_ref[...])
    b3 = before.reshape(G, P, tm)
    for k in range(TOP_K):
        hit = e_iota == picks[k]
        rk = jnp.sum(jnp.sum(jnp.where(hit, b3, 0.0), axis=1, keepdims=True), axis=0, keepdims=True)
        eidx_ref[k:k + 1, :] = picks[k].reshape(1, tm)
        rank_ref[k:k + 1, :] = rk.reshape(1, tm).astype(I32)
        gate_ref[k:k + 1, :] = (wts[k] / wsum * ROUTED_SCALE).reshape(1, tm)
    run_ref[...] = run_ref[...] + jnp.sum(m2d, axis=1, keepdims=True)
    cnt_ref[...] = run_ref[...].astype(I32)


def _router(x1, w_router, router_bias):
    T, D = x1.shape
    tm = min(512, T)
    E = N_EXPERTS
    tok = lambda: pl.BlockSpec((TOP_K, tm), lambda i: (0, i))
    return pl.pallas_call(
        functools.partial(_router_kernel, tm=tm),
        grid=(T // tm,),
        in_specs=[pl.BlockSpec((tm, D), lambda i: (i, 0)),
                  pl.BlockSpec((E, D), lambda i: (0, 0)),
                  pl.BlockSpec((E, 1), lambda i: (0, 0))],
        out_specs=[tok(), tok(), tok(), pl.BlockSpec((E, 1), lambda i: (0, 0))],
        out_shape=[jax.ShapeDtypeStruct((TOP_K, T), I32), jax.ShapeDtypeStruct((TOP_K, T), I32),
                   jax.ShapeDtypeStruct((TOP_K, T), F32), jax.ShapeDtypeStruct((E, 1), I32)],
        scratch_shapes=[pltpu.VMEM((E, 1), F32), pltpu.VMEM((tm, tm), BF16)],
        compiler_params=_params(("arbitrary",), 32),
        name="router",
    )(x1, w_router.astype(F32).T, router_bias.astype(F32).reshape(E, 1))


def _dest_kernel(starts_ref, eidx_ref, rank_ref, o_ref):
    e = eidx_ref[...]
    acc = rank_ref[...]
    for j in range(N_EXPERTS):
        acc = acc + jnp.where(e == j, starts_ref[j], 0)
    o_ref[...] = acc


def _dest(starts, eidx, rank):
    full = lambda: pl.BlockSpec(eidx.shape, lambda i, s: (0, 0))
    return pl.pallas_call(
        _dest_kernel,
        grid_spec=pltpu.PrefetchScalarGridSpec(num_scalar_prefetch=1, grid=(1,),
                                               in_specs=[full(), full()], out_specs=full()),
        out_shape=jax.ShapeDtypeStruct(eidx.shape, I32),
        compiler_params=_params(("arbitrary",), 16),
        name="dest",
    )(starts, eidx, rank)


def _dispatch_kernel(dest_ref, x_ref, xs_hbm, sem, *, tm):
    def row_copy(t, k):
        return pltpu.make_async_copy(x_ref.at[pl.ds(t, 1)], xs_hbm.at[pl.ds(dest_ref[k, t], 1)], sem)

    def issue(t, carry):
        for k in range(TOP_K):
            row_copy(t, k).start(priority=k % 2)
        return carry

    def drain(t, carry):
        for k in range(TOP_K):
            row_copy(t, k).wait()
        return carry

    lax.fori_loop(0, tm, issue, 0)
    lax.fori_loop(0, tm, drain, 0)


def _dispatch(dest, x1p):
    T = x1p.shape[0]
    tm = min(128, T)
    return pl.pallas_call(
        functools.partial(_dispatch_kernel, tm=tm),
        grid=(T // tm,),
        in_specs=[pl.BlockSpec((TOP_K, tm), lambda i: (0, i), memory_space=pltpu.SMEM),
                  pl.BlockSpec((tm,) + ROW_TILE, lambda i: (i, 0, 0))],
        out_specs=pl.BlockSpec(memory_space=pl.ANY),
        out_shape=jax.ShapeDtypeStruct((T * TOP_K,) + ROW_TILE, x1p.dtype),
        scratch_shapes=[pltpu.SemaphoreType.DMA(())],
        compiler_params=_params(("arbitrary",), 16),
        name="dispatch",
    )(dest, x1p)


def _gmm_kernel(vblk, vexp, vfirst, vchg, vslot, vnext, gstart, nvis, xs_ref, wg_hbm, wu_hbm, wd_hbm, o_ref,
                wg_buf, wu_buf, wd_buf, wg_s, wu_s, wd_s, sem):
    i = pl.program_id(0)
    e = vexp[i]

    def weight_copies(expert, slot):
        return [pltpu.make_async_copy(w_hbm.at[0, expert], buf.at[slot], sem.at[slot, j])
                for j, (w_hbm, buf) in enumerate(((wg_hbm, wg_buf), (wu_hbm, wu_buf), (wd_hbm, wd_buf)))]

    @pl.when(i == 0)
    def _():
        for c in weight_copies(e, vslot[0]):
            c.start(priority=1)

    @pl.when(vchg[i] == 1)
    def _():
        slot = vslot[i]
        for c in weight_copies(e, slot):
            c.wait()

        @pl.when(vnext[i] >= 0)
        def _():
            for c in weight_copies(vnext[i], 1 - slot):
                c.start(priority=1)

        wg_s[...] = wg_buf[slot].astype(BF16)
        wu_s[...] = wu_buf[slot].astype(BF16)
        wd_s[...] = wd_buf[slot].astype(BF16)

    @pl.when(i < nvis[0])
    def _():
        x = _unpack_halves(_load_row_tiles(xs_ref)).astype(BF16)
        g = _dot(x, wg_s[...])
        u = _dot(x, wu_s[...])
        y = _pack_halves(_dot((_silu(g) * u).astype(BF16), wd_s[...]))
        row = vblk[i] * GMM_ROWS + lax.broadcasted_iota(I32, (GMM_ROWS, 1), 0)
        mine = jnp.logical_and(row >= gstart[e], row < gstart[e + 1])

        @pl.when(vfirst[i] == 1)
        def _():
            _store_row_tiles(o_ref, jnp.where(mine, y, jnp.uint32(0)))

        @pl.when(vfirst[i] == 0)
        def _():
            _store_row_tiles(o_ref, jnp.where(mine, y, _load_row_tiles(o_ref)))


def _gmm_plan(counts, n_rows):
    E = N_EXPERTS
    nblk = n_rows // GMM_ROWS
    nvis_max = nblk + E
    ends = jnp.cumsum(counts)
    gstart = jnp.concatenate([jnp.zeros((1,), I32), ends]).astype(I32)
    count_le = lambda table, x: jnp.sum((table[None, :] <= x[:, None]).astype(I32), axis=1)
    take = lambda table, idx: jnp.sum(jnp.where(idx[:, None] == jnp.arange(table.shape[0], dtype=I32)[None, :],
                                                table[None, :], 0), axis=1)
    blk_lo = jnp.arange(nblk, dtype=I32) * GMM_ROWS
    e_lo = count_le(ends, blk_lo)
    e_hi = count_le(ends, blk_lo + (GMM_ROWS - 1))
    per_blk = e_hi - e_lo + 1
    vend = jnp.cumsum(per_blk).astype(I32)
    vbeg = vend - per_blk
    nvis = vend[-1]
    vi = jnp.minimum(jnp.arange(nvis_max, dtype=I32), nvis - 1)
    vblk = count_le(vend, vi)
    vbeg_v = take(vbeg, vblk)
    vexp = jnp.minimum(take(e_lo, vblk) + (vi - vbeg_v), E - 1)
    vfirst = jnp.logical_and(vi == vbeg_v, jnp.arange(nvis_max) < nvis).astype(I32)
    vchg = jnp.concatenate([jnp.ones((1,), I32), (vexp[1:] != vexp[:-1]).astype(I32)])
    vslot = (jnp.cumsum(vchg) - 1) % 2
    pos = jnp.where(vchg == 1, jnp.arange(nvis_max, dtype=I32), nvis_max)
    nxt = jnp.concatenate([lax.cummin(pos[::-1])[::-1][1:], jnp.full((1,), nvis_max, I32)])
    vnext = jnp.where(nxt < nvis_max, take(vexp, jnp.minimum(nxt, nvis_max - 1)), -1)
    sched = (vblk, vexp, vfirst, vchg, vslot.astype(I32), vnext.astype(I32), gstart, nvis.reshape(1).astype(I32))
    return sched, nvis_max


def _gmm(xs, counts, w_gate, w_up, w_down):
    R = xs.shape[0]
    D = D_MODEL
    FF = EXPERT_FF
    sched, nvis_max = _gmm_plan(counts, R)
    rows = lambda: pl.BlockSpec((GMM_ROWS,) + ROW_TILE, lambda i, vb, *_: (vb[i], 0, 0))
    hbm = lambda: pl.BlockSpec(memory_space=pl.ANY)
    grid_spec = pltpu.PrefetchScalarGridSpec(
        num_scalar_prefetch=len(sched),
        grid=(nvis_max,),
        in_specs=[rows(), hbm(), hbm(), hbm()],
        out_specs=rows(),
        scratch_shapes=[pltpu.VMEM((2, D, FF), F32), pltpu.VMEM((2, D, FF), F32), pltpu.VMEM((2, FF, D), F32),
                        pltpu.VMEM((D, FF), BF16), pltpu.VMEM((D, FF), BF16), pltpu.VMEM((FF, D), BF16),
                        pltpu.SemaphoreType.DMA((2, 3))],
    )
    return pl.pallas_call(
        _gmm_kernel,
        grid_spec=grid_spec,
        out_shape=jax.ShapeDtypeStruct((R,) + ROW_TILE, jnp.uint32),
        compiler_params=_params(("arbitrary",), 56),
        name="gmm",
    )(*sched, xs, w_gate, w_up, w_down)


def _combine_kernel(dest_ref, ys_hbm, gate_ref, x_ref, xb_ref, wsg_ref, wsu_ref, wsd_ref, g_ref, b_ref,
                    o_ref, buf_ref, sem, *, tm):
    def row_copy(t, k):
        return pltpu.make_async_copy(ys_hbm.at[pl.ds(dest_ref[k, t], 1)],
                                     buf_ref.at[k, pl.ds(t, 1)], sem)

    def issue(t, carry):
        for k in range(TOP_K):
            row_copy(t, k).start(priority=k % 2)
        return carry

    def drain(t, carry):
        for k in range(TOP_K):
            row_copy(t, k).wait()
        return carry

    lax.fori_loop(0, tm, issue, 0)
    xb = xb_ref[...]
    hidden = _silu(_dot(xb, wsg_ref[...])) * _dot(xb, wsu_ref[...])
    acc = DEEPNORM_ALPHA * x_ref[...] + _dot(hidden.astype(BF16), wsd_ref[...])
    lax.fori_loop(0, tm, drain, 0)
    gates = gate_ref[...]
    for k in range(TOP_K):
        acc = acc + gates[:, k:k + 1] * _unpack_halves(_load_row_tiles(buf_ref.at[k]))
    o_ref[...] = _layer_norm(acc, g_ref[...], b_ref[...])


def _combine(dest, ys, gates_t, x1, x1b, wsg, wsu, wsd, g, b):
    T, D = x1.shape
    tm = min(128, T)
    FF = wsg.shape[1]
    row = lambda: pl.BlockSpec((tm, D), lambda i: (i, 0))
    vec = lambda: pl.BlockSpec((1, D), lambda i: (0, 0))
    return pl.pallas_call(
        functools.partial(_combine_kernel, tm=tm),
        grid=(T // tm,),
        in_specs=[pl.BlockSpec((TOP_K, tm), lambda i: (0, i), memory_space=pltpu.SMEM),
                  pl.BlockSpec(memory_space=pl.ANY),
                  pl.BlockSpec((tm, TOP_K), lambda i: (i, 0)),
                  row(), row(),
                  pl.BlockSpec((D, FF), lambda i: (0, 0)),
                  pl.BlockSpec((D, FF), lambda i: (0, 0)),
                  pl.BlockSpec((FF, D), lambda i: (0, 0)),
                  vec(), vec()],
        out_specs=row(),
        out_shape=jax.ShapeDtypeStruct((T, D), F32),
        scratch_shapes=[pltpu.VMEM((TOP_K, tm) + ROW_TILE, jnp.uint32), pltpu.SemaphoreType.DMA(())],
        compiler_params=_params(("arbitrary",), 40),
        name="combine",
    )(dest, ys, gates_t, x1, x1b, wsg, wsu, wsd, g.reshape(1, D), b.reshape(1, D))


def kernel(x, w_in, conv_w, gdn_a_log, gdn_dt_bias, gdn_norm_w, diff_lambda, diff_subln_w, rel_bias_table,
           w_branch_a, w_branch_b, w_out, ln1_g, ln1_b, w_router, router_bias, w_gate, w_up, w_down,
           ws_gate, ws_up, ws_down, ln2_g, ln2_b):
    B, T, D = x.shape
    assert B == 1 and D == D_MODEL
    x2 = x.reshape(T, D)
    xb = x2.astype(BF16)
    w_in_t = jnp.swapaxes(w_in, 1, 2)
    qkv = _proj(xb, w_in_t, 0, 3, 1024, BF16)
    g4 = _proj(xb, w_in_t, 3, 4, 1024, F32)
    gab = _proj(xb, w_in_t, 56, 1, LANES, F32)

    tq = min(512, T)
    ya = _attn(qkv, _attn_bias_band(rel_bias_table.astype(F32), tq), diff_lambda[0].astype(F32),
               diff_subln_w[0].astype(F32), tq)
    yb = _gdn(g4, gab, conv_w[0].astype(F32), gdn_a_log[0], gdn_dt_bias[0], gdn_norm_w[0])
    merged = _merge(ya, yb, xb, w_branch_a[0], w_branch_b[0], w_in_t)
    x1, x1b, x1p = _outln(merged, w_out[0].astype(BF16), x2, ln1_g[0], ln1_b[0])

    eidx, rank, gates, counts = _router(x1, w_router[0], router_bias[0])
    counts = counts.reshape(N_EXPERTS)
    starts = jnp.cumsum(counts) - counts
    dest = _dest(starts.astype(I32), eidx, rank)
    xs = _dispatch(dest, x1p)
    ys = _gmm(xs, counts, w_gate, w_up, w_down)
    out = _combine(dest, ys, gates.T, x1, x1b, ws_gate[0].astype(BF16), ws_up[0].astype(BF16),
                   ws_down[0].astype(BF16), ln2_g[0], ln2_b[0])
    return out.reshape(B, T, D)
```

```python
import functools
import math

import jax
import jax.numpy as jnp
from jax import lax
from jax.experimental import pallas as pl
from jax.experimental.pallas import tpu as pltpu

F32 = jnp.float32
BF16 = jnp.bfloat16
I32 = jnp.int32

D_MODEL = 2048
CHUNK = 64
DIFF_HEADS = 8
DIFF_HEAD_DIM = 64
GDN_HEADS = 8
GDN_HEAD_DIM = 128
CONV_WIDTH = 4
REL_BUCKETS = 32
REL_MAX_DISTANCE = 128
N_EXPERTS = 64
TOP_K = 8
N_GROUPS = 8
TOPK_GROUPS = 4
EXPERT_FF = 512
ROUTED_SCALE = 2.5
DEPTH = 1
DEEPNORM_ALPHA = (2 * DEPTH) ** 0.25
LN_EPS = 1e-5
NORM_EPS = 1e-6
LAMBDA_INIT = 0.8 - 0.6 * math.exp(-0.3 * 0)

LANES = 128
MIB = 1024 * 1024
NEG_BIG = -1e30
LOG2E = math.log2(math.e)
GMM_ROWS = 256
ROW_TILE = (D_MODEL // 2 // LANES, LANES)


def _params(semantics, vmem_mib):
    return pltpu.CompilerParams(dimension_semantics=semantics, vmem_limit_bytes=vmem_mib * MIB)


def _sigmoid(x):
    return 1.0 / (1.0 + jnp.exp(-x))


def _silu(x):
    return x * _sigmoid(x)


def _dot(a, b):
    return jnp.dot(a, b, preferred_element_type=F32)


def _dot_nt(a, b):
    return lax.dot_general(a, b, (((1,), (1,)), ((), ())), preferred_element_type=F32)


def _pack_halves(y):
    n = y.shape[1] // 2
    bits = lambda v: lax.bitcast_convert_type(v.astype(BF16).astype(F32), jnp.uint32)
    return (bits(y[:, :n]) >> 16) | (bits(y[:, n:]) & jnp.uint32(0xFFFF0000))


def _unpack_halves(p):
    lo = lax.bitcast_convert_type(p << 16, F32)
    hi = lax.bitcast_convert_type(p & jnp.uint32(0xFFFF0000), F32)
    return jnp.concatenate([lo, hi], axis=1)


def _store_row_tiles(ref, packed):
    ref[...] = packed.reshape(ref.shape)


def _load_row_tiles(ref):
    return ref[...].reshape(ref.shape[0], -1)


def _split3(a):
    hi = a.astype(BF16)
    r1 = a - hi.astype(F32)
    mid = r1.astype(BF16)
    lo = (r1 - mid.astype(F32)).astype(BF16)
    return hi, mid, lo


def _proj_kernel(x_ref, w_ref, o_ref, wbf_ref, *, tn):
    @pl.when(pl.program_id(1) == 0)
    def _():
        wbf_ref[...] = w_ref[...].astype(BF16)

    acc = _dot_nt(x_ref[...], wbf_ref[...])
    for j in range(tn // LANES):
        o_ref[j] = acc[:, j * LANES:(j + 1) * LANES].astype(o_ref.dtype)


def _proj(xb, wt, blk0, nblk, tn, odt):
    T, D = xb.shape
    tm = min(1024, T)
    return pl.pallas_call(
        functools.partial(_proj_kernel, tn=tn),
        grid=(nblk, T // tm),
        in_specs=[pl.BlockSpec((tm, D), lambda n, m: (m, 0)),
                  pl.BlockSpec((None, tn, D), lambda n, m: (0, blk0 + n, 0))],
        out_specs=pl.BlockSpec((tn // LANES, tm, LANES), lambda n, m: (n, m, 0)),
        out_shape=jax.ShapeDtypeStruct((nblk * tn // LANES, T, LANES), odt),
        scratch_shapes=[pltpu.VMEM((tn, D), BF16)],
        compiler_params=_params(("arbitrary", "arbitrary"), 48),
        name="proj",
    )(xb, wt)


def _attn_kernel(q_ref, k_ref, v_ref, band_ref, lam_ref, subw_ref, o_ref,
                 m1, l1, a1, m2, l2, a2, s_buf, bias_s, *, tq):
    qi = pl.program_id(1)

    @pl.when(qi == 0)
    def _():
        toep = pltpu.roll(jnp.broadcast_to(band_ref[0], (tq, 3 * tq)), 0, 1, stride=1, stride_axis=0)
        qpos = lax.broadcasted_iota(I32, (tq, tq), 0)
        kpos = lax.broadcasted_iota(I32, (tq, tq), 1)
        visible = (kpos // CHUNK) <= (qpos // CHUNK)
        bias_s[0] = jnp.where(visible, toep[:, 2 * tq:], NEG_BIG)
        bias_s[1] = toep[:, tq:2 * tq]

    q = q_ref[0].astype(F32) * (DIFF_HEAD_DIM ** -0.5 * LOG2E)
    ones = jnp.ones((tq, LANES), BF16)
    lane = lax.broadcasted_iota(I32, q.shape, 1)
    q1 = jnp.where(lane < DIFF_HEAD_DIM, q, 0.0).astype(BF16)
    q2 = jnp.where(lane >= DIFF_HEAD_DIM, q, 0.0).astype(BF16)
    for m, l, a in ((m1, l1, a1), (m2, l2, a2)):
        m[...] = jnp.full(m.shape, NEG_BIG, F32)
        l[...] = jnp.zeros(l.shape, F32)
        a[...] = jnp.zeros(a.shape, F32)

    def scores(j, slot):
        k = k_ref[0, pl.ds(pl.multiple_of(j * tq, tq), tq), :]
        s_buf[slot, 0] = _dot_nt(q1, k)
        s_buf[slot, 1] = _dot_nt(q2, k)

    def accumulate(j, slot, bias_idx):
        v = jnp.concatenate([v_ref[0, pl.ds(pl.multiple_of(j * tq, tq), tq), :], ones], axis=1)
        for i, (m, l, a) in enumerate(((m1, l1, a1), (m2, l2, a2))):
            s = s_buf[slot, i]
            if bias_idx is not None:
                s = s + bias_s[bias_idx]
            m_old = m[...]
            m_new = jnp.maximum(m_old, jnp.max(s, axis=1, keepdims=True))
            alpha = jnp.exp2(m_old - m_new)
            p = jnp.exp2(s - jnp.concatenate([m_new] * (tq // LANES), axis=1))
            pv = _dot(p.astype(BF16), v)
            l[...] = alpha * l[...] + pv[:, LANES:]
            a[...] = alpha * a[...] + pv[:, :LANES]
            m[...] = m_new

    n_far = jnp.maximum(qi - 1, 0)
    odd = lax.rem(n_far, 2)

    @pl.when(odd == 1)
    def _():
        scores(0, 1)
        scores(1, 0)
        accumulate(0, 1, None)

    @pl.when(odd == 0)
    def _():
        scores(0, 0)

    @pl.loop(0, n_far // 2)
    def _(t):
        base = odd + 2 * t
        scores(base + 1, 1)
        accumulate(base, 0, None)
        scores(base + 2, 0)
        accumulate(base + 1, 1, None)

    @pl.when(qi >= 1)
    def _():
        scores(qi, 1)
        accumulate(qi - 1, 0, 1)
        accumulate(qi, 1, 0)

    @pl.when(qi == 0)
    def _():
        accumulate(0, 0, 0)

    lf = lam_ref[...]
    lam = (jnp.exp(jnp.sum(lf[0:1] * lf[1:2], axis=1, keepdims=True))
           - jnp.exp(jnp.sum(lf[2:3] * lf[3:4], axis=1, keepdims=True)) + LAMBDA_INIT)
    o = a1[...] / l1[...] - lam * (a2[...] / l2[...])
    y = o * lax.rsqrt(jnp.mean(o * o, axis=1, keepdims=True) + NORM_EPS) * subw_ref[...] * (1.0 - LAMBDA_INIT)
    o_ref[...] = y.astype(o_ref.dtype)


def _t5_bucket(rel):
    nb = REL_BUCKETS // 2
    base = jnp.where(rel > 0, nb, 0)
    n = jnp.abs(rel)
    max_exact = nb // 2
    nf = jnp.maximum(n, 1).astype(F32)
    large = max_exact + (jnp.log(nf / max_exact) / math.log(REL_MAX_DISTANCE / max_exact)
                         * (nb - max_exact)).astype(I32)
    large = jnp.minimum(large, nb - 1)
    return base + jnp.where(n < max_exact, n, large)


def _attn_bias_band(rel_table, tq):
    span = 3 * tq
    rel = jnp.arange(span) - 2 * tq
    onehot = (_t5_bucket(rel)[:, None] == jnp.arange(REL_BUCKETS)[None, :]).astype(F32)
    far = rel_table[REL_BUCKETS // 2 - 1]
    band = (jnp.dot(onehot, rel_table, precision=lax.Precision.HIGHEST) - far).T
    return (band * LOG2E)[:, None, :]


def _attn(qkv, bias_band, lam_vecs, subln_w, tq):
    nb, T, _ = qkv.shape
    H = DIFF_HEADS
    return pl.pallas_call(
        functools.partial(_attn_kernel, tq=tq),
        grid=(H, T // tq),
        in_specs=[pl.BlockSpec((1, tq, LANES), lambda h, i: (h, i, 0)),
                  pl.BlockSpec((1, T, LANES), lambda h, i: (H + h, 0, 0)),
                  pl.BlockSpec((1, T, LANES), lambda h, i: (2 * H + h, 0, 0)),
                  pl.BlockSpec((1, 1, 3 * tq), lambda h, i: (h, 0, 0)),
                  pl.BlockSpec((4, DIFF_HEAD_DIM), lambda h, i: (0, 0)),
                  pl.BlockSpec((1, LANES), lambda h, i: (0, 0))],
        out_specs=pl.BlockSpec((tq, LANES), lambda h, i: (i, h)),
        out_shape=jax.ShapeDtypeStruct((T, H * LANES), BF16),
        scratch_shapes=[pltpu.VMEM((tq, LANES), F32)] * 6 + [pltpu.VMEM((2, 2, tq, tq), F32), pltpu.VMEM((2, tq, tq), F32)],
        compiler_params=_params(("arbitrary", "arbitrary"), 40),
        name="attn",
    )(qkv, qkv, qkv, bias_band, lam_vecs, subln_w.reshape(1, LANES))


GDN_CHUNKS_PER_STEP = 4


def _gdn_kernel(g4_ref, gab_ref, convw_ref, alog_ref, dtb_ref, normw_ref, o_ref,
                s_ref, xbuf_ref):
    C = CHUNK
    H = GDN_HEADS
    R = GDN_CHUNKS_PER_STEP * C
    n = pl.program_id(0)

    @pl.when(n == 0)
    def _():
        s_ref[...] = jnp.zeros(s_ref.shape, F32)
        xbuf_ref[...] = jnp.zeros(xbuf_ref.shape, F32)

    gab = gab_ref[0]
    sp = gab + dtb_ref[...]
    softplus = jnp.maximum(sp, 0.0) + jnp.log(1.0 + jnp.exp(-jnp.abs(sp)))
    g = -jnp.exp(alog_ref[...]) * softplus
    beta_all = _sigmoid(gab)
    row_in_chunk = lax.broadcasted_iota(I32, (R, LANES), 0) % C
    gc = g
    for sh in (1, 2, 4, 8, 16, 32):
        gc = gc + jnp.where(row_in_chunk >= sh, pltpu.roll(gc, sh, 0), 0.0)
    gc_t = gc.T

    ri = lax.broadcasted_iota(I32, (C, C), 0)
    ci = lax.broadcasted_iota(I32, (C, C), 1)
    tri = ri >= ci
    strict = ri > ci
    eye = (ri == ci).astype(F32)

    for c in range(3 * H):
        xbuf_ref[c, 8:8 + R, :] = g4_ref[c]

    def conv(c):
        acc = xbuf_ref[c, 8 - (CONV_WIDTH - 1):8 - (CONV_WIDTH - 1) + R, :] * convw_ref[0, c:c + 1, :]
        for j in range(1, CONV_WIDTH):
            lo = 8 - (CONV_WIDTH - 1) + j
            acc = acc + xbuf_ref[c, lo:lo + R, :] * convw_ref[j, c:c + 1, :]
        return _silu(acc)

    def l2n(x):
        return x * lax.rsqrt(jnp.sum(x * x, axis=1, keepdims=True) + NORM_EPS)

    scale = GDN_HEAD_DIM ** -0.5
    heads = range(H)
    chunks = range(GDN_CHUNKS_PER_STEP)
    units = [(c, h) for c in chunks for h in heads]
    rows = lambda x, c: x[c * C:(c + 1) * C]
    q_all = [l2n(conv(h)) * scale for h in heads]
    k_all = [l2n(conv(H + h)) for h in heads]
    v_all = [conv(2 * H + h) for h in heads]
    q = {(c, h): rows(q_all[h], c) for c, h in units}
    k = {(c, h): rows(k_all[h], c) for c, h in units}
    v = {(c, h): rows(v_all[h], c) for c, h in units}
    beta = {(c, h): rows(beta_all, c)[:, 8 + h:9 + h] for c, h in units}
    gcol = {(c, h): rows(gc, c)[:, h:h + 1] for c, h in units}
    glast = {(c, h): gc[(c + 1) * C - 1:(c + 1) * C, h:h + 1] for c, h in units}
    eg = {u: jnp.exp(gcol[u]) for u in units}
    decay = {(c, h): jnp.exp(jnp.where(tri, gcol[c, h] - gc_t[h:h + 1, c * C:(c + 1) * C], -jnp.inf))
             for c, h in units}
    kb = {u: k[u] * beta[u] for u in units}
    prod = {u: _dot_nt(jnp.concatenate([kb[u], q[u]], axis=0).astype(BF16), k[u].astype(BF16))
            for u in units}
    lmat = {u: jnp.where(strict, prod[u][:C] * decay[u], 0.0) for u in units}
    aqk = {u: jnp.where(tri, prod[u][C:] * decay[u], 0.0).astype(BF16) for u in units}
    tm = {u: eye - lmat[u] for u in units}
    pw = lmat
    for _ in range(5):
        pwb = {u: pw[u].astype(BF16) for u in units}
        pw = {u: _dot(pwb[u], pwb[u]) for u in units}
        tm = {u: tm[u] + _dot(tm[u].astype(BF16), pw[u].astype(BF16)) for u in units}
    uw = {u: _dot(tm[u].astype(BF16),
                  jnp.concatenate([v[u] * beta[u], kb[u] * eg[u]], axis=1).astype(BF16)) for u in units}
    wq = {u: jnp.concatenate([uw[u][:, LANES:], q[u] * eg[u]], axis=0).astype(BF16) for u in units}
    k_dec = {u: (k[u] * jnp.exp(glast[u] - gcol[u])).astype(BF16) for u in units}
    s_cur = [s_ref[h] for h in heads]
    o = {}
    for c in chunks:
        sd = [_dot(wq[c, h], s_cur[h].astype(BF16)) for h in heads]
        vnb = [(uw[c, h][:, :LANES] - sd[h][:C]).astype(BF16) for h in heads]
        for h in heads:
            o[c, h] = sd[h][C:] + _dot(aqk[c, h], vnb[h])
        s_cur = [s_cur[h] * jnp.exp(glast[c, h]) + lax.dot_general(
            k_dec[c, h], vnb[h], (((0,), (0,)), ((), ())), preferred_element_type=F32) for h in heads]
    for h in heads:
        s_ref[h] = s_cur[h]
    for c, h in units:
        on = o[c, h] * lax.rsqrt(jnp.mean(o[c, h] * o[c, h], axis=1, keepdims=True) + NORM_EPS) * normw_ref[...]
        z = g4_ref[3 * H + h, c * C:(c + 1) * C, :]
        o_ref[c * C:(c + 1) * C, h * LANES:(h + 1) * LANES] = (on * _silu(z)).astype(o_ref.dtype)

    for c in range(3 * H):
        xbuf_ref[c, 0:8, :] = g4_ref[c, R - 8:R, :]


def _gdn(g4, gab, conv_w, a_log, dt_bias, norm_w):
    nb, T, _ = g4.shape
    H = GDN_HEADS
    C = GDN_CHUNKS_PER_STEP * CHUNK
    pad = lambda v, off: jnp.zeros((1, LANES), F32).at[0, off:off + H].set(v.astype(F32))
    return pl.pallas_call(
        _gdn_kernel,
        grid=(T // C,),
        in_specs=[pl.BlockSpec((4 * H, C, LANES), lambda n: (0, n, 0)),
                  pl.BlockSpec((1, C, LANES), lambda n: (0, n, 0)),
                  pl.BlockSpec((CONV_WIDTH, 3 * H, LANES), lambda n: (0, 0, 0)),
                  pl.BlockSpec((1, LANES), lambda n: (0, 0)),
                  pl.BlockSpec((1, LANES), lambda n: (0, 0)),
                  pl.BlockSpec((1, LANES), lambda n: (0, 0))],
        out_specs=pl.BlockSpec((C, H * LANES), lambda n: (n, 0)),
        out_shape=jax.ShapeDtypeStruct((T, H * LANES), BF16),
        scratch_shapes=[pltpu.VMEM((H, GDN_HEAD_DIM, GDN_HEAD_DIM), F32),
                        pltpu.VMEM((3 * H, 8 + C, LANES), F32)],
        compiler_params=_params(("arbitrary",), 32),
        name="gdn",
    )(g4, gab, conv_w.reshape(CONV_WIDTH, 3 * H, LANES), pad(a_log, 0), pad(dt_bias, 0),
      norm_w.reshape(1, LANES).astype(F32))


GATE_COL = 7 * 1024 + 2 * GDN_HEADS


def _merge_kernel(ya_ref, yb_ref, x_ref, wa_ref, wb_ref, wga_ref, wgb_ref, o_ref,
                  wa_s, wb_s, wga_s, wgb_s):
    @pl.when(pl.program_id(1) == 0)
    def _():
        wa_s[...] = wa_ref[...].astype(BF16)
        wb_s[...] = wb_ref[...].astype(BF16)
        wga_s[...] = wga_ref[0].astype(BF16)
        wgb_s[...] = wgb_ref[0].astype(BF16)

    x = x_ref[...]
    a = _dot(ya_ref[...], wa_s[...])
    b = _dot(yb_ref[...], wb_s[...])
    ga = _sigmoid(_dot_nt(x, wga_s[...]))
    gb = _sigmoid(_dot_nt(x, wgb_s[...]))
    o_ref[...] = (ga * a + gb * b).astype(o_ref.dtype)


def _merge(ya, yb, xb, w_a, w_b, w_in_t):
    T, D = xb.shape
    W = ya.shape[1]
    tm = min(512, T)
    tn = 512
    wspec = lambda rows: pl.BlockSpec((rows, tn), lambda n, m: (0, n))
    gspec = lambda off: pl.BlockSpec((pl.Element(1), pl.Element(tn), pl.Element(D)),
                                     lambda n, m: (0, pl.multiple_of(GATE_COL + off + n * tn, 16), 0))
    return pl.pallas_call(
        _merge_kernel,
        grid=(D // tn, T // tm),
        in_specs=[pl.BlockSpec((tm, W), lambda n, m: (m, 0)),
                  pl.BlockSpec((tm, W), lambda n, m: (m, 0)),
                  pl.BlockSpec((tm, D), lambda n, m: (m, 0)),
                  wspec(W), wspec(W), gspec(0), gspec(D)],
        out_specs=pl.BlockSpec((tm, tn), lambda n, m: (m, n)),
        out_shape=jax.ShapeDtypeStruct((T, D), BF16),
        scratch_shapes=[pltpu.VMEM((W, tn), BF16), pltpu.VMEM((W, tn), BF16),
                        pltpu.VMEM((tn, D), BF16), pltpu.VMEM((tn, D), BF16)],
        compiler_params=_params(("arbitrary", "arbitrary"), 48),
        name="merge",
    )(ya, yb, xb, w_a, w_b, w_in_t, w_in_t)


def _layer_norm(x, g, b):
    mu = jnp.mean(x, axis=1, keepdims=True)
    xc = x - mu
    var = jnp.mean(xc * xc, axis=1, keepdims=True)
    return xc * lax.rsqrt(var + LN_EPS) * g + b


def _outln_kernel(mg_ref, w_ref, x_ref, g_ref, b_ref, o_ref, ob_ref, op_ref):
    h = _dot(mg_ref[...], w_ref[...])
    y = _layer_norm(DEEPNORM_ALPHA * x_ref[...] + h, g_ref[...], b_ref[...])
    o_ref[...] = y
    ob_ref[...] = y.astype(BF16)
    _store_row_tiles(op_ref, _pack_halves(y))


def _outln(merged, w_out_bf, x, g, b):
    T, D = x.shape
    tm = min(512, T)
    row = lambda: pl.BlockSpec((tm, D), lambda m: (m, 0))
    vec = lambda: pl.BlockSpec((1, D), lambda m: (0, 0))
    return pl.pallas_call(
        _outln_kernel,
        grid=(T // tm,),
        in_specs=[row(), pl.BlockSpec((D, D), lambda m: (0, 0)), row(), vec(), vec()],
        out_specs=[row(), row(), pl.BlockSpec((tm,) + ROW_TILE, lambda m: (m, 0, 0))],
        out_shape=[jax.ShapeDtypeStruct((T, D), F32), jax.ShapeDtypeStruct((T, D), BF16),
                   jax.ShapeDtypeStruct((T,) + ROW_TILE, jnp.uint32)],
        compiler_params=_params(("arbitrary",), 48),
        name="outln",
    )(merged, w_out_bf, x, g.reshape(1, D), b.reshape(1, D))


def _router_kernel(x_ref, wr_ref, bias_ref, eidx_ref, rank_ref, gate_ref, cnt_ref,
                   run_ref, tri_ref, *, tm):
    G = N_GROUPS
    P = N_EXPERTS // N_GROUPS
    i = pl.program_id(0)

    @pl.when(i == 0)
    def _():
        run_ref[...] = jnp.zeros(run_ref.shape, F32)
        r = lax.broadcasted_iota(I32, (tm, tm), 0)
        c = lax.broadcasted_iota(I32, (tm, tm), 1)
        tri_ref[...] = jnp.where(r < c, 1.0, 0.0).astype(BF16)

    w3 = _split3(wr_ref[...])
    x3 = _split3(x_ref[...])
    logits = _dot_nt(w3[1], x3[1])
    for a, b in ((0, 2), (2, 0), (0, 1), (1, 0), (0, 0)):
        logits = logits + _dot_nt(w3[a], x3[b])
    scores = _sigmoid(logits)
    choice = scores + bias_ref[...]
    c3 = choice.reshape(G, P, tm)
    s3 = scores.reshape(G, P, tm)
    j_iota = lax.broadcasted_iota(I32, (G, P, tm), 1)
    g_iota = lax.broadcasted_iota(I32, (G, 1, tm), 0)
    e_iota = lax.broadcasted_iota(I32, (G, P, tm), 0) * P + j_iota
    ninf = -jnp.inf

    def amax3(v, idx, big):
        m = jnp.max(jnp.max(v, axis=1, keepdims=True), axis=0, keepdims=True)
        f = jnp.min(jnp.min(jnp.where(v == m, idx, big), axis=1, keepdims=True), axis=0, keepdims=True)
        return m, f

    m1 = jnp.max(c3, axis=1, keepdims=True)
    f1 = jnp.min(jnp.where(c3 == m1, j_iota, P), axis=1, keepdims=True)
    m2 = jnp.max(jnp.where(j_iota == f1, ninf, c3), axis=1, keepdims=True)
    gsc = m1 + m2
    gsel = jnp.zeros((G, 1, tm), jnp.bool_)
    for _ in range(TOPK_GROUPS):
        _, f = amax3(gsc, g_iota, G)
        hit = g_iota == f
        gsel = jnp.logical_or(gsel, hit)
        gsc = jnp.where(hit, ninf, gsc)
    cm = jnp.where(gsel, c3, ninf)

    picks = []
    wts = []
    msel = jnp.zeros((G, P, tm), F32)
    for _ in range(TOP_K):
        _, f = amax3(cm, e_iota, N_EXPERTS)
        hit = e_iota == f
        wts.append(jnp.sum(jnp.sum(jnp.where(hit, s3, 0.0), axis=1, keepdims=True), axis=0, keepdims=True))
        picks.append(f)
        msel = jnp.where(hit, 1.0, msel)
        cm = jnp.where(hit, ninf, cm)
    wsum = wts[0]
    for w in wts[1:]:
        wsum = wsum + w

    m2d = msel.reshape(N_EXPERTS, tm)
    before = run_ref[...] + _dot(m2d.astype(BF16), tri_ref[...])
    b3 = before.reshape(G, P, tm)
    for k in range(TOP_K):
        hit = e_iota == picks[k]
        rk = jnp.sum(jnp.sum(jnp.where(hit, b3, 0.0), axis=1, keepdims=True), axis=0, keepdims=True)
        eidx_ref[k:k + 1, :] = picks[k].reshape(1, tm)
        rank_ref[k:k + 1, :] = rk.reshape(1, tm).astype(I32)
        gate_ref[k:k + 1, :] = (wts[k] / wsum * ROUTED_SCALE).reshape(1, tm)
    run_ref[...] = run_ref[...] + jnp.sum(m2d, axis=1, keepdims=True)
    cnt_ref[...] = run_ref[...].astype(I32)


def _router(x1, w_router, router_bias):
    T, D = x1.shape
    tm = min(512, T)
    E = N_EXPERTS
    tok = lambda: pl.BlockSpec((TOP_K, tm), lambda i: (0, i))
    return pl.pallas_call(
        functools.partial(_router_kernel, tm=tm),
        grid=(T // tm,),
        in_specs=[pl.BlockSpec((tm, D), lambda i: (i, 0)),
                  pl.BlockSpec((E, D), lambda i: (0, 0)),
                  pl.BlockSpec((E, 1), lambda i: (0, 0))],
        out_specs=[tok(), tok(), tok(), pl.BlockSpec((E, 1), lambda i: (0, 0))],
        out_shape=[jax.ShapeDtypeStruct((TOP_K, T), I32), jax.ShapeDtypeStruct((TOP_K, T), I32),
                   jax.ShapeDtypeStruct((TOP_K, T), F32), jax.ShapeDtypeStruct((E, 1), I32)],
        scratch_shapes=[pltpu.VMEM((E, 1), F32), pltpu.VMEM((tm, tm), BF16)],
        compiler_params=_params(("arbitrary",), 32),
        name="router",
    )(x1, w_router.astype(F32).T, router_bias.astype(F32).reshape(E, 1))


def _dest_kernel(starts_ref, eidx_ref, rank_ref, o_ref):
    e = eidx_ref[...]
    acc = rank_ref[...]
    for j in range(N_EXPERTS):
        acc = acc + jnp.where(e == j, starts_ref[j], 0)
    o_ref[...] = acc


def _dest(starts, eidx, rank):
    full = lambda: pl.BlockSpec(eidx.shape, lambda i, s: (0, 0))
    return pl.pallas_call(
        _dest_kernel,
        grid_spec=pltpu.PrefetchScalarGridSpec(num_scalar_prefetch=1, grid=(1,),
                                               in_specs=[full(), full()], out_specs=full()),
        out_shape=jax.ShapeDtypeStruct(eidx.shape, I32),
        compiler_params=_params(("arbitrary",), 16),
        name="dest",
    )(starts, eidx, rank)


def _dispatch_kernel(dest_ref, lo_ref, hi_ref, nvis_ref, x_ref, xs_hbm, zero_ref, sem, *, tm, nblk):
    i = pl.program_id(0)

    def row_copy(t, k):
        return pltpu.make_async_copy(x_ref.at[pl.ds(t, 1)], xs_hbm.at[pl.ds(dest_ref[k, t], 1)], sem)

    def issue(t, carry):
        for k in range(TOP_K):
            row_copy(t, k).start(priority=k % 2)
        return carry

    def drain(t, carry):
        for k in range(TOP_K):
            row_copy(t, k).wait()
        return carry

    lax.fori_loop(0, tm, issue, 0)
    lax.fori_loop(0, tm, drain, 0)

    @pl.when(i == pl.num_programs(0) - 1)
    def _():
        zero_ref[...] = jnp.zeros(zero_ref.shape, zero_ref.dtype)

        def for_each_fill(fn):
            def per_expert(e, carry):
                n = hi_ref[e] - lo_ref[e]
                off = lo_ref[e]
                size = GMM_ROWS // 2
                while size >= 1:
                    @pl.when((n & size) != 0)
                    def _(off=off, size=size):
                        fn(pltpu.make_async_copy(zero_ref.at[pl.ds(0, size)], xs_hbm.at[pl.ds(off, size)], sem))
                    off = off + (n & size)
                    size //= 2
                return carry

            def per_block(b, carry):
                start = pl.multiple_of(b * GMM_ROWS, GMM_ROWS)
                fn(pltpu.make_async_copy(zero_ref, xs_hbm.at[pl.ds(start, GMM_ROWS)], sem))
                return carry

            lax.fori_loop(0, N_EXPERTS, per_expert, 0)
            lax.fori_loop(nvis_ref[0], nblk, per_block, 0)

        for_each_fill(lambda c: c.start())
        for_each_fill(lambda c: c.wait())


def _dispatch(dest, x1p, pad_lo, pad_hi, nvis, nblk):
    T = x1p.shape[0]
    tm = min(256, T)
    smem = lambda: pl.BlockSpec(memory_space=pltpu.SMEM)
    return pl.pallas_call(
        functools.partial(_dispatch_kernel, tm=tm, nblk=nblk),
        grid=(T // tm,),
        in_specs=[pl.BlockSpec((TOP_K, tm), lambda i: (0, i), memory_space=pltpu.SMEM),
                  smem(), smem(), smem(),
                  pl.BlockSpec((tm,) + ROW_TILE, lambda i: (i, 0, 0))],
        out_specs=pl.BlockSpec(memory_space=pl.ANY),
        out_shape=jax.ShapeDtypeStruct((nblk * GMM_ROWS,) + ROW_TILE, x1p.dtype),
        scratch_shapes=[pltpu.VMEM((GMM_ROWS,) + ROW_TILE, x1p.dtype), pltpu.SemaphoreType.DMA(())],
        compiler_params=_params(("arbitrary",), 16),
        name="dispatch",
    )(dest, pad_lo, pad_hi, nvis, x1p)


def _gmm_kernel(vblk, vexp, vchg, vslot, vnext, nvis, xs_ref, wg_hbm, wu_hbm, wd_hbm, o_ref,
                wg_buf, wu_buf, wd_buf, wg_s, wu_s, wd_s, sem):
    i = pl.program_id(0)
    e = vexp[i]

    def weight_copies(expert, slot):
        return [pltpu.make_async_copy(w_hbm.at[0, expert], buf.at[slot], sem.at[slot, j])
                for j, (w_hbm, buf) in enumerate(((wg_hbm, wg_buf), (wu_hbm, wu_buf), (wd_hbm, wd_buf)))]

    @pl.when(i == 0)
    def _():
        for c in weight_copies(e, vslot[0]):
            c.start(priority=1)

    @pl.when(vchg[i] == 1)
    def _():
        slot = vslot[i]
        for c in weight_copies(e, slot):
            c.wait()

        @pl.when(vnext[i] >= 0)
        def _():
            for c in weight_copies(vnext[i], 1 - slot):
                c.start(priority=1)

        wg_s[...] = wg_buf[slot].astype(BF16)
        wu_s[...] = wu_buf[slot].astype(BF16)
        wd_s[...] = wd_buf[slot].astype(BF16)

    @pl.when(i < nvis[0])
    def _():
        x = _unpack_halves(_load_row_tiles(xs_ref)).astype(BF16)
        g = _dot(x, wg_s[...])
        u = _dot(x, wu_s[...])
        _store_row_tiles(o_ref, _pack_halves(_dot((_silu(g) * u).astype(BF16), wd_s[...])))

    @pl.when(i >= nvis[0])
    def _():
        o_ref[...] = jnp.zeros(o_ref.shape, o_ref.dtype)


def _moe_blocks(n_assign):
    return n_assign // GMM_ROWS + N_EXPERTS


def _moe_layout(counts, nblk):
    E = N_EXPERTS
    padded = (counts + (GMM_ROWS - 1)) // GMM_ROWS * GMM_ROWS
    pend = jnp.cumsum(padded).astype(I32)
    pstart = pend - padded
    nvis = pend[-1] // GMM_ROWS
    count_le = lambda table, x: jnp.sum((table[None, :] <= x[:, None]).astype(I32), axis=1)
    take = lambda table, idx: jnp.sum(jnp.where(idx[:, None] == jnp.arange(table.shape[0], dtype=I32)[None, :],
                                                table[None, :], 0), axis=1)
    vblk = jnp.minimum(jnp.arange(nblk, dtype=I32), nvis - 1)
    vexp = jnp.minimum(count_le(pend, vblk * GMM_ROWS), E - 1)
    vchg = jnp.concatenate([jnp.ones((1,), I32), (vexp[1:] != vexp[:-1]).astype(I32)])
    vslot = (jnp.cumsum(vchg) - 1) % 2
    pos = jnp.where(vchg == 1, jnp.arange(nblk, dtype=I32), nblk)
    nxt = jnp.concatenate([lax.cummin(pos[::-1])[::-1][1:], jnp.full((1,), nblk, I32)])
    vnext = jnp.where(nxt < nblk, take(vexp, jnp.minimum(nxt, nblk - 1)), -1)
    sched = (vblk, vexp, vchg, vslot.astype(I32), vnext.astype(I32), nvis.reshape(1).astype(I32))
    return pstart, pstart + counts, pend, sched


def _gmm(xs, sched, w_gate, w_up, w_down):
    R = xs.shape[0]
    D = D_MODEL
    FF = EXPERT_FF
    rows = lambda: pl.BlockSpec((GMM_ROWS,) + ROW_TILE, lambda i, vb, *_: (vb[i], 0, 0))
    hbm = lambda: pl.BlockSpec(memory_space=pl.ANY)
    grid_spec = pltpu.PrefetchScalarGridSpec(
        num_scalar_prefetch=len(sched),
        grid=(R // GMM_ROWS,),
        in_specs=[rows(), hbm(), hbm(), hbm()],
        out_specs=pl.BlockSpec((GMM_ROWS,) + ROW_TILE, lambda i, *_: (i, 0, 0)),
        scratch_shapes=[pltpu.VMEM((2, D, FF), F32), pltpu.VMEM((2, D, FF), F32), pltpu.VMEM((2, FF, D), F32),
                        pltpu.VMEM((D, FF), BF16), pltpu.VMEM((D, FF), BF16), pltpu.VMEM((FF, D), BF16),
                        pltpu.SemaphoreType.DMA((2, 3))],
    )
    return pl.pallas_call(
        _gmm_kernel,
        grid_spec=grid_spec,
        out_shape=jax.ShapeDtypeStruct((R,) + ROW_TILE, jnp.uint32),
        compiler_params=_params(("arbitrary",), 56),
        name="gmm",
    )(*sched, xs, w_gate, w_up, w_down)


def _combine_kernel(dest_ref, ys_hbm, gate_ref, x_ref, xb_ref, wsg_ref, wsu_ref, wsd_ref, g_ref, b_ref,
                    o_ref, buf_ref, sem, *, tm):
    def row_copy(t, k):
        return pltpu.make_async_copy(ys_hbm.at[pl.ds(dest_ref[k, t], 1)],
                                     buf_ref.at[k, pl.ds(t, 1)], sem)

    def issue(t, carry):
        for k in range(TOP_K):
            row_copy(t, k).start(priority=k % 2)
        return carry

    def drain(t, carry):
        for k in range(TOP_K):
            row_copy(t, k).wait()
        return carry

    lax.fori_loop(0, tm, issue, 0)
    xb = xb_ref[...]
    hidden = _silu(_dot(xb, wsg_ref[...])) * _dot(xb, wsu_ref[...])
    acc = DEEPNORM_ALPHA * x_ref[...] + _dot(hidden.astype(BF16), wsd_ref[...])
    lax.fori_loop(0, tm, drain, 0)
    gates = gate_ref[...]
    for k in range(TOP_K):
        acc = acc + gates[:, k:k + 1] * _unpack_halves(_load_row_tiles(buf_ref.at[k]))
    o_ref[...] = _layer_norm(acc, g_ref[...], b_ref[...])


def _combine(dest, ys, gates_t, x1, x1b, wsg, wsu, wsd, g, b):
    T, D = x1.shape
    tm = min(256, T)
    FF = wsg.shape[1]
    row = lambda: pl.BlockSpec((tm, D), lambda i: (i, 0))
    vec = lambda: pl.BlockSpec((1, D), lambda i: (0, 0))
    return pl.pallas_call(
        functools.partial(_combine_kernel, tm=tm),
        grid=(T // tm,),
        in_specs=[pl.BlockSpec((TOP_K, tm), lambda i: (0, i), memory_space=pltpu.SMEM),
                  pl.BlockSpec(memory_space=pl.ANY),
                  pl.BlockSpec((tm, TOP_K), lambda i: (i, 0)),
                  row(), row(),
                  pl.BlockSpec((D, FF), lambda i: (0, 0)),
                  pl.BlockSpec((D, FF), lambda i: (0, 0)),
                  pl.BlockSpec((FF, D), lambda i: (0, 0)),
                  vec(), vec()],
        out_specs=row(),
        out_shape=jax.ShapeDtypeStruct((T, D), F32),
        scratch_shapes=[pltpu.VMEM((TOP_K, tm) + ROW_TILE, jnp.uint32), pltpu.SemaphoreType.DMA(())],
        compiler_params=_params(("arbitrary",), 40),
        name="combine",
    )(dest, ys, gates_t, x1, x1b, wsg, wsu, wsd, g.reshape(1, D), b.reshape(1, D))


def kernel(x, w_in, conv_w, gdn_a_log, gdn_dt_bias, gdn_norm_w, diff_lambda, diff_subln_w, rel_bias_table,
           w_branch_a, w_branch_b, w_out, ln1_g, ln1_b, w_router, router_bias, w_gate, w_up, w_down,
           ws_gate, ws_up, ws_down, ln2_g, ln2_b):
    B, T, D = x.shape
    assert B == 1 and D == D_MODEL
    x2 = x.reshape(T, D)
    xb = x2.astype(BF16)
    w_in_t = jnp.swapaxes(w_in, 1, 2)
    qkv = _proj(xb, w_in_t, 0, 3, 1024, BF16)
    g4 = _proj(xb, w_in_t, 3, 4, 1024, F32)
    gab = _proj(xb, w_in_t, 56, 1, LANES, F32)

    tq = min(512, T)
    ya = _attn(qkv, _attn_bias_band(rel_bias_table.astype(F32), tq), diff_lambda[0].astype(F32),
               diff_subln_w[0].astype(F32), tq)
    yb = _gdn(g4, gab, conv_w[0].astype(F32), gdn_a_log[0], gdn_dt_bias[0], gdn_norm_w[0])
    merged = _merge(ya, yb, xb, w_branch_a[0], w_branch_b[0], w_in_t)
    x1, x1b, x1p = _outln(merged, w_out[0].astype(BF16), x2, ln1_g[0], ln1_b[0])

    eidx, rank, gates, counts = _router(x1, w_router[0], router_bias[0])
    nblk = _moe_blocks(T * TOP_K)
    starts, pad_lo, pad_hi, sched = _moe_layout(counts.reshape(N_EXPERTS), nblk)
    dest = _dest(starts, eidx, rank)
    xs = _dispatch(dest, x1p, pad_lo, pad_hi, sched[-1], nblk)
    ys = _gmm(xs, sched, w_gate, w_up, w_down)
    out = _combine(dest, ys, gates.T, x1, x1b, ws_gate[0].astype(BF16), ws_up[0].astype(BF16),
                   ws_down[0].astype(BF16), ln2_g[0], ln2_b[0])
    return out.reshape(B, T, D)
```

```python
import functools
import math

import jax
import jax.numpy as jnp
from jax import lax
from jax.experimental import pallas as pl
from jax.experimental.pallas import tpu as pltpu

F32 = jnp.float32
BF16 = jnp.bfloat16
I32 = jnp.int32

D_MODEL = 2048
CHUNK = 64
DIFF_HEADS = 8
DIFF_HEAD_DIM = 64
GDN_HEADS = 8
GDN_HEAD_DIM = 128
CONV_WIDTH = 4
REL_BUCKETS = 32
REL_MAX_DISTANCE = 128
N_EXPERTS = 64
TOP_K = 8
N_GROUPS = 8
TOPK_GROUPS = 4
EXPERT_FF = 512
ROUTED_SCALE = 2.5
DEPTH = 1
DEEPNORM_ALPHA = (2 * DEPTH) ** 0.25
LN_EPS = 1e-5
NORM_EPS = 1e-6
LAMBDA_INIT = 0.8 - 0.6 * math.exp(-0.3 * 0)

LANES = 128
MIB = 1024 * 1024
NEG_BIG = -1e30
LOG2E = math.log2(math.e)
GMM_ROWS = 256
ROW_TILE = (D_MODEL // 2 // LANES, LANES)


def _params(semantics, vmem_mib):
    return pltpu.CompilerParams(dimension_semantics=semantics, vmem_limit_bytes=vmem_mib * MIB)


def _sigmoid(x):
    return 1.0 / (1.0 + jnp.exp(-x))


def _silu(x):
    return x * _sigmoid(x)


def _dot(a, b):
    return jnp.dot(a, b, preferred_element_type=F32)


def _dot_nt(a, b):
    return lax.dot_general(a, b, (((1,), (1,)), ((), ())), preferred_element_type=F32)


def _pack_halves(y):
    n = y.shape[1] // 2
    bits = lambda v: lax.bitcast_convert_type(v.astype(BF16).astype(F32), jnp.uint32)
    return (bits(y[:, :n]) >> 16) | (bits(y[:, n:]) & jnp.uint32(0xFFFF0000))


def _unpack_halves(p):
    lo = lax.bitcast_convert_type(p << 16, F32)
    hi = lax.bitcast_convert_type(p & jnp.uint32(0xFFFF0000), F32)
    return jnp.concatenate([lo, hi], axis=1)


def _store_row_tiles(ref, packed):
    ref[...] = packed.reshape(ref.shape)


def _load_row_tiles(ref):
    return ref[...].reshape(ref.shape[0], -1)


def _split3(a):
    hi = a.astype(BF16)
    r1 = a - hi.astype(F32)
    mid = r1.astype(BF16)
    lo = (r1 - mid.astype(F32)).astype(BF16)
    return hi, mid, lo


def _proj_kernel(x_ref, w_ref, o_ref, wbf_ref, *, tn):
    @pl.when(pl.program_id(1) == 0)
    def _():
        wbf_ref[...] = w_ref[...].astype(BF16)

    acc = _dot_nt(x_ref[...], wbf_ref[...])
    for j in range(tn // LANES):
        o_ref[j] = acc[:, j * LANES:(j + 1) * LANES].astype(o_ref.dtype)


def _proj(xb, wt, blk0, nblk, tn, odt):
    T, D = xb.shape
    tm = min(1024, T)
    return pl.pallas_call(
        functools.partial(_proj_kernel, tn=tn),
        grid=(nblk, T // tm),
        in_specs=[pl.BlockSpec((tm, D), lambda n, m: (m, 0)),
                  pl.BlockSpec((None, tn, D), lambda n, m: (0, blk0 + n, 0))],
        out_specs=pl.BlockSpec((tn // LANES, tm, LANES), lambda n, m: (n, m, 0)),
        out_shape=jax.ShapeDtypeStruct((nblk * tn // LANES, T, LANES), odt),
        scratch_shapes=[pltpu.VMEM((tn, D), BF16)],
        compiler_params=_params(("arbitrary", "arbitrary"), 48),
        name="proj",
    )(xb, wt)


def _attn_kernel(q_ref, k_ref, v_ref, band_ref, lam_ref, subw_ref, o_ref,
                 m1, l1, a1, m2, l2, a2, s_buf, bias_s, *, tq):
    qi = pl.program_id(1)

    @pl.when(qi == 0)
    def _():
        toep = pltpu.roll(jnp.broadcast_to(band_ref[0], (tq, 3 * tq)), 0, 1, stride=1, stride_axis=0)
        qpos = lax.broadcasted_iota(I32, (tq, tq), 0)
        kpos = lax.broadcasted_iota(I32, (tq, tq), 1)
        visible = (kpos // CHUNK) <= (qpos // CHUNK)
        bias_s[0] = jnp.where(visible, toep[:, 2 * tq:], NEG_BIG)
        bias_s[1] = toep[:, tq:2 * tq]

    q = q_ref[0].astype(F32) * (DIFF_HEAD_DIM ** -0.5 * LOG2E)
    ones = jnp.ones((tq, LANES), BF16)
    lane = lax.broadcasted_iota(I32, q.shape, 1)
    q1 = jnp.where(lane < DIFF_HEAD_DIM, q, 0.0).astype(BF16)
    q2 = jnp.where(lane >= DIFF_HEAD_DIM, q, 0.0).astype(BF16)
    for m, l, a in ((m1, l1, a1), (m2, l2, a2)):
        m[...] = jnp.full(m.shape, NEG_BIG, F32)
        l[...] = jnp.zeros(l.shape, F32)
        a[...] = jnp.zeros(a.shape, F32)

    def scores(j, slot):
        k = k_ref[0, pl.ds(pl.multiple_of(j * tq, tq), tq), :]
        s_buf[slot, 0] = _dot_nt(q1, k)
        s_buf[slot, 1] = _dot_nt(q2, k)

    def accumulate(j, slot, bias_idx):
        v = jnp.concatenate([v_ref[0, pl.ds(pl.multiple_of(j * tq, tq), tq), :], ones], axis=1)
        for i, (m, l, a) in enumerate(((m1, l1, a1), (m2, l2, a2))):
            s = s_buf[slot, i]
            if bias_idx is not None:
                s = s + bias_s[bias_idx]
            m_old = m[...]
            m_new = jnp.maximum(m_old, jnp.max(s, axis=1, keepdims=True))
            alpha = jnp.exp2(m_old - m_new)
            p = jnp.exp2(s - jnp.concatenate([m_new] * (tq // LANES), axis=1))
            pv = _dot(p.astype(BF16), v)
            l[...] = alpha * l[...] + pv[:, LANES:]
            a[...] = alpha * a[...] + pv[:, :LANES]
            m[...] = m_new

    n_far = jnp.maximum(qi - 1, 0)
    odd = lax.rem(n_far, 2)

    @pl.when(odd == 1)
    def _():
        scores(0, 1)
        scores(1, 0)
        accumulate(0, 1, None)

    @pl.when(odd == 0)
    def _():
        scores(0, 0)

    @pl.loop(0, n_far // 2)
    def _(t):
        base = odd + 2 * t
        scores(base + 1, 1)
        accumulate(base, 0, None)
        scores(base + 2, 0)
        accumulate(base + 1, 1, None)

    @pl.when(qi >= 1)
    def _():
        scores(qi, 1)
        accumulate(qi - 1, 0, 1)
        accumulate(qi, 1, 0)

    @pl.when(qi == 0)
    def _():
        accumulate(0, 0, 0)

    lf = lam_ref[...]
    lam = (jnp.exp(jnp.sum(lf[0:1] * lf[1:2], axis=1, keepdims=True))
           - jnp.exp(jnp.sum(lf[2:3] * lf[3:4], axis=1, keepdims=True)) + LAMBDA_INIT)
    o = a1[...] / l1[...] - lam * (a2[...] / l2[...])
    y = o * lax.rsqrt(jnp.mean(o * o, axis=1, keepdims=True) + NORM_EPS) * subw_ref[...] * (1.0 - LAMBDA_INIT)
    o_ref[...] = y.astype(o_ref.dtype)


def _t5_bucket(rel):
    nb = REL_BUCKETS // 2
    base = jnp.where(rel > 0, nb, 0)
    n = jnp.abs(rel)
    max_exact = nb // 2
    nf = jnp.maximum(n, 1).astype(F32)
    large = max_exact + (jnp.log(nf / max_exact) / math.log(REL_MAX_DISTANCE / max_exact)
                         * (nb - max_exact)).astype(I32)
    large = jnp.minimum(large, nb - 1)
    return base + jnp.where(n < max_exact, n, large)


def _attn_bias_band(rel_table, tq):
    span = 3 * tq
    rel = jnp.arange(span) - 2 * tq
    onehot = (_t5_bucket(rel)[:, None] == jnp.arange(REL_BUCKETS)[None, :]).astype(F32)
    far = rel_table[REL_BUCKETS // 2 - 1]
    band = (jnp.dot(onehot, rel_table, precision=lax.Precision.HIGHEST) - far).T
    return (band * LOG2E)[:, None, :]


def _attn(qkv, bias_band, lam_vecs, subln_w, tq):
    nb, T, _ = qkv.shape
    H = DIFF_HEADS
    return pl.pallas_call(
        functools.partial(_attn_kernel, tq=tq),
        grid=(H, T // tq),
        in_specs=[pl.BlockSpec((1, tq, LANES), lambda h, i: (h, i, 0)),
                  pl.BlockSpec((1, T, LANES), lambda h, i: (H + h, 0, 0)),
                  pl.BlockSpec((1, T, LANES), lambda h, i: (2 * H + h, 0, 0)),
                  pl.BlockSpec((1, 1, 3 * tq), lambda h, i: (h, 0, 0)),
                  pl.BlockSpec((4, DIFF_HEAD_DIM), lambda h, i: (0, 0)),
                  pl.BlockSpec((1, LANES), lambda h, i: (0, 0))],
        out_specs=pl.BlockSpec((tq, LANES), lambda h, i: (i, h)),
        out_shape=jax.ShapeDtypeStruct((T, H * LANES), BF16),
        scratch_shapes=[pltpu.VMEM((tq, LANES), F32)] * 6 + [pltpu.VMEM((2, 2, tq, tq), F32), pltpu.VMEM((2, tq, tq), F32)],
        compiler_params=_params(("arbitrary", "arbitrary"), 40),
        name="attn",
    )(qkv, qkv, qkv, bias_band, lam_vecs, subln_w.reshape(1, LANES))


GDN_CHUNKS_PER_STEP = 4


def _gdn_kernel(g4_ref, gab_ref, convw_ref, alog_ref, dtb_ref, normw_ref, o_ref,
                s_ref, xbuf_ref):
    C = CHUNK
    H = GDN_HEADS
    R = GDN_CHUNKS_PER_STEP * C
    n = pl.program_id(0)

    @pl.when(n == 0)
    def _():
        s_ref[...] = jnp.zeros(s_ref.shape, F32)
        xbuf_ref[...] = jnp.zeros(xbuf_ref.shape, F32)

    gab = gab_ref[0]
    sp = gab + dtb_ref[...]
    softplus = jnp.maximum(sp, 0.0) + jnp.log(1.0 + jnp.exp(-jnp.abs(sp)))
    g = -jnp.exp(alog_ref[...]) * softplus
    beta_all = _sigmoid(gab)
    row_in_chunk = lax.broadcasted_iota(I32, (R, LANES), 0) % C
    gc = g
    for sh in (1, 2, 4, 8, 16, 32):
        gc = gc + jnp.where(row_in_chunk >= sh, pltpu.roll(gc, sh, 0), 0.0)
    gc_t = gc.T

    ri = lax.broadcasted_iota(I32, (C, C), 0)
    ci = lax.broadcasted_iota(I32, (C, C), 1)
    tri = ri >= ci
    strict = ri > ci
    eye = (ri == ci).astype(F32)

    for c in range(3 * H):
        xbuf_ref[c, 8:8 + R, :] = g4_ref[c]

    def conv(c):
        acc = xbuf_ref[c, 8 - (CONV_WIDTH - 1):8 - (CONV_WIDTH - 1) + R, :] * convw_ref[0, c:c + 1, :]
        for j in range(1, CONV_WIDTH):
            lo = 8 - (CONV_WIDTH - 1) + j
            acc = acc + xbuf_ref[c, lo:lo + R, :] * convw_ref[j, c:c + 1, :]
        return _silu(acc)

    def l2n(x):
        return x * lax.rsqrt(jnp.sum(x * x, axis=1, keepdims=True) + NORM_EPS)

    scale = GDN_HEAD_DIM ** -0.5
    heads = range(H)
    chunks = range(GDN_CHUNKS_PER_STEP)
    units = [(c, h) for c in chunks for h in heads]
    rows = lambda x, c: x[c * C:(c + 1) * C]
    q_all = [l2n(conv(h)) * scale for h in heads]
    k_all = [l2n(conv(H + h)) for h in heads]
    v_all = [conv(2 * H + h) for h in heads]
    q = {(c, h): rows(q_all[h], c) for c, h in units}
    k = {(c, h): rows(k_all[h], c) for c, h in units}
    v = {(c, h): rows(v_all[h], c) for c, h in units}
    beta = {(c, h): rows(beta_all, c)[:, 8 + h:9 + h] for c, h in units}
    gcol = {(c, h): rows(gc, c)[:, h:h + 1] for c, h in units}
    glast = {(c, h): gc[(c + 1) * C - 1:(c + 1) * C, h:h + 1] for c, h in units}
    eg = {u: jnp.exp(gcol[u]) for u in units}
    decay = {(c, h): jnp.exp(jnp.where(tri, gcol[c, h] - gc_t[h:h + 1, c * C:(c + 1) * C], -jnp.inf))
             for c, h in units}
    kb = {u: k[u] * beta[u] for u in units}
    prod = {u: _dot_nt(jnp.concatenate([kb[u], q[u]], axis=0).astype(BF16), k[u].astype(BF16))
            for u in units}
    lmat = {u: jnp.where(strict, prod[u][:C] * decay[u], 0.0) for u in units}
    aqk = {u: jnp.where(tri, prod[u][C:] * decay[u], 0.0).astype(BF16) for u in units}
    tm = {u: eye - lmat[u] for u in units}
    pw = lmat
    for _ in range(5):
        pwb = {u: pw[u].astype(BF16) for u in units}
        pw = {u: _dot(pwb[u], pwb[u]) for u in units}
        tm = {u: tm[u] + _dot(tm[u].astype(BF16), pw[u].astype(BF16)) for u in units}
    uw = {u: _dot(tm[u].astype(BF16),
                  jnp.concatenate([v[u] * beta[u], kb[u] * eg[u]], axis=1).astype(BF16)) for u in units}
    wq = {u: jnp.concatenate([uw[u][:, LANES:], q[u] * eg[u]], axis=0).astype(BF16) for u in units}
    k_dec = {u: (k[u] * jnp.exp(glast[u] - gcol[u])).astype(BF16) for u in units}
    s_cur = [s_ref[h] for h in heads]
    o = {}
    for c in chunks:
        sd = [_dot(wq[c, h], s_cur[h].astype(BF16)) for h in heads]
        vnb = [(uw[c, h][:, :LANES] - sd[h][:C]).astype(BF16) for h in heads]
        for h in heads:
            o[c, h] = sd[h][C:] + _dot(aqk[c, h], vnb[h])
        s_cur = [s_cur[h] * jnp.exp(glast[c, h]) + lax.dot_general(
            k_dec[c, h], vnb[h], (((0,), (0,)), ((), ())), preferred_element_type=F32) for h in heads]
    for h in heads:
        s_ref[h] = s_cur[h]
    for c, h in units:
        on = o[c, h] * lax.rsqrt(jnp.mean(o[c, h] * o[c, h], axis=1, keepdims=True) + NORM_EPS) * normw_ref[...]
        z = g4_ref[3 * H + h, c * C:(c + 1) * C, :]
        o_ref[c * C:(c + 1) * C, h * LANES:(h + 1) * LANES] = (on * _silu(z)).astype(o_ref.dtype)

    for c in range(3 * H):
        xbuf_ref[c, 0:8, :] = g4_ref[c, R - 8:R, :]


def _gdn(g4, gab, conv_w, a_log, dt_bias, norm_w):
    nb, T, _ = g4.shape
    H = GDN_HEADS
    C = GDN_CHUNKS_PER_STEP * CHUNK
    pad = lambda v, off: jnp.zeros((1, LANES), F32).at[0, off:off + H].set(v.astype(F32))
    return pl.pallas_call(
        _gdn_kernel,
        grid=(T // C,),
        in_specs=[pl.BlockSpec((4 * H, C, LANES), lambda n: (0, n, 0)),
                  pl.BlockSpec((1, C, LANES), lambda n: (0, n, 0)),
                  pl.BlockSpec((CONV_WIDTH, 3 * H, LANES), lambda n: (0, 0, 0)),
                  pl.BlockSpec((1, LANES), lambda n: (0, 0)),
                  pl.BlockSpec((1, LANES), lambda n: (0, 0)),
                  pl.BlockSpec((1, LANES), lambda n: (0, 0))],
        out_specs=pl.BlockSpec((C, H * LANES), lambda n: (n, 0)),
        out_shape=jax.ShapeDtypeStruct((T, H * LANES), BF16),
        scratch_shapes=[pltpu.VMEM((H, GDN_HEAD_DIM, GDN_HEAD_DIM), F32),
                        pltpu.VMEM((3 * H, 8 + C, LANES), F32)],
        compiler_params=_params(("arbitrary",), 32),
        name="gdn",
    )(g4, gab, conv_w.reshape(CONV_WIDTH, 3 * H, LANES), pad(a_log, 0), pad(dt_bias, 0),
      norm_w.reshape(1, LANES).astype(F32))


GATE_COL = 7 * 1024 + 2 * GDN_HEADS


def _merge_kernel(ya_ref, yb_ref, x_ref, wa_ref, wb_ref, wga_ref, wgb_ref, o_ref,
                  wa_s, wb_s, wga_s, wgb_s):
    @pl.when(pl.program_id(1) == 0)
    def _():
        wa_s[...] = wa_ref[...].astype(BF16)
        wb_s[...] = wb_ref[...].astype(BF16)
        wga_s[...] = wga_ref[0].astype(BF16)
        wgb_s[...] = wgb_ref[0].astype(BF16)

    x = x_ref[...]
    a = _dot(ya_ref[...], wa_s[...])
    b = _dot(yb_ref[...], wb_s[...])
    ga = _sigmoid(_dot_nt(x, wga_s[...]))
    gb = _sigmoid(_dot_nt(x, wgb_s[...]))
    o_ref[...] = (ga * a + gb * b).astype(o_ref.dtype)


def _merge(ya, yb, xb, w_a, w_b, w_in_t):
    T, D = xb.shape
    W = ya.shape[1]
    tm = min(512, T)
    tn = 512
    wspec = lambda rows: pl.BlockSpec((rows, tn), lambda n, m: (0, n))
    gspec = lambda off: pl.BlockSpec((pl.Element(1), pl.Element(tn), pl.Element(D)),
                                     lambda n, m: (0, pl.multiple_of(GATE_COL + off + n * tn, 16), 0))
    return pl.pallas_call(
        _merge_kernel,
        grid=(D // tn, T // tm),
        in_specs=[pl.BlockSpec((tm, W), lambda n, m: (m, 0)),
                  pl.BlockSpec((tm, W), lambda n, m: (m, 0)),
                  pl.BlockSpec((tm, D), lambda n, m: (m, 0)),
                  wspec(W), wspec(W), gspec(0), gspec(D)],
        out_specs=pl.BlockSpec((tm, tn), lambda n, m: (m, n)),
        out_shape=jax.ShapeDtypeStruct((T, D), BF16),
        scratch_shapes=[pltpu.VMEM((W, tn), BF16), pltpu.VMEM((W, tn), BF16),
                        pltpu.VMEM((tn, D), BF16), pltpu.VMEM((tn, D), BF16)],
        compiler_params=_params(("arbitrary", "arbitrary"), 48),
        name="merge",
    )(ya, yb, xb, w_a, w_b, w_in_t, w_in_t)


def _layer_norm(x, g, b):
    mu = jnp.mean(x, axis=1, keepdims=True)
    xc = x - mu
    var = jnp.mean(xc * xc, axis=1, keepdims=True)
    return xc * lax.rsqrt(var + LN_EPS) * g + b


def _outln_kernel(mg_ref, w_ref, x_ref, g_ref, b_ref, o_ref, ob_ref, op_ref):
    h = _dot(mg_ref[...], w_ref[...])
    y = _layer_norm(DEEPNORM_ALPHA * x_ref[...] + h, g_ref[...], b_ref[...])
    o_ref[...] = y
    ob_ref[...] = y.astype(BF16)
    _store_row_tiles(op_ref, _pack_halves(y))


def _outln(merged, w_out_bf, x, g, b):
    T, D = x.shape
    tm = min(512, T)
    row = lambda: pl.BlockSpec((tm, D), lambda m: (m, 0))
    vec = lambda: pl.BlockSpec((1, D), lambda m: (0, 0))
    return pl.pallas_call(
        _outln_kernel,
        grid=(T // tm,),
        in_specs=[row(), pl.BlockSpec((D, D), lambda m: (0, 0)), row(), vec(), vec()],
        out_specs=[row(), row(), pl.BlockSpec((tm,) + ROW_TILE, lambda m: (m, 0, 0))],
        out_shape=[jax.ShapeDtypeStruct((T, D), F32), jax.ShapeDtypeStruct((T, D), BF16),
                   jax.ShapeDtypeStruct((T,) + ROW_TILE, jnp.uint32)],
        compiler_params=_params(("arbitrary",), 48),
        name="outln",
    )(merged, w_out_bf, x, g.reshape(1, D), b.reshape(1, D))


def _router_kernel(x_ref, wr_ref, bias_ref, eidx_ref, rank_ref, gate_ref, cnt_ref,
                   run_ref, tri_ref, *, tm):
    G = N_GROUPS
    P = N_EXPERTS // N_GROUPS
    i = pl.program_id(0)

    @pl.when(i == 0)
    def _():
        run_ref[...] = jnp.zeros(run_ref.shape, F32)
        r = lax.broadcasted_iota(I32, (tm, tm), 0)
        c = lax.broadcasted_iota(I32, (tm, tm), 1)
        tri_ref[...] = jnp.where(r < c, 1.0, 0.0).astype(BF16)

    w3 = _split3(wr_ref[...])
    x3 = _split3(x_ref[...])
    logits = _dot_nt(w3[1], x3[1])
    for a, b in ((0, 2), (2, 0), (0, 1), (1, 0), (0, 0)):
        logits = logits + _dot_nt(w3[a], x3[b])
    scores = _sigmoid(logits)
    choice = scores + bias_ref[...]
    c3 = choice.reshape(G, P, tm)
    s3 = scores.reshape(G, P, tm)
    j_iota = lax.broadcasted_iota(I32, (G, P, tm), 1)
    g_iota = lax.broadcasted_iota(I32, (G, 1, tm), 0)
    e_iota = lax.broadcasted_iota(I32, (G, P, tm), 0) * P + j_iota
    ninf = -jnp.inf

    def amax3(v, idx, big):
        m = jnp.max(jnp.max(v, axis=1, keepdims=True), axis=0, keepdims=True)
        f = jnp.min(jnp.min(jnp.where(v == m, idx, big), axis=1, keepdims=True), axis=0, keepdims=True)
        return m, f

    m1 = jnp.max(c3, axis=1, keepdims=True)
    f1 = jnp.min(jnp.where(c3 == m1, j_iota, P), axis=1, keepdims=True)
    m2 = jnp.max(jnp.where(j_iota == f1, ninf, c3), axis=1, keepdims=True)
    gsc = m1 + m2
    gsel = jnp.zeros((G, 1, tm), jnp.bool_)
    for _ in range(TOPK_GROUPS):
        _, f = amax3(gsc, g_iota, G)
        hit = g_iota == f
        gsel = jnp.logical_or(gsel, hit)
        gsc = jnp.where(hit, ninf, gsc)
    cm = jnp.where(gsel, c3, ninf)

    picks = []
    wts = []
    msel = jnp.zeros((G, P, tm), F32)
    for _ in range(TOP_K):
        _, f = amax3(cm, e_iota, N_EXPERTS)
        hit = e_iota == f
        wts.append(jnp.sum(jnp.sum(jnp.where(hit, s3, 0.0), axis=1, keepdims=True), axis=0, keepdims=True))
        picks.append(f)
        msel = jnp.where(hit, 1.0, msel)
        cm = jnp.where(hit, ninf, cm)
    wsum = wts[0]
    for w in wts[1:]:
        wsum = wsum + w

    m2d = msel.reshape(N_EXPERTS, tm)
    before = run_ref[...] + _dot(m2d.astype(BF16), tri_ref[...])
    b3 = before.reshape(G, P, tm)
    for k in range(TOP_K):
        hit = e_iota == picks[k]
        rk = jnp.sum(jnp.sum(jnp.where(hit, b3, 0.0), axis=1, keepdims=True), axis=0, keepdims=True)
        eidx_ref[k:k + 1, :] = picks[k].reshape(1, tm)
        rank_ref[k:k + 1, :] = rk.reshape(1, tm).astype(I32)
        gate_ref[k:k + 1, :] = (wts[k] / wsum * ROUTED_SCALE).reshape(1, tm)
    run_ref[...] = run_ref[...] + jnp.sum(m2d, axis=1, keepdims=True)
    cnt_ref[...] = run_ref[...].astype(I32)


def _router(x1, w_router, router_bias):
    T, D = x1.shape
    tm = min(512, T)
    E = N_EXPERTS
    tok = lambda: pl.BlockSpec((TOP_K, tm), lambda i: (0, i))
    return pl.pallas_call(
        functools.partial(_router_kernel, tm=tm),
        grid=(T // tm,),
        in_specs=[pl.BlockSpec((tm, D), lambda i: (i, 0)),
                  pl.BlockSpec((E, D), lambda i: (0, 0)),
                  pl.BlockSpec((E, 1), lambda i: (0, 0))],
        out_specs=[tok(), tok(), tok(), pl.BlockSpec((E, 1), lambda i: (0, 0))],
        out_shape=[jax.ShapeDtypeStruct((TOP_K, T), I32), jax.ShapeDtypeStruct((TOP_K, T), I32),
                   jax.ShapeDtypeStruct((TOP_K, T), F32), jax.ShapeDtypeStruct((E, 1), I32)],
        scratch_shapes=[pltpu.VMEM((E, 1), F32), pltpu.VMEM((tm, tm), BF16)],
        compiler_params=_params(("arbitrary",), 32),
        name="router",
    )(x1, w_router.astype(F32).T, router_bias.astype(F32).reshape(E, 1))


def _dest_kernel(starts_ref, eidx_ref, rank_ref, o_ref):
    e = eidx_ref[...]
    acc = rank_ref[...]
    for j in range(N_EXPERTS):
        acc = acc + jnp.where(e == j, starts_ref[j], 0)
    o_ref[...] = acc


def _dest(starts, eidx, rank):
    full = lambda: pl.BlockSpec(eidx.shape, lambda i, s: (0, 0))
    return pl.pallas_call(
        _dest_kernel,
        grid_spec=pltpu.PrefetchScalarGridSpec(num_scalar_prefetch=1, grid=(1,),
                                               in_specs=[full(), full()], out_specs=full()),
        out_shape=jax.ShapeDtypeStruct(eidx.shape, I32),
        compiler_params=_params(("arbitrary",), 16),
        name="dest",
    )(starts, eidx, rank)


def _dispatch_kernel(dest_ref, lo_ref, hi_ref, nvis_ref, x_ref, xs_hbm, zero_ref, sem, *, tm, nblk):
    i = pl.program_id(0)

    def row_copy(t, k):
        return pltpu.make_async_copy(x_ref.at[pl.ds(t, 1)], xs_hbm.at[pl.ds(dest_ref[k, t], 1)], sem)

    def issue(t, carry):
        for k in range(TOP_K):
            row_copy(t, k).start(priority=k % 2)
        return carry

    def drain(t, carry):
        for k in range(TOP_K):
            row_copy(t, k).wait()
        return carry

    lax.fori_loop(0, tm, issue, 0)
    lax.fori_loop(0, tm, drain, 0)

    @pl.when(i == pl.num_programs(0) - 1)
    def _():
        zero_ref[...] = jnp.zeros(zero_ref.shape, zero_ref.dtype)

        def for_each_fill(fn):
            def per_expert(e, carry):
                n = hi_ref[e] - lo_ref[e]
                off = lo_ref[e]
                size = GMM_ROWS // 2
                while size >= 1:
                    @pl.when((n & size) != 0)
                    def _(off=off, size=size):
                        fn(pltpu.make_async_copy(zero_ref.at[pl.ds(0, size)], xs_hbm.at[pl.ds(off, size)], sem))
                    off = off + (n & size)
                    size //= 2
                return carry

            def per_block(b, carry):
                start = pl.multiple_of(b * GMM_ROWS, GMM_ROWS)
                fn(pltpu.make_async_copy(zero_ref, xs_hbm.at[pl.ds(start, GMM_ROWS)], sem))
                return carry

            lax.fori_loop(0, N_EXPERTS, per_expert, 0)
            lax.fori_loop(nvis_ref[0], nblk, per_block, 0)

        for_each_fill(lambda c: c.start())
        for_each_fill(lambda c: c.wait())


def _dispatch(dest, x1p, pad_lo, pad_hi, nvis, nblk):
    T = x1p.shape[0]
    tm = min(256, T)
    smem = lambda: pl.BlockSpec(memory_space=pltpu.SMEM)
    return pl.pallas_call(
        functools.partial(_dispatch_kernel, tm=tm, nblk=nblk),
        grid=(T // tm,),
        in_specs=[pl.BlockSpec((TOP_K, tm), lambda i: (0, i), memory_space=pltpu.SMEM),
                  smem(), smem(), smem(),
                  pl.BlockSpec((tm,) + ROW_TILE, lambda i: (i, 0, 0))],
        out_specs=pl.BlockSpec(memory_space=pl.ANY),
        out_shape=jax.ShapeDtypeStruct((nblk * GMM_ROWS,) + ROW_TILE, x1p.dtype),
        scratch_shapes=[pltpu.VMEM((GMM_ROWS,) + ROW_TILE, x1p.dtype), pltpu.SemaphoreType.DMA(())],
        compiler_params=_params(("arbitrary",), 16),
        name="dispatch",
    )(dest, pad_lo, pad_hi, nvis, x1p)


def _gmm_kernel(vblk, vexp, vchg, vslot, vnext, nvis, xs_ref, wg_hbm, wu_hbm, wd_hbm, o_ref,
                wg_buf, wu_buf, wd_buf, wg_s, wu_s, wd_s, sem):
    i = pl.program_id(0)
    e = vexp[i]

    def weight_copies(expert, slot):
        return [pltpu.make_async_copy(w_hbm.at[0, expert], buf.at[slot], sem.at[slot, j])
                for j, (w_hbm, buf) in enumerate(((wg_hbm, wg_buf), (wu_hbm, wu_buf), (wd_hbm, wd_buf)))]

    @pl.when(i == 0)
    def _():
        for c in weight_copies(e, vslot[0]):
            c.start(priority=1)

    @pl.when(vchg[i] == 1)
    def _():
        slot = vslot[i]
        for c in weight_copies(e, slot):
            c.wait()

        @pl.when(vnext[i] >= 0)
        def _():
            for c in weight_copies(vnext[i], 1 - slot):
                c.start(priority=1)

        wg_s[...] = wg_buf[slot].astype(BF16)
        wu_s[...] = wu_buf[slot].astype(BF16)
        wd_s[...] = wd_buf[slot].astype(BF16)

    @pl.when(i < nvis[0])
    def _():
        x = _unpack_halves(_load_row_tiles(xs_ref)).astype(BF16)
        g = _dot(x, wg_s[...])
        u = _dot(x, wu_s[...])
        _store_row_tiles(o_ref, _pack_halves(_dot((_silu(g) * u).astype(BF16), wd_s[...])))

    @pl.when(i >= nvis[0])
    def _():
        o_ref[...] = jnp.zeros(o_ref.shape, o_ref.dtype)


def _moe_blocks(n_assign):
    return n_assign // GMM_ROWS + N_EXPERTS


def _moe_layout(counts, nblk):
    E = N_EXPERTS
    padded = (counts + (GMM_ROWS - 1)) // GMM_ROWS * GMM_ROWS
    pend = jnp.cumsum(padded).astype(I32)
    pstart = pend - padded
    nvis = pend[-1] // GMM_ROWS
    count_le = lambda table, x: jnp.sum((table[None, :] <= x[:, None]).astype(I32), axis=1)
    take = lambda table, idx: jnp.sum(jnp.where(idx[:, None] == jnp.arange(table.shape[0], dtype=I32)[None, :],
                                                table[None, :], 0), axis=1)
    vblk = jnp.minimum(jnp.arange(nblk, dtype=I32), nvis - 1)
    vexp = jnp.minimum(count_le(pend, vblk * GMM_ROWS), E - 1)
    vchg = jnp.concatenate([jnp.ones((1,), I32), (vexp[1:] != vexp[:-1]).astype(I32)])
    vslot = (jnp.cumsum(vchg) - 1) % 2
    pos = jnp.where(vchg == 1, jnp.arange(nblk, dtype=I32), nblk)
    nxt = jnp.concatenate([lax.cummin(pos[::-1])[::-1][1:], jnp.full((1,), nblk, I32)])
    vnext = jnp.where(nxt < nblk, take(vexp, jnp.minimum(nxt, nblk - 1)), -1)
    sched = (vblk, vexp, vchg, vslot.astype(I32), vnext.astype(I32), nvis.reshape(1).astype(I32))
    return pstart, pstart + counts, pend, sched


def _gmm(xs, sched, w_gate, w_up, w_down):
    R = xs.shape[0]
    D = D_MODEL
    FF = EXPERT_FF
    rows = lambda: pl.BlockSpec((GMM_ROWS,) + ROW_TILE, lambda i, vb, *_: (vb[i], 0, 0))
    hbm = lambda: pl.BlockSpec(memory_space=pl.ANY)
    grid_spec = pltpu.PrefetchScalarGridSpec(
        num_scalar_prefetch=len(sched),
        grid=(R // GMM_ROWS,),
        in_specs=[rows(), hbm(), hbm(), hbm()],
        out_specs=pl.BlockSpec((GMM_ROWS,) + ROW_TILE, lambda i, *_: (i, 0, 0)),
        scratch_shapes=[pltpu.VMEM((2, D, FF), F32), pltpu.VMEM((2, D, FF), F32), pltpu.VMEM((2, FF, D), F32),
                        pltpu.VMEM((D, FF), BF16), pltpu.VMEM((D, FF), BF16), pltpu.VMEM((FF, D), BF16),
                        pltpu.SemaphoreType.DMA((2, 3))],
    )
    return pl.pallas_call(
        _gmm_kernel,
        grid_spec=grid_spec,
        out_shape=jax.ShapeDtypeStruct((R,) + ROW_TILE, jnp.uint32),
        compiler_params=_params(("arbitrary",), 56),
        name="gmm",
    )(*sched, xs, w_gate, w_up, w_down)


def _combine_kernel(dest_ref, next_ref, ys_hbm, gate_ref, x_ref, xb_ref, wsg_ref, wsu_ref, wsd_ref, g_ref, b_ref,
                    o_ref, buf_ref, sem, *, tm):
    i = pl.program_id(0)
    last = pl.num_programs(0) - 1
    slot = lax.rem(i, 2)

    def row_copy(idx_ref, s, t, k):
        return pltpu.make_async_copy(ys_hbm.at[pl.ds(idx_ref[k, t], 1)],
                                     buf_ref.at[s, k, pl.ds(t, 1)], sem.at[s])

    def rolled(fn):
        def body(t, carry):
            for k in range(TOP_K):
                fn(t, k)
            return carry
        lax.fori_loop(0, tm, body, 0)

    @pl.when(i == 0)
    def _():
        rolled(lambda t, k: row_copy(dest_ref, slot, t, k).start(priority=k % 2))

    rolled(lambda t, k: row_copy(dest_ref, slot, t, k).wait())

    for t in range(tm):
        for k in range(TOP_K):
            row_copy(next_ref, 1 - slot, t, k).start(priority=k % 2)
    xb = xb_ref[...]
    hidden = _silu(_dot(xb, wsg_ref[...])) * _dot(xb, wsu_ref[...])
    acc = DEEPNORM_ALPHA * x_ref[...] + _dot(hidden.astype(BF16), wsd_ref[...])
    gates = gate_ref[...]
    for k in range(TOP_K):
        acc = acc + gates[:, k:k + 1] * _unpack_halves(_load_row_tiles(buf_ref.at[slot, k]))
    o_ref[...] = _layer_norm(acc, g_ref[...], b_ref[...])

    @pl.when(i == last)
    def _():
        rolled(lambda t, k: row_copy(next_ref, 1 - slot, t, k).wait())


def _combine(dest, ys, gates_t, x1, x1b, wsg, wsu, wsd, g, b):
    T, D = x1.shape
    tm = min(256, T)
    FF = wsg.shape[1]
    row = lambda: pl.BlockSpec((tm, D), lambda i: (i, 0))
    vec = lambda: pl.BlockSpec((1, D), lambda i: (0, 0))
    nt = T // tm
    return pl.pallas_call(
        functools.partial(_combine_kernel, tm=tm),
        grid=(nt,),
        in_specs=[pl.BlockSpec((TOP_K, tm), lambda i: (0, i), memory_space=pltpu.SMEM),
                  pl.BlockSpec((TOP_K, tm), lambda i: (0, jnp.minimum(i + 1, nt - 1)), memory_space=pltpu.SMEM),
                  pl.BlockSpec(memory_space=pl.ANY),
                  pl.BlockSpec((tm, TOP_K), lambda i: (i, 0)),
                  row(), row(),
                  pl.BlockSpec((D, FF), lambda i: (0, 0)),
                  pl.BlockSpec((D, FF), lambda i: (0, 0)),
                  pl.BlockSpec((FF, D), lambda i: (0, 0)),
                  vec(), vec()],
        out_specs=row(),
        out_shape=jax.ShapeDtypeStruct((T, D), F32),
        scratch_shapes=[pltpu.VMEM((2, TOP_K, tm) + ROW_TILE, jnp.uint32), pltpu.SemaphoreType.DMA((2,))],
        compiler_params=_params(("arbitrary",), 56),
        name="combine",
    )(dest, dest, ys, gates_t, x1, x1b, wsg, wsu, wsd, g.reshape(1, D), b.reshape(1, D))


def kernel(x, w_in, conv_w, gdn_a_log, gdn_dt_bias, gdn_norm_w, diff_lambda, diff_subln_w, rel_bias_table,
           w_branch_a, w_branch_b, w_out, ln1_g, ln1_b, w_router, router_bias, w_gate, w_up, w_down,
           ws_gate, ws_up, ws_down, ln2_g, ln2_b):
    B, T, D = x.shape
    assert B == 1 and D == D_MODEL
    x2 = x.reshape(T, D)
    xb = x2.astype(BF16)
    w_in_t = jnp.swapaxes(w_in, 1, 2)
    qkv = _proj(xb, w_in_t, 0, 3, 1024, BF16)
    g4 = _proj(xb, w_in_t, 3, 4, 1024, F32)
    gab = _proj(xb, w_in_t, 56, 1, LANES, F32)

    tq = min(512, T)
    ya = _attn(qkv, _attn_bias_band(rel_bias_table.astype(F32), tq), diff_lambda[0].astype(F32),
               diff_subln_w[0].astype(F32), tq)
    yb = _gdn(g4, gab, conv_w[0].astype(F32), gdn_a_log[0], gdn_dt_bias[0], gdn_norm_w[0])
    merged = _merge(ya, yb, xb, w_branch_a[0], w_branch_b[0], w_in_t)
    x1, x1b, x1p = _outln(merged, w_out[0].astype(BF16), x2, ln1_g[0], ln1_b[0])

    eidx, rank, gates, counts = _router(x1, w_router[0], router_bias[0])
    nblk = _moe_blocks(T * TOP_K)
    starts, pad_lo, pad_hi, sched = _moe_layout(counts.reshape(N_EXPERTS), nblk)
    dest = _dest(starts, eidx, rank)
    xs = _dispatch(dest, x1p, pad_lo, pad_hi, sched[-1], nblk)
    ys = _gmm(xs, sched, w_gate, w_up, w_down)
    out = _combine(dest, ys, gates.T, x1, x1b, ws_gate[0].astype(BF16), ws_up[0].astype(BF16),
                   ws_down[0].astype(BF16), ln2_g[0], ln2_b[0])
    return out.reshape(B, T, D)
```

```python
import functools
import math

import jax
import jax.numpy as jnp
from jax import lax
from jax.experimental import pallas as pl
from jax.experimental.pallas import tpu as pltpu

F32 = jnp.float32
BF16 = jnp.bfloat16
I32 = jnp.int32

D_MODEL = 2048
CHUNK = 64
DIFF_HEADS = 8
DIFF_HEAD_DIM = 64
GDN_HEADS = 8
GDN_HEAD_DIM = 128
CONV_WIDTH = 4
REL_BUCKETS = 32
REL_MAX_DISTANCE = 128
N_EXPERTS = 64
TOP_K = 8
N_GROUPS = 8
TOPK_GROUPS = 4
EXPERT_FF = 512
ROUTED_SCALE = 2.5
DEPTH = 1
DEEPNORM_ALPHA = (2 * DEPTH) ** 0.25
LN_EPS = 1e-5
NORM_EPS = 1e-6
LAMBDA_INIT = 0.8 - 0.6 * math.exp(-0.3 * 0)

LANES = 128
MIB = 1024 * 1024
NEG_BIG = -1e30
LOG2E = math.log2(math.e)
GMM_ROWS = 256
ROW_TILE = (D_MODEL // 2 // LANES, LANES)


def _params(semantics, vmem_mib):
    return pltpu.CompilerParams(dimension_semantics=semantics, vmem_limit_bytes=vmem_mib * MIB)


def _sigmoid(x):
    return 1.0 / (1.0 + jnp.exp(-x))


def _silu(x):
    return x * _sigmoid(x)


def _dot(a, b):
    return jnp.dot(a, b, preferred_element_type=F32)


def _dot_nt(a, b):
    return lax.dot_general(a, b, (((1,), (1,)), ((), ())), preferred_element_type=F32)


def _pack_halves(y):
    n = y.shape[1] // 2
    bits = lambda v: lax.bitcast_convert_type(v.astype(BF16).astype(F32), jnp.uint32)
    return (bits(y[:, :n]) >> 16) | (bits(y[:, n:]) & jnp.uint32(0xFFFF0000))


def _unpack_halves(p):
    lo = lax.bitcast_convert_type(p << 16, F32)
    hi = lax.bitcast_convert_type(p & jnp.uint32(0xFFFF0000), F32)
    return jnp.concatenate([lo, hi], axis=1)


def _store_row_tiles(ref, packed):
    ref[...] = packed.reshape(ref.shape)


def _load_row_tiles(ref):
    return ref[...].reshape(ref.shape[0], -1)


def _split3(a):
    hi = a.astype(BF16)
    r1 = a - hi.astype(F32)
    mid = r1.astype(BF16)
    lo = (r1 - mid.astype(F32)).astype(BF16)
    return hi, mid, lo


def _proj_kernel(x_ref, w_ref, o_ref, wbf_ref, *, tn):
    @pl.when(pl.program_id(1) == 0)
    def _():
        wbf_ref[...] = w_ref[...].astype(BF16)

    acc = _dot_nt(x_ref[...], wbf_ref[...])
    for j in range(tn // LANES):
        o_ref[j] = acc[:, j * LANES:(j + 1) * LANES].astype(o_ref.dtype)


def _proj(xb, wt, blk0, nblk, tn, odt):
    T, D = xb.shape
    tm = min(1024, T)
    return pl.pallas_call(
        functools.partial(_proj_kernel, tn=tn),
        grid=(nblk, T // tm),
        in_specs=[pl.BlockSpec((tm, D), lambda n, m: (m, 0)),
                  pl.BlockSpec((None, tn, D), lambda n, m: (0, blk0 + n, 0))],
        out_specs=pl.BlockSpec((tn // LANES, tm, LANES), lambda n, m: (n, m, 0)),
        out_shape=jax.ShapeDtypeStruct((nblk * tn // LANES, T, LANES), odt),
        scratch_shapes=[pltpu.VMEM((tn, D), BF16)],
        compiler_params=_params(("arbitrary", "arbitrary"), 48),
        name="proj",
    )(xb, wt)


def _attn_kernel(q_ref, k_ref, v_ref, band_ref, lam_ref, subw_ref, o_ref,
                 m1, l1, a1, m2, l2, a2, s_buf, bias_s, *, tq):
    qi = pl.program_id(1)

    @pl.when(qi == 0)
    def _():
        toep = pltpu.roll(jnp.broadcast_to(band_ref[0], (tq, 3 * tq)), 0, 1, stride=1, stride_axis=0)
        qpos = lax.broadcasted_iota(I32, (tq, tq), 0)
        kpos = lax.broadcasted_iota(I32, (tq, tq), 1)
        visible = (kpos // CHUNK) <= (qpos // CHUNK)
        bias_s[0] = jnp.where(visible, toep[:, 2 * tq:], NEG_BIG)
        bias_s[1] = toep[:, tq:2 * tq]

    q = q_ref[0].astype(F32) * (DIFF_HEAD_DIM ** -0.5 * LOG2E)
    ones = jnp.ones((tq, LANES), BF16)
    lane = lax.broadcasted_iota(I32, q.shape, 1)
    q1 = jnp.where(lane < DIFF_HEAD_DIM, q, 0.0).astype(BF16)
    q2 = jnp.where(lane >= DIFF_HEAD_DIM, q, 0.0).astype(BF16)
    for m, l, a in ((m1, l1, a1), (m2, l2, a2)):
        m[...] = jnp.full(m.shape, NEG_BIG, F32)
        l[...] = jnp.zeros(l.shape, F32)
        a[...] = jnp.zeros(a.shape, F32)

    def scores(j, slot):
        k = k_ref[0, pl.ds(pl.multiple_of(j * tq, tq), tq), :]
        s_buf[slot, 0] = _dot_nt(q1, k)
        s_buf[slot, 1] = _dot_nt(q2, k)

    def accumulate(j, slot, bias_idx):
        v = jnp.concatenate([v_ref[0, pl.ds(pl.multiple_of(j * tq, tq), tq), :], ones], axis=1)
        for i, (m, l, a) in enumerate(((m1, l1, a1), (m2, l2, a2))):
            s = s_buf[slot, i]
            if bias_idx is not None:
                s = s + bias_s[bias_idx]
            m_old = m[...]
            m_new = jnp.maximum(m_old, jnp.max(s, axis=1, keepdims=True))
            alpha = jnp.exp2(m_old - m_new)
            p = jnp.exp2(s - jnp.concatenate([m_new] * (tq // LANES), axis=1))
            pv = _dot(p.astype(BF16), v)
            l[...] = alpha * l[...] + pv[:, LANES:]
            a[...] = alpha * a[...] + pv[:, :LANES]
            m[...] = m_new

    n_far = jnp.maximum(qi - 1, 0)
    odd = lax.rem(n_far, 2)

    @pl.when(odd == 1)
    def _():
        scores(0, 1)
        scores(1, 0)
        accumulate(0, 1, None)

    @pl.when(odd == 0)
    def _():
        scores(0, 0)

    @pl.loop(0, n_far // 2)
    def _(t):
        base = odd + 2 * t
        scores(base + 1, 1)
        accumulate(base, 0, None)
        scores(base + 2, 0)
        accumulate(base + 1, 1, None)

    @pl.when(qi >= 1)
    def _():
        scores(qi, 1)
        accumulate(qi - 1, 0, 1)
        accumulate(qi, 1, 0)

    @pl.when(qi == 0)
    def _():
        accumulate(0, 0, 0)

    lf = lam_ref[...]
    lam = (jnp.exp(jnp.sum(lf[0:1] * lf[1:2], axis=1, keepdims=True))
           - jnp.exp(jnp.sum(lf[2:3] * lf[3:4], axis=1, keepdims=True)) + LAMBDA_INIT)
    o = a1[...] / l1[...] - lam * (a2[...] / l2[...])
    y = o * lax.rsqrt(jnp.mean(o * o, axis=1, keepdims=True) + NORM_EPS) * subw_ref[...] * (1.0 - LAMBDA_INIT)
    o_ref[...] = y.astype(o_ref.dtype)


def _t5_bucket(rel):
    nb = REL_BUCKETS // 2
    base = jnp.where(rel > 0, nb, 0)
    n = jnp.abs(rel)
    max_exact = nb // 2
    nf = jnp.maximum(n, 1).astype(F32)
    large = max_exact + (jnp.log(nf / max_exact) / math.log(REL_MAX_DISTANCE / max_exact)
                         * (nb - max_exact)).astype(I32)
    large = jnp.minimum(large, nb - 1)
    return base + jnp.where(n < max_exact, n, large)


def _attn_bias_band(rel_table, tq):
    span = 3 * tq
    rel = jnp.arange(span) - 2 * tq
    onehot = (_t5_bucket(rel)[:, None] == jnp.arange(REL_BUCKETS)[None, :]).astype(F32)
    far = rel_table[REL_BUCKETS // 2 - 1]
    band = (jnp.dot(onehot, rel_table, precision=lax.Precision.HIGHEST) - far).T
    return (band * LOG2E)[:, None, :]


def _attn(qkv, bias_band, lam_vecs, subln_w, tq):
    nb, T, _ = qkv.shape
    H = DIFF_HEADS
    return pl.pallas_call(
        functools.partial(_attn_kernel, tq=tq),
        grid=(H, T // tq),
        in_specs=[pl.BlockSpec((1, tq, LANES), lambda h, i: (h, i, 0)),
                  pl.BlockSpec((1, T, LANES), lambda h, i: (H + h, 0, 0)),
                  pl.BlockSpec((1, T, LANES), lambda h, i: (2 * H + h, 0, 0)),
                  pl.BlockSpec((1, 1, 3 * tq), lambda h, i: (h, 0, 0)),
                  pl.BlockSpec((4, DIFF_HEAD_DIM), lambda h, i: (0, 0)),
                  pl.BlockSpec((1, LANES), lambda h, i: (0, 0))],
        out_specs=pl.BlockSpec((tq, LANES), lambda h, i: (i, h)),
        out_shape=jax.ShapeDtypeStruct((T, H * LANES), BF16),
        scratch_shapes=[pltpu.VMEM((tq, LANES), F32)] * 6 + [pltpu.VMEM((2, 2, tq, tq), F32), pltpu.VMEM((2, tq, tq), F32)],
        compiler_params=_params(("arbitrary", "arbitrary"), 40),
        name="attn",
    )(qkv, qkv, qkv, bias_band, lam_vecs, subln_w.reshape(1, LANES))


GDN_CHUNKS_PER_STEP = 4


def _gdn_kernel(g4_ref, gab_ref, convw_ref, alog_ref, dtb_ref, normw_ref, o_ref,
                s_ref, xbuf_ref):
    C = CHUNK
    H = GDN_HEADS
    R = GDN_CHUNKS_PER_STEP * C
    n = pl.program_id(0)

    @pl.when(n == 0)
    def _():
        s_ref[...] = jnp.zeros(s_ref.shape, F32)
        xbuf_ref[...] = jnp.zeros(xbuf_ref.shape, F32)

    gab = gab_ref[0]
    sp = gab + dtb_ref[...]
    softplus = jnp.maximum(sp, 0.0) + jnp.log(1.0 + jnp.exp(-jnp.abs(sp)))
    g = -jnp.exp(alog_ref[...]) * softplus
    beta_all = _sigmoid(gab)
    row_in_chunk = lax.broadcasted_iota(I32, (R, LANES), 0) % C
    gc = g
    for sh in (1, 2, 4, 8, 16, 32):
        gc = gc + jnp.where(row_in_chunk >= sh, pltpu.roll(gc, sh, 0), 0.0)
    gc_t = gc.T

    ri = lax.broadcasted_iota(I32, (C, C), 0)
    ci = lax.broadcasted_iota(I32, (C, C), 1)
    tri = ri >= ci
    strict = ri > ci
    eye = (ri == ci).astype(F32)

    for c in range(3 * H):
        xbuf_ref[c, 8:8 + R, :] = g4_ref[c]

    def conv(c):
        acc = xbuf_ref[c, 8 - (CONV_WIDTH - 1):8 - (CONV_WIDTH - 1) + R, :] * convw_ref[0, c:c + 1, :]
        for j in range(1, CONV_WIDTH):
            lo = 8 - (CONV_WIDTH - 1) + j
            acc = acc + xbuf_ref[c, lo:lo + R, :] * convw_ref[j, c:c + 1, :]
        return _silu(acc)

    def l2n(x):
        return x * lax.rsqrt(jnp.sum(x * x, axis=1, keepdims=True) + NORM_EPS)

    scale = GDN_HEAD_DIM ** -0.5
    heads = range(H)
    chunks = range(GDN_CHUNKS_PER_STEP)
    units = [(c, h) for c in chunks for h in heads]
    rows = lambda x, c: x[c * C:(c + 1) * C]
    q_all = [l2n(conv(h)) * scale for h in heads]
    k_all = [l2n(conv(H + h)) for h in heads]
    v_all = [conv(2 * H + h) for h in heads]
    q = {(c, h): rows(q_all[h], c) for c, h in units}
    k = {(c, h): rows(k_all[h], c) for c, h in units}
    v = {(c, h): rows(v_all[h], c) for c, h in units}
    beta = {(c, h): rows(beta_all, c)[:, 8 + h:9 + h] for c, h in units}
    gcol = {(c, h): rows(gc, c)[:, h:h + 1] for c, h in units}
    glast = {(c, h): gc[(c + 1) * C - 1:(c + 1) * C, h:h + 1] for c, h in units}
    eg = {u: jnp.exp(gcol[u]) for u in units}
    decay = {(c, h): jnp.exp(jnp.where(tri, gcol[c, h] - gc_t[h:h + 1, c * C:(c + 1) * C], -jnp.inf))
             for c, h in units}
    kb = {u: k[u] * beta[u] for u in units}
    prod = {u: _dot_nt(jnp.concatenate([kb[u], q[u]], axis=0).astype(BF16), k[u].astype(BF16))
            for u in units}
    lmat = {u: jnp.where(strict, prod[u][:C] * decay[u], 0.0) for u in units}
    aqk = {u: jnp.where(tri, prod[u][C:] * decay[u], 0.0).astype(BF16) for u in units}
    tm = {u: eye - lmat[u] for u in units}
    pw = lmat
    for _ in range(5):
        pwb = {u: pw[u].astype(BF16) for u in units}
        pw = {u: _dot(pwb[u], pwb[u]) for u in units}
        tm = {u: tm[u] + _dot(tm[u].astype(BF16), pw[u].astype(BF16)) for u in units}
    uw = {u: _dot(tm[u].astype(BF16),
                  jnp.concatenate([v[u] * beta[u], kb[u] * eg[u]], axis=1).astype(BF16)) for u in units}
    wq = {u: jnp.concatenate([uw[u][:, LANES:], q[u] * eg[u]], axis=0).astype(BF16) for u in units}
    k_dec = {u: (k[u] * jnp.exp(glast[u] - gcol[u])).astype(BF16) for u in units}
    s_cur = [s_ref[h] for h in heads]
    o = {}
    for c in chunks:
        sd = [_dot(wq[c, h], s_cur[h].astype(BF16)) for h in heads]
        vnb = [(uw[c, h][:, :LANES] - sd[h][:C]).astype(BF16) for h in heads]
        for h in heads:
            o[c, h] = sd[h][C:] + _dot(aqk[c, h], vnb[h])
        s_cur = [s_cur[h] * jnp.exp(glast[c, h]) + lax.dot_general(
            k_dec[c, h], vnb[h], (((0,), (0,)), ((), ())), preferred_element_type=F32) for h in heads]
    for h in heads:
        s_ref[h] = s_cur[h]
    for c, h in units:
        on = o[c, h] * lax.rsqrt(jnp.mean(o[c, h] * o[c, h], axis=1, keepdims=True) + NORM_EPS) * normw_ref[...]
        z = g4_ref[3 * H + h, c * C:(c + 1) * C, :]
        o_ref[c * C:(c + 1) * C, h * LANES:(h + 1) * LANES] = (on * _silu(z)).astype(o_ref.dtype)

    for c in range(3 * H):
        xbuf_ref[c, 0:8, :] = g4_ref[c, R - 8:R, :]


def _gdn(g4, gab, conv_w, a_log, dt_bias, norm_w):
    nb, T, _ = g4.shape
    H = GDN_HEADS
    C = GDN_CHUNKS_PER_STEP * CHUNK
    pad = lambda v, off: jnp.zeros((1, LANES), F32).at[0, off:off + H].set(v.astype(F32))
    return pl.pallas_call(
        _gdn_kernel,
        grid=(T // C,),
        in_specs=[pl.BlockSpec((4 * H, C, LANES), lambda n: (0, n, 0)),
                  pl.BlockSpec((1, C, LANES), lambda n: (0, n, 0)),
                  pl.BlockSpec((CONV_WIDTH, 3 * H, LANES), lambda n: (0, 0, 0)),
                  pl.BlockSpec((1, LANES), lambda n: (0, 0)),
                  pl.BlockSpec((1, LANES), lambda n: (0, 0)),
                  pl.BlockSpec((1, LANES), lambda n: (0, 0))],
        out_specs=pl.BlockSpec((C, H * LANES), lambda n: (n, 0)),
        out_shape=jax.ShapeDtypeStruct((T, H * LANES), BF16),
        scratch_shapes=[pltpu.VMEM((H, GDN_HEAD_DIM, GDN_HEAD_DIM), F32),
                        pltpu.VMEM((3 * H, 8 + C, LANES), F32)],
        compiler_params=_params(("arbitrary",), 32),
        name="gdn",
    )(g4, gab, conv_w.reshape(CONV_WIDTH, 3 * H, LANES), pad(a_log, 0), pad(dt_bias, 0),
      norm_w.reshape(1, LANES).astype(F32))


GATE_COL = 7 * 1024 + 2 * GDN_HEADS


def _merge_kernel(ya_ref, yb_ref, x_ref, wa_ref, wb_ref, wga_ref, wgb_ref, o_ref,
                  wa_s, wb_s, wga_s, wgb_s):
    @pl.when(pl.program_id(1) == 0)
    def _():
        wa_s[...] = wa_ref[...].astype(BF16)
        wb_s[...] = wb_ref[...].astype(BF16)
        wga_s[...] = wga_ref[0].astype(BF16)
        wgb_s[...] = wgb_ref[0].astype(BF16)

    x = x_ref[...]
    a = _dot(ya_ref[...], wa_s[...])
    b = _dot(yb_ref[...], wb_s[...])
    ga = _sigmoid(_dot_nt(x, wga_s[...]))
    gb = _sigmoid(_dot_nt(x, wgb_s[...]))
    o_ref[...] = (ga * a + gb * b).astype(o_ref.dtype)


def _merge(ya, yb, xb, w_a, w_b, w_in_t):
    T, D = xb.shape
    W = ya.shape[1]
    tm = min(512, T)
    tn = 512
    wspec = lambda rows: pl.BlockSpec((rows, tn), lambda n, m: (0, n))
    gspec = lambda off: pl.BlockSpec((pl.Element(1), pl.Element(tn), pl.Element(D)),
                                     lambda n, m: (0, pl.multiple_of(GATE_COL + off + n * tn, 16), 0))
    return pl.pallas_call(
        _merge_kernel,
        grid=(D // tn, T // tm),
        in_specs=[pl.BlockSpec((tm, W), lambda n, m: (m, 0)),
                  pl.BlockSpec((tm, W), lambda n, m: (m, 0)),
                  pl.BlockSpec((tm, D), lambda n, m: (m, 0)),
                  wspec(W), wspec(W), gspec(0), gspec(D)],
        out_specs=pl.BlockSpec((tm, tn), lambda n, m: (m, n)),
        out_shape=jax.ShapeDtypeStruct((T, D), BF16),
        scratch_shapes=[pltpu.VMEM((W, tn), BF16), pltpu.VMEM((W, tn), BF16),
                        pltpu.VMEM((tn, D), BF16), pltpu.VMEM((tn, D), BF16)],
        compiler_params=_params(("arbitrary", "arbitrary"), 48),
        name="merge",
    )(ya, yb, xb, w_a, w_b, w_in_t, w_in_t)


def _layer_norm(x, g, b):
    mu = jnp.mean(x, axis=1, keepdims=True)
    xc = x - mu
    var = jnp.mean(xc * xc, axis=1, keepdims=True)
    return xc * lax.rsqrt(var + LN_EPS) * g + b


def _outln_kernel(mg_ref, w_ref, x_ref, g_ref, b_ref, o_ref, ob_ref, op_ref):
    h = _dot(mg_ref[...], w_ref[...])
    y = _layer_norm(DEEPNORM_ALPHA * x_ref[...] + h, g_ref[...], b_ref[...])
    o_ref[...] = y
    ob_ref[...] = y.astype(BF16)
    _store_row_tiles(op_ref, _pack_halves(y))


def _outln(merged, w_out_bf, x, g, b):
    T, D = x.shape
    tm = min(512, T)
    row = lambda: pl.BlockSpec((tm, D), lambda m: (m, 0))
    vec = lambda: pl.BlockSpec((1, D), lambda m: (0, 0))
    return pl.pallas_call(
        _outln_kernel,
        grid=(T // tm,),
        in_specs=[row(), pl.BlockSpec((D, D), lambda m: (0, 0)), row(), vec(), vec()],
        out_specs=[row(), row(), pl.BlockSpec((tm,) + ROW_TILE, lambda m: (m, 0, 0))],
        out_shape=[jax.ShapeDtypeStruct((T, D), F32), jax.ShapeDtypeStruct((T, D), BF16),
                   jax.ShapeDtypeStruct((T,) + ROW_TILE, jnp.uint32)],
        compiler_params=_params(("arbitrary",), 48),
        name="outln",
    )(merged, w_out_bf, x, g.reshape(1, D), b.reshape(1, D))


def _router_kernel(x_ref, wr_ref, bias_ref, eidx_ref, rank_ref, gate_ref, cnt_ref,
                   run_ref, tri_ref, *, tm):
    G = N_GROUPS
    P = N_EXPERTS // N_GROUPS
    i = pl.program_id(0)

    @pl.when(i == 0)
    def _():
        run_ref[...] = jnp.zeros(run_ref.shape, F32)
        r = lax.broadcasted_iota(I32, (tm, tm), 0)
        c = lax.broadcasted_iota(I32, (tm, tm), 1)
        tri_ref[...] = jnp.where(r < c, 1.0, 0.0).astype(BF16)

    w3 = _split3(wr_ref[...])
    x3 = _split3(x_ref[...])
    logits = _dot_nt(w3[1], x3[1])
    for a, b in ((0, 2), (2, 0), (0, 1), (1, 0), (0, 0)):
        logits = logits + _dot_nt(w3[a], x3[b])
    scores = _sigmoid(logits)
    choice = scores + bias_ref[...]
    c3 = choice.reshape(G, P, tm)
    s3 = scores.reshape(G, P, tm)
    j_iota = lax.broadcasted_iota(I32, (G, P, tm), 1)
    g_iota = lax.broadcasted_iota(I32, (G, 1, tm), 0)
    e_iota = lax.broadcasted_iota(I32, (G, P, tm), 0) * P + j_iota
    ninf = -jnp.inf

    def amax3(v, idx, big):
        m = jnp.max(jnp.max(v, axis=1, keepdims=True), axis=0, keepdims=True)
        f = jnp.min(jnp.min(jnp.where(v == m, idx, big), axis=1, keepdims=True), axis=0, keepdims=True)
        return m, f

    m1 = jnp.max(c3, axis=1, keepdims=True)
    f1 = jnp.min(jnp.where(c3 == m1, j_iota, P), axis=1, keepdims=True)
    m2 = jnp.max(jnp.where(j_iota == f1, ninf, c3), axis=1, keepdims=True)
    gsc = m1 + m2
    gsel = jnp.zeros((G, 1, tm), jnp.bool_)
    for _ in range(TOPK_GROUPS):
        _, f = amax3(gsc, g_iota, G)
        hit = g_iota == f
        gsel = jnp.logical_or(gsel, hit)
        gsc = jnp.where(hit, ninf, gsc)
    cm = jnp.where(gsel, c3, ninf)

    picks = []
    wts = []
    msel = jnp.zeros((G, P, tm), F32)
    for _ in range(TOP_K):
        _, f = amax3(cm, e_iota, N_EXPERTS)
        hit = e_iota == f
        wts.append(jnp.sum(jnp.sum(jnp.where(hit, s3, 0.0), axis=1, keepdims=True), axis=0, keepdims=True))
        picks.append(f)
        msel = jnp.where(hit, 1.0, msel)
        cm = jnp.where(hit, ninf, cm)
    wsum = wts[0]
    for w in wts[1:]:
        wsum = wsum + w

    m2d = msel.reshape(N_EXPERTS, tm)
    before = run_ref[...] + _dot(m2d.astype(BF16), tri_ref[...])
    b3 = before.reshape(G, P, tm)
    for k in range(TOP_K):
        hit = e_iota == picks[k]
        rk = jnp.sum(jnp.sum(jnp.where(hit, b3, 0.0), axis=1, keepdims=True), axis=0, keepdims=True)
        eidx_ref[k:k + 1, :] = picks[k].reshape(1, tm)
        rank_ref[k:k + 1, :] = rk.reshape(1, tm).astype(I32)
        gate_ref[k:k + 1, :] = (wts[k] / wsum * ROUTED_SCALE).reshape(1, tm)
    run_ref[...] = run_ref[...] + jnp.sum(m2d, axis=1, keepdims=True)
    cnt_ref[...] = run_ref[...].astype(I32)


def _router(x1, w_router, router_bias):
    T, D = x1.shape
    tm = min(512, T)
    E = N_EXPERTS
    tok = lambda: pl.BlockSpec((TOP_K, tm), lambda i: (0, i))
    return pl.pallas_call(
        functools.partial(_router_kernel, tm=tm),
        grid=(T // tm,),
        in_specs=[pl.BlockSpec((tm, D), lambda i: (i, 0)),
                  pl.BlockSpec((E, D), lambda i: (0, 0)),
                  pl.BlockSpec((E, 1), lambda i: (0, 0))],
        out_specs=[tok(), tok(), tok(), pl.BlockSpec((E, 1), lambda i: (0, 0))],
        out_shape=[jax.ShapeDtypeStruct((TOP_K, T), I32), jax.ShapeDtypeStruct((TOP_K, T), I32),
                   jax.ShapeDtypeStruct((TOP_K, T), F32), jax.ShapeDtypeStruct((E, 1), I32)],
        scratch_shapes=[pltpu.VMEM((E, 1), F32), pltpu.VMEM((tm, tm), BF16)],
        compiler_params=_params(("arbitrary",), 32),
        name="router",
    )(x1, w_router.astype(F32).T, router_bias.astype(F32).reshape(E, 1))


def _dest_kernel(starts_ref, eidx_ref, rank_ref, o_ref):
    e = eidx_ref[...]
    acc = rank_ref[...]
    for j in range(N_EXPERTS):
        acc = acc + jnp.where(e == j, starts_ref[j], 0)
    o_ref[...] = acc


def _dest(starts, eidx, rank):
    full = lambda: pl.BlockSpec(eidx.shape, lambda i, s: (0, 0))
    return pl.pallas_call(
        _dest_kernel,
        grid_spec=pltpu.PrefetchScalarGridSpec(num_scalar_prefetch=1, grid=(1,),
                                               in_specs=[full(), full()], out_specs=full()),
        out_shape=jax.ShapeDtypeStruct(eidx.shape, I32),
        compiler_params=_params(("arbitrary",), 16),
        name="dest",
    )(starts, eidx, rank)


def _dispatch_kernel(dest_ref, lo_ref, hi_ref, nvis_ref, x_ref, x1_ref, xb_ref, wsg_ref, wsu_ref, wsd_ref,
                     xs_hbm, base_ref, zero_ref, sem, *, tm, nblk):
    i = pl.program_id(0)

    def row_copy(t, k):
        return pltpu.make_async_copy(x_ref.at[pl.ds(t, 1)], xs_hbm.at[pl.ds(dest_ref[k, t], 1)], sem)

    def issue(t, carry):
        for k in range(TOP_K):
            row_copy(t, k).start(priority=k % 2)
        return carry

    lax.fori_loop(0, tm, issue, 0)
    xb = xb_ref[...]
    hidden = _silu(_dot(xb, wsg_ref[...])) * _dot(xb, wsu_ref[...])
    base_ref[...] = DEEPNORM_ALPHA * x1_ref[...] + _dot(hidden.astype(BF16), wsd_ref[...])
    for k in range(TOP_K):
        pltpu.make_async_copy(x_ref, xs_hbm.at[pl.ds(0, tm)], sem).wait()

    @pl.when(i == pl.num_programs(0) - 1)
    def _():
        zero_ref[...] = jnp.zeros(zero_ref.shape, zero_ref.dtype)

        def for_each_fill(fn):
            def per_expert(e, carry):
                n = hi_ref[e] - lo_ref[e]
                off = lo_ref[e]
                size = GMM_ROWS // 2
                while size >= 1:
                    @pl.when((n & size) != 0)
                    def _(off=off, size=size):
                        fn(pltpu.make_async_copy(zero_ref.at[pl.ds(0, size)], xs_hbm.at[pl.ds(off, size)], sem))
                    off = off + (n & size)
                    size //= 2
                return carry

            def per_block(b, carry):
                start = pl.multiple_of(b * GMM_ROWS, GMM_ROWS)
                fn(pltpu.make_async_copy(zero_ref, xs_hbm.at[pl.ds(start, GMM_ROWS)], sem))
                return carry

            lax.fori_loop(0, N_EXPERTS, per_expert, 0)
            lax.fori_loop(nvis_ref[0], nblk, per_block, 0)

        for_each_fill(lambda c: c.start())
        for_each_fill(lambda c: c.wait())


def _dispatch(dest, x1p, pad_lo, pad_hi, nvis, nblk, x1, x1b, wsg, wsu, wsd):
    T, D = x1.shape
    FF = wsg.shape[1]
    tm = min(256, T)
    smem = lambda: pl.BlockSpec(memory_space=pltpu.SMEM)
    row = lambda: pl.BlockSpec((tm, D), lambda i: (i, 0))
    return pl.pallas_call(
        functools.partial(_dispatch_kernel, tm=tm, nblk=nblk),
        grid=(T // tm,),
        in_specs=[pl.BlockSpec((TOP_K, tm), lambda i: (0, i), memory_space=pltpu.SMEM),
                  smem(), smem(), smem(),
                  pl.BlockSpec((tm,) + ROW_TILE, lambda i: (i, 0, 0)),
                  row(), row(),
                  pl.BlockSpec((D, FF), lambda i: (0, 0)),
                  pl.BlockSpec((D, FF), lambda i: (0, 0)),
                  pl.BlockSpec((FF, D), lambda i: (0, 0))],
        out_specs=[pl.BlockSpec(memory_space=pl.ANY), row()],
        out_shape=[jax.ShapeDtypeStruct((nblk * GMM_ROWS,) + ROW_TILE, x1p.dtype),
                   jax.ShapeDtypeStruct((T, D), F32)],
        scratch_shapes=[pltpu.VMEM((GMM_ROWS,) + ROW_TILE, x1p.dtype), pltpu.SemaphoreType.DMA(())],
        compiler_params=_params(("arbitrary",), 40),
        name="dispatch",
    )(dest, pad_lo, pad_hi, nvis, x1p, x1, x1b, wsg, wsu, wsd)


def _gmm_kernel(vblk, vexp, vchg, vslot, vnext, nvis, xs_ref, wg_hbm, wu_hbm, wd_hbm, o_ref,
                wg_buf, wu_buf, wd_buf, wg_s, wu_s, wd_s, sem):
    i = pl.program_id(0)
    e = vexp[i]

    def weight_copies(expert, slot):
        return [pltpu.make_async_copy(w_hbm.at[0, expert], buf.at[slot], sem.at[slot, j])
                for j, (w_hbm, buf) in enumerate(((wg_hbm, wg_buf), (wu_hbm, wu_buf), (wd_hbm, wd_buf)))]

    @pl.when(i == 0)
    def _():
        for c in weight_copies(e, vslot[0]):
            c.start(priority=1)

    @pl.when(vchg[i] == 1)
    def _():
        slot = vslot[i]
        for c in weight_copies(e, slot):
            c.wait()

        @pl.when(vnext[i] >= 0)
        def _():
            for c in weight_copies(vnext[i], 1 - slot):
                c.start(priority=1)

        wg_s[...] = wg_buf[slot].astype(BF16)
        wu_s[...] = wu_buf[slot].astype(BF16)
        wd_s[...] = wd_buf[slot].astype(BF16)

    @pl.when(i < nvis[0])
    def _():
        x = _unpack_halves(_load_row_tiles(xs_ref)).astype(BF16)
        g = _dot(x, wg_s[...])
        u = _dot(x, wu_s[...])
        _store_row_tiles(o_ref, _pack_halves(_dot((_silu(g) * u).astype(BF16), wd_s[...])))

    @pl.when(i >= nvis[0])
    def _():
        o_ref[...] = jnp.zeros(o_ref.shape, o_ref.dtype)


def _moe_blocks(n_assign):
    return n_assign // GMM_ROWS + N_EXPERTS


def _moe_layout(counts, nblk):
    E = N_EXPERTS
    padded = (counts + (GMM_ROWS - 1)) // GMM_ROWS * GMM_ROWS
    pend = jnp.cumsum(padded).astype(I32)
    pstart = pend - padded
    nvis = pend[-1] // GMM_ROWS
    count_le = lambda table, x: jnp.sum((table[None, :] <= x[:, None]).astype(I32), axis=1)
    take = lambda table, idx: jnp.sum(jnp.where(idx[:, None] == jnp.arange(table.shape[0], dtype=I32)[None, :],
                                                table[None, :], 0), axis=1)
    vblk = jnp.minimum(jnp.arange(nblk, dtype=I32), nvis - 1)
    vexp = jnp.minimum(count_le(pend, vblk * GMM_ROWS), E - 1)
    vchg = jnp.concatenate([jnp.ones((1,), I32), (vexp[1:] != vexp[:-1]).astype(I32)])
    vslot = (jnp.cumsum(vchg) - 1) % 2
    pos = jnp.where(vchg == 1, jnp.arange(nblk, dtype=I32), nblk)
    nxt = jnp.concatenate([lax.cummin(pos[::-1])[::-1][1:], jnp.full((1,), nblk, I32)])
    vnext = jnp.where(nxt < nblk, take(vexp, jnp.minimum(nxt, nblk - 1)), -1)
    sched = (vblk, vexp, vchg, vslot.astype(I32), vnext.astype(I32), nvis.reshape(1).astype(I32))
    return pstart, pstart + counts, pend, sched


def _gmm(xs, sched, w_gate, w_up, w_down):
    R = xs.shape[0]
    D = D_MODEL
    FF = EXPERT_FF
    rows = lambda: pl.BlockSpec((GMM_ROWS,) + ROW_TILE, lambda i, vb, *_: (vb[i], 0, 0))
    hbm = lambda: pl.BlockSpec(memory_space=pl.ANY)
    grid_spec = pltpu.PrefetchScalarGridSpec(
        num_scalar_prefetch=len(sched),
        grid=(R // GMM_ROWS,),
        in_specs=[rows(), hbm(), hbm(), hbm()],
        out_specs=pl.BlockSpec((GMM_ROWS,) + ROW_TILE, lambda i, *_: (i, 0, 0)),
        scratch_shapes=[pltpu.VMEM((2, D, FF), F32), pltpu.VMEM((2, D, FF), F32), pltpu.VMEM((2, FF, D), F32),
                        pltpu.VMEM((D, FF), BF16), pltpu.VMEM((D, FF), BF16), pltpu.VMEM((FF, D), BF16),
                        pltpu.SemaphoreType.DMA((2, 3))],
    )
    return pl.pallas_call(
        _gmm_kernel,
        grid_spec=grid_spec,
        out_shape=jax.ShapeDtypeStruct((R,) + ROW_TILE, jnp.uint32),
        compiler_params=_params(("arbitrary",), 56),
        name="gmm",
    )(*sched, xs, w_gate, w_up, w_down)


def _combine_kernel(dest_ref, next_ref, ys_hbm, gate_ref, base_ref, g_ref, b_ref, o_ref, buf_ref, sem, *, tm):
    i = pl.program_id(0)
    last = pl.num_programs(0) - 1
    slot = lax.rem(i, 2)

    def row_copy(idx_ref, s, t, k):
        return pltpu.make_async_copy(ys_hbm.at[pl.ds(idx_ref[k, t], 1)],
                                     buf_ref.at[s, k, pl.ds(t, 1)], sem.at[s])

    def rolled(fn):
        def body(t, carry):
            for k in range(TOP_K):
                fn(t, k)
            return carry
        lax.fori_loop(0, tm, body, 0)

    @pl.when(i == 0)
    def _():
        rolled(lambda t, k: row_copy(dest_ref, slot, t, k).start(priority=k % 2))

    def wait_slot(s):
        for k in range(TOP_K):
            pltpu.make_async_copy(ys_hbm.at[pl.ds(0, tm)], buf_ref.at[s, k], sem.at[s]).wait()

    wait_slot(slot)

    for t in range(tm):
        for k in range(TOP_K):
            row_copy(next_ref, 1 - slot, t, k).start(priority=k % 2)
    acc = base_ref[...]
    gates = gate_ref[...]
    for k in range(TOP_K):
        acc = acc + gates[:, k:k + 1] * _unpack_halves(_load_row_tiles(buf_ref.at[slot, k]))
    o_ref[...] = _layer_norm(acc, g_ref[...], b_ref[...])

    @pl.when(i == last)
    def _():
        wait_slot(1 - slot)


def _combine(dest, ys, gates_t, base, g, b):
    T, D = base.shape
    tm = min(256, T)
    row = lambda: pl.BlockSpec((tm, D), lambda i: (i, 0))
    vec = lambda: pl.BlockSpec((1, D), lambda i: (0, 0))
    nt = T // tm
    return pl.pallas_call(
        functools.partial(_combine_kernel, tm=tm),
        grid=(nt,),
        in_specs=[pl.BlockSpec((TOP_K, tm), lambda i: (0, i), memory_space=pltpu.SMEM),
                  pl.BlockSpec((TOP_K, tm), lambda i: (0, jnp.minimum(i + 1, nt - 1)), memory_space=pltpu.SMEM),
                  pl.BlockSpec(memory_space=pl.ANY),
                  pl.BlockSpec((tm, TOP_K), lambda i: (i, 0)),
                  row(), vec(), vec()],
        out_specs=row(),
        out_shape=jax.ShapeDtypeStruct((T, D), F32),
        scratch_shapes=[pltpu.VMEM((2, TOP_K, tm) + ROW_TILE, jnp.uint32), pltpu.SemaphoreType.DMA((2,))],
        compiler_params=_params(("arbitrary",), 40),
        name="combine",
    )(dest, dest, ys, gates_t, base, g.reshape(1, D), b.reshape(1, D))


def kernel(x, w_in, conv_w, gdn_a_log, gdn_dt_bias, gdn_norm_w, diff_lambda, diff_subln_w, rel_bias_table,
           w_branch_a, w_branch_b, w_out, ln1_g, ln1_b, w_router, router_bias, w_gate, w_up, w_down,
           ws_gate, ws_up, ws_down, ln2_g, ln2_b):
    B, T, D = x.shape
    assert B == 1 and D == D_MODEL
    x2 = x.reshape(T, D)
    xb = x2.astype(BF16)
    w_in_t = jnp.swapaxes(w_in, 1, 2)
    qkv = _proj(xb, w_in_t, 0, 3, 1024, BF16)
    g4 = _proj(xb, w_in_t, 3, 4, 1024, F32)
    gab = _proj(xb, w_in_t, 56, 1, LANES, F32)

    tq = min(512, T)
    ya = _attn(qkv, _attn_bias_band(rel_bias_table.astype(F32), tq), diff_lambda[0].astype(F32),
               diff_subln_w[0].astype(F32), tq)
    yb = _gdn(g4, gab, conv_w[0].astype(F32), gdn_a_log[0], gdn_dt_bias[0], gdn_norm_w[0])
    merged = _merge(ya, yb, xb, w_branch_a[0], w_branch_b[0], w_in_t)
    x1, x1b, x1p = _outln(merged, w_out[0].astype(BF16), x2, ln1_g[0], ln1_b[0])

    eidx, rank, gates, counts = _router(x1, w_router[0], router_bias[0])
    nblk = _moe_blocks(T * TOP_K)
    starts, pad_lo, pad_hi, sched = _moe_layout(counts.reshape(N_EXPERTS), nblk)
    dest = _dest(starts, eidx, rank)
    xs, base = _dispatch(dest, x1p, pad_lo, pad_hi, sched[-1], nblk, x1, x1b, ws_gate[0].astype(BF16),
                         ws_up[0].astype(BF16), ws_down[0].astype(BF16))
    ys = _gmm(xs, sched, w_gate, w_up, w_down)
    out = _combine(dest, ys, gates.T, base, ln2_g[0], ln2_b[0])
    return out.reshape(B, T, D)
```

```python
import functools
import math

import jax
import jax.numpy as jnp
from jax import lax
from jax.experimental import pallas as pl
from jax.experimental.pallas import tpu as pltpu

F32 = jnp.float32
BF16 = jnp.bfloat16
I32 = jnp.int32

D_MODEL = 2048
CHUNK = 64
DIFF_HEADS = 8
DIFF_HEAD_DIM = 64
GDN_HEADS = 8
GDN_HEAD_DIM = 128
CONV_WIDTH = 4
REL_BUCKETS = 32
REL_MAX_DISTANCE = 128
N_EXPERTS = 64
TOP_K = 8
N_GROUPS = 8
TOPK_GROUPS = 4
EXPERT_FF = 512
ROUTED_SCALE = 2.5
DEPTH = 1
DEEPNORM_ALPHA = (2 * DEPTH) ** 0.25
LN_EPS = 1e-5
NORM_EPS = 1e-6
LAMBDA_INIT = 0.8 - 0.6 * math.exp(-0.3 * 0)

LANES = 128
MIB = 1024 * 1024
NEG_BIG = -1e30
LOG2E = math.log2(math.e)
GMM_ROWS = 256
ROW_TILE = (D_MODEL // 2 // LANES, LANES)


def _params(semantics, vmem_mib):
    return pltpu.CompilerParams(dimension_semantics=semantics, vmem_limit_bytes=vmem_mib * MIB)


def _sigmoid(x):
    return 1.0 / (1.0 + jnp.exp(-x))


def _silu(x):
    return x * _sigmoid(x)


def _dot(a, b):
    return jnp.dot(a, b, preferred_element_type=F32)


def _dot_nt(a, b):
    return lax.dot_general(a, b, (((1,), (1,)), ((), ())), preferred_element_type=F32)


def _pack_halves(y):
    n = y.shape[1] // 2
    bits = lambda v: lax.bitcast_convert_type(v.astype(BF16).astype(F32), jnp.uint32)
    return (bits(y[:, :n]) >> 16) | (bits(y[:, n:]) & jnp.uint32(0xFFFF0000))


def _unpack_halves(p):
    lo = lax.bitcast_convert_type(p << 16, F32)
    hi = lax.bitcast_convert_type(p & jnp.uint32(0xFFFF0000), F32)
    return jnp.concatenate([lo, hi], axis=1)


def _store_row_tiles(ref, packed):
    ref[...] = packed.reshape(ref.shape)


def _load_row_tiles(ref):
    return ref[...].reshape(ref.shape[0], -1)


def _split3(a):
    hi = a.astype(BF16)
    r1 = a - hi.astype(F32)
    mid = r1.astype(BF16)
    lo = (r1 - mid.astype(F32)).astype(BF16)
    return hi, mid, lo


def _proj_kernel(x_ref, w_ref, o_ref, wbf_ref, *, tn):
    @pl.when(pl.program_id(1) == 0)
    def _():
        wbf_ref[...] = w_ref[...].astype(BF16)

    acc = _dot_nt(x_ref[...], wbf_ref[...])
    for j in range(tn // LANES):
        o_ref[j] = acc[:, j * LANES:(j + 1) * LANES].astype(o_ref.dtype)


def _proj(xb, wt, blk0, nblk, tn, odt):
    T, D = xb.shape
    tm = min(1024, T)
    return pl.pallas_call(
        functools.partial(_proj_kernel, tn=tn),
        grid=(nblk, T // tm),
        in_specs=[pl.BlockSpec((tm, D), lambda n, m: (m, 0)),
                  pl.BlockSpec((None, tn, D), lambda n, m: (0, blk0 + n, 0))],
        out_specs=pl.BlockSpec((tn // LANES, tm, LANES), lambda n, m: (n, m, 0)),
        out_shape=jax.ShapeDtypeStruct((nblk * tn // LANES, T, LANES), odt),
        scratch_shapes=[pltpu.VMEM((tn, D), BF16)],
        compiler_params=_params(("arbitrary", "arbitrary"), 48),
        name="proj",
    )(xb, wt)


def _attn_kernel(q_ref, k_ref, v_ref, band_ref, lam_ref, subw_ref, o_ref,
                 m1, l1, a1, m2, l2, a2, s_buf, bias_s, *, tq):
    qi = pl.program_id(1)

    @pl.when(qi == 0)
    def _():
        toep = pltpu.roll(jnp.broadcast_to(band_ref[0], (tq, 3 * tq)), 0, 1, stride=1, stride_axis=0)
        qpos = lax.broadcasted_iota(I32, (tq, tq), 0)
        kpos = lax.broadcasted_iota(I32, (tq, tq), 1)
        visible = (kpos // CHUNK) <= (qpos // CHUNK)
        bias_s[0] = jnp.where(visible, toep[:, 2 * tq:], NEG_BIG)
        bias_s[1] = toep[:, tq:2 * tq]

    q = q_ref[0].astype(F32) * (DIFF_HEAD_DIM ** -0.5 * LOG2E)
    ones = jnp.ones((tq, LANES), BF16)
    lane = lax.broadcasted_iota(I32, q.shape, 1)
    q1 = jnp.where(lane < DIFF_HEAD_DIM, q, 0.0).astype(BF16)
    q2 = jnp.where(lane >= DIFF_HEAD_DIM, q, 0.0).astype(BF16)
    for m, l, a in ((m1, l1, a1), (m2, l2, a2)):
        m[...] = jnp.full(m.shape, NEG_BIG, F32)
        l[...] = jnp.zeros(l.shape, F32)
        a[...] = jnp.zeros(a.shape, F32)

    def scores(j, slot):
        k = k_ref[0, pl.ds(pl.multiple_of(j * tq, tq), tq), :]
        s_buf[slot, 0] = _dot_nt(q1, k)
        s_buf[slot, 1] = _dot_nt(q2, k)

    def accumulate(j, slot, bias_idx):
        v = jnp.concatenate([v_ref[0, pl.ds(pl.multiple_of(j * tq, tq), tq), :], ones], axis=1)
        for i, (m, l, a) in enumerate(((m1, l1, a1), (m2, l2, a2))):
            s = s_buf[slot, i]
            if bias_idx is not None:
                s = s + bias_s[bias_idx]
            m_old = m[...]
            m_new = jnp.maximum(m_old, jnp.max(s, axis=1, keepdims=True))
            alpha = jnp.exp2(m_old - m_new)
            p = jnp.exp2(s - jnp.concatenate([m_new] * (tq // LANES), axis=1))
            pv = _dot(p.astype(BF16), v)
            l[...] = alpha * l[...] + pv[:, LANES:]
            a[...] = alpha * a[...] + pv[:, :LANES]
            m[...] = m_new

    n_far = jnp.maximum(qi - 1, 0)
    odd = lax.rem(n_far, 2)

    @pl.when(odd == 1)
    def _():
        scores(0, 1)
        scores(1, 0)
        accumulate(0, 1, None)

    @pl.when(odd == 0)
    def _():
        scores(0, 0)

    @pl.loop(0, n_far // 2)
    def _(t):
        base = odd + 2 * t
        scores(base + 1, 1)
        accumulate(base, 0, None)
        scores(base + 2, 0)
        accumulate(base + 1, 1, None)

    @pl.when(qi >= 1)
    def _():
        scores(qi, 1)
        accumulate(qi - 1, 0, 1)
        accumulate(qi, 1, 0)

    @pl.when(qi == 0)
    def _():
        accumulate(0, 0, 0)

    lf = lam_ref[...]
    lam = (jnp.exp(jnp.sum(lf[0:1] * lf[1:2], axis=1, keepdims=True))
           - jnp.exp(jnp.sum(lf[2:3] * lf[3:4], axis=1, keepdims=True)) + LAMBDA_INIT)
    o = a1[...] / l1[...] - lam * (a2[...] / l2[...])
    y = o * lax.rsqrt(jnp.mean(o * o, axis=1, keepdims=True) + NORM_EPS) * subw_ref[...] * (1.0 - LAMBDA_INIT)
    o_ref[...] = y.astype(o_ref.dtype)


def _t5_bucket(rel):
    nb = REL_BUCKETS // 2
    base = jnp.where(rel > 0, nb, 0)
    n = jnp.abs(rel)
    max_exact = nb // 2
    nf = jnp.maximum(n, 1).astype(F32)
    large = max_exact + (jnp.log(nf / max_exact) / math.log(REL_MAX_DISTANCE / max_exact)
                         * (nb - max_exact)).astype(I32)
    large = jnp.minimum(large, nb - 1)
    return base + jnp.where(n < max_exact, n, large)


def _attn_bias_band(rel_table, tq):
    span = 3 * tq
    rel = jnp.arange(span) - 2 * tq
    onehot = (_t5_bucket(rel)[:, None] == jnp.arange(REL_BUCKETS)[None, :]).astype(F32)
    far = rel_table[REL_BUCKETS // 2 - 1]
    band = (jnp.dot(onehot, rel_table, precision=lax.Precision.HIGHEST) - far).T
    return (band * LOG2E)[:, None, :]


def _attn(qkv, bias_band, lam_vecs, subln_w, tq):
    nb, T, _ = qkv.shape
    H = DIFF_HEADS
    return pl.pallas_call(
        functools.partial(_attn_kernel, tq=tq),
        grid=(H, T // tq),
        in_specs=[pl.BlockSpec((1, tq, LANES), lambda h, i: (h, i, 0)),
                  pl.BlockSpec((1, T, LANES), lambda h, i: (H + h, 0, 0)),
                  pl.BlockSpec((1, T, LANES), lambda h, i: (2 * H + h, 0, 0)),
                  pl.BlockSpec((1, 1, 3 * tq), lambda h, i: (h, 0, 0)),
                  pl.BlockSpec((4, DIFF_HEAD_DIM), lambda h, i: (0, 0)),
                  pl.BlockSpec((1, LANES), lambda h, i: (0, 0))],
        out_specs=pl.BlockSpec((tq, LANES), lambda h, i: (i, h)),
        out_shape=jax.ShapeDtypeStruct((T, H * LANES), BF16),
        scratch_shapes=[pltpu.VMEM((tq, LANES), F32)] * 6 + [pltpu.VMEM((2, 2, tq, tq), F32), pltpu.VMEM((2, tq, tq), F32)],
        compiler_params=_params(("arbitrary", "arbitrary"), 40),
        name="attn",
    )(qkv, qkv, qkv, bias_band, lam_vecs, subln_w.reshape(1, LANES))


GDN_CHUNKS_PER_STEP = 4


def _gdn_kernel(g4_ref, gab_ref, convw_ref, alog_ref, dtb_ref, normw_ref, o_ref,
                s_ref, xbuf_ref):
    C = CHUNK
    H = GDN_HEADS
    R = GDN_CHUNKS_PER_STEP * C
    n = pl.program_id(0)

    @pl.when(n == 0)
    def _():
        s_ref[...] = jnp.zeros(s_ref.shape, F32)
        xbuf_ref[...] = jnp.zeros(xbuf_ref.shape, F32)

    gab = gab_ref[0]
    sp = gab + dtb_ref[...]
    softplus = jnp.maximum(sp, 0.0) + jnp.log(1.0 + jnp.exp(-jnp.abs(sp)))
    g = -jnp.exp(alog_ref[...]) * softplus
    beta_all = _sigmoid(gab)
    row_in_chunk = lax.broadcasted_iota(I32, (R, LANES), 0) % C
    gc = g
    for sh in (1, 2, 4, 8, 16, 32):
        gc = gc + jnp.where(row_in_chunk >= sh, pltpu.roll(gc, sh, 0), 0.0)
    gc_t = gc.T

    ri = lax.broadcasted_iota(I32, (C, C), 0)
    ci = lax.broadcasted_iota(I32, (C, C), 1)
    tri = ri >= ci
    strict = ri > ci
    eye = (ri == ci).astype(F32)

    for c in range(3 * H):
        xbuf_ref[c, 8:8 + R, :] = g4_ref[c]

    def conv(c):
        acc = xbuf_ref[c, 8 - (CONV_WIDTH - 1):8 - (CONV_WIDTH - 1) + R, :] * convw_ref[0, c:c + 1, :]
        for j in range(1, CONV_WIDTH):
            lo = 8 - (CONV_WIDTH - 1) + j
            acc = acc + xbuf_ref[c, lo:lo + R, :] * convw_ref[j, c:c + 1, :]
        return _silu(acc)

    def l2n(x):
        return x * lax.rsqrt(jnp.sum(x * x, axis=1, keepdims=True) + NORM_EPS)

    scale = GDN_HEAD_DIM ** -0.5
    heads = range(H)
    chunks = range(GDN_CHUNKS_PER_STEP)
    units = [(c, h) for c in chunks for h in heads]
    rows = lambda x, c: x[c * C:(c + 1) * C]
    q_all = [l2n(conv(h)) * scale for h in heads]
    k_all = [l2n(conv(H + h)) for h in heads]
    v_all = [conv(2 * H + h) for h in heads]
    q = {(c, h): rows(q_all[h], c) for c, h in units}
    k = {(c, h): rows(k_all[h], c) for c, h in units}
    v = {(c, h): rows(v_all[h], c) for c, h in units}
    beta = {(c, h): rows(beta_all, c)[:, 8 + h:9 + h] for c, h in units}
    gcol = {(c, h): rows(gc, c)[:, h:h + 1] for c, h in units}
    glast = {(c, h): gc[(c + 1) * C - 1:(c + 1) * C, h:h + 1] for c, h in units}
    eg = {u: jnp.exp(gcol[u]) for u in units}
    decay = {(c, h): jnp.exp(jnp.where(tri, gcol[c, h] - gc_t[h:h + 1, c * C:(c + 1) * C], -jnp.inf))
             for c, h in units}
    kb = {u: k[u] * beta[u] for u in units}
    prod = {u: _dot_nt(jnp.concatenate([kb[u], q[u]], axis=0).astype(BF16), k[u].astype(BF16))
            for u in units}
    lmat = {u: jnp.where(strict, prod[u][:C] * decay[u], 0.0) for u in units}
    aqk = {u: jnp.where(tri, prod[u][C:] * decay[u], 0.0).astype(BF16) for u in units}
    tm = {u: eye - lmat[u] for u in units}
    pw = lmat
    for _ in range(5):
        pwb = {u: pw[u].astype(BF16) for u in units}
        pw = {u: _dot(pwb[u], pwb[u]) for u in units}
        tm = {u: tm[u] + _dot(tm[u].astype(BF16), pw[u].astype(BF16)) for u in units}
    uw = {u: _dot(tm[u].astype(BF16),
                  jnp.concatenate([v[u] * beta[u], kb[u] * eg[u]], axis=1).astype(BF16)) for u in units}
    wq = {u: jnp.concatenate([uw[u][:, LANES:], q[u] * eg[u]], axis=0).astype(BF16) for u in units}
    k_dec = {u: (k[u] * jnp.exp(glast[u] - gcol[u])).astype(BF16) for u in units}
    s_cur = [s_ref[h] for h in heads]
    o = {}
    for c in chunks:
        sd = [_dot(wq[c, h], s_cur[h].astype(BF16)) for h in heads]
        vnb = [(uw[c, h][:, :LANES] - sd[h][:C]).astype(BF16) for h in heads]
        for h in heads:
            o[c, h] = sd[h][C:] + _dot(aqk[c, h], vnb[h])
        s_cur = [s_cur[h] * jnp.exp(glast[c, h]) + lax.dot_general(
            k_dec[c, h], vnb[h], (((0,), (0,)), ((), ())), preferred_element_type=F32) for h in heads]
    for h in heads:
        s_ref[h] = s_cur[h]
    for c, h in units:
        on = o[c, h] * lax.rsqrt(jnp.mean(o[c, h] * o[c, h], axis=1, keepdims=True) + NORM_EPS) * normw_ref[...]
        z = g4_ref[3 * H + h, c * C:(c + 1) * C, :]
        o_ref[c * C:(c + 1) * C, h * LANES:(h + 1) * LANES] = (on * _silu(z)).astype(o_ref.dtype)

    for c in range(3 * H):
        xbuf_ref[c, 0:8, :] = g4_ref[c, R - 8:R, :]


def _gdn(g4, gab, conv_w, a_log, dt_bias, norm_w):
    nb, T, _ = g4.shape
    H = GDN_HEADS
    C = GDN_CHUNKS_PER_STEP * CHUNK
    pad = lambda v, off: jnp.zeros((1, LANES), F32).at[0, off:off + H].set(v.astype(F32))
    return pl.pallas_call(
        _gdn_kernel,
        grid=(T // C,),
        in_specs=[pl.BlockSpec((4 * H, C, LANES), lambda n: (0, n, 0)),
                  pl.BlockSpec((1, C, LANES), lambda n: (0, n, 0)),
                  pl.BlockSpec((CONV_WIDTH, 3 * H, LANES), lambda n: (0, 0, 0)),
                  pl.BlockSpec((1, LANES), lambda n: (0, 0)),
                  pl.BlockSpec((1, LANES), lambda n: (0, 0)),
                  pl.BlockSpec((1, LANES), lambda n: (0, 0))],
        out_specs=pl.BlockSpec((C, H * LANES), lambda n: (n, 0)),
        out_shape=jax.ShapeDtypeStruct((T, H * LANES), BF16),
        scratch_shapes=[pltpu.VMEM((H, GDN_HEAD_DIM, GDN_HEAD_DIM), F32),
                        pltpu.VMEM((3 * H, 8 + C, LANES), F32)],
        compiler_params=_params(("arbitrary",), 32),
        name="gdn",
    )(g4, gab, conv_w.reshape(CONV_WIDTH, 3 * H, LANES), pad(a_log, 0), pad(dt_bias, 0),
      norm_w.reshape(1, LANES).astype(F32))


GATE_COL = 7 * 1024 + 2 * GDN_HEADS


def _merge_kernel(ya_ref, yb_ref, x_ref, wa_ref, wb_ref, wga_ref, wgb_ref, o_ref,
                  wa_s, wb_s, wga_s, wgb_s):
    @pl.when(pl.program_id(1) == 0)
    def _():
        wa_s[...] = wa_ref[...].astype(BF16)
        wb_s[...] = wb_ref[...].astype(BF16)
        wga_s[...] = wga_ref[0].astype(BF16)
        wgb_s[...] = wgb_ref[0].astype(BF16)

    x = x_ref[...]
    a = _dot(ya_ref[...], wa_s[...])
    b = _dot(yb_ref[...], wb_s[...])
    ga = _sigmoid(_dot_nt(x, wga_s[...]))
    gb = _sigmoid(_dot_nt(x, wgb_s[...]))
    o_ref[...] = (ga * a + gb * b).astype(o_ref.dtype)


def _merge(ya, yb, xb, w_a, w_b, w_in_t):
    T, D = xb.shape
    W = ya.shape[1]
    tm = min(512, T)
    tn = 512
    wspec = lambda rows: pl.BlockSpec((rows, tn), lambda n, m: (0, n))
    gspec = lambda off: pl.BlockSpec((pl.Element(1), pl.Element(tn), pl.Element(D)),
                                     lambda n, m: (0, pl.multiple_of(GATE_COL + off + n * tn, 16), 0))
    return pl.pallas_call(
        _merge_kernel,
        grid=(D // tn, T // tm),
        in_specs=[pl.BlockSpec((tm, W), lambda n, m: (m, 0)),
                  pl.BlockSpec((tm, W), lambda n, m: (m, 0)),
                  pl.BlockSpec((tm, D), lambda n, m: (m, 0)),
                  wspec(W), wspec(W), gspec(0), gspec(D)],
        out_specs=pl.BlockSpec((tm, tn), lambda n, m: (m, n)),
        out_shape=jax.ShapeDtypeStruct((T, D), BF16),
        scratch_shapes=[pltpu.VMEM((W, tn), BF16), pltpu.VMEM((W, tn), BF16),
                        pltpu.VMEM((tn, D), BF16), pltpu.VMEM((tn, D), BF16)],
        compiler_params=_params(("arbitrary", "arbitrary"), 48),
        name="merge",
    )(ya, yb, xb, w_a, w_b, w_in_t, w_in_t)


def _layer_norm(x, g, b):
    mu = jnp.mean(x, axis=1, keepdims=True)
    xc = x - mu
    var = jnp.mean(xc * xc, axis=1, keepdims=True)
    return xc * lax.rsqrt(var + LN_EPS) * g + b


def _outln_kernel(mg_ref, w_ref, x_ref, g_ref, b_ref, o_ref, ob_ref, op_ref):
    h = _dot(mg_ref[...], w_ref[...])
    y = _layer_norm(DEEPNORM_ALPHA * x_ref[...] + h, g_ref[...], b_ref[...])
    o_ref[...] = y
    ob_ref[...] = y.astype(BF16)
    _store_row_tiles(op_ref, _pack_halves(y))


def _outln(merged, w_out_bf, x, g, b):
    T, D = x.shape
    tm = min(512, T)
    row = lambda: pl.BlockSpec((tm, D), lambda m: (m, 0))
    vec = lambda: pl.BlockSpec((1, D), lambda m: (0, 0))
    return pl.pallas_call(
        _outln_kernel,
        grid=(T // tm,),
        in_specs=[row(), pl.BlockSpec((D, D), lambda m: (0, 0)), row(), vec(), vec()],
        out_specs=[row(), row(), pl.BlockSpec((tm,) + ROW_TILE, lambda m: (m, 0, 0))],
        out_shape=[jax.ShapeDtypeStruct((T, D), F32), jax.ShapeDtypeStruct((T, D), BF16),
                   jax.ShapeDtypeStruct((T,) + ROW_TILE, jnp.uint32)],
        compiler_params=_params(("arbitrary",), 48),
        name="outln",
    )(merged, w_out_bf, x, g.reshape(1, D), b.reshape(1, D))


def _router_kernel(x_ref, wr_ref, bias_ref, eidx_ref, rank_ref, gate_ref, cnt_ref,
                   run_ref, tri_ref, *, tm):
    G = N_GROUPS
    P = N_EXPERTS // N_GROUPS
    i = pl.program_id(0)

    @pl.when(i == 0)
    def _():
        run_ref[...] = jnp.zeros(run_ref.shape, F32)
        r = lax.broadcasted_iota(I32, (tm, tm), 0)
        c = lax.broadcasted_iota(I32, (tm, tm), 1)
        tri_ref[...] = jnp.where(r < c, 1.0, 0.0).astype(BF16)

    E = N_EXPERTS
    wcat = jnp.concatenate(_split3(wr_ref[...]), axis=0)
    x_hi, x_mid, x_lo = _split3(x_ref[...])
    p_hi = _dot_nt(wcat, x_hi)
    p_mid = _dot_nt(wcat[:2 * E], x_mid)
    p_lo = _dot_nt(wcat[:E], x_lo)
    logits = ((p_mid[E:] + p_lo + p_hi[2 * E:]) + (p_mid[:E] + p_hi[E:2 * E])) + p_hi[:E]
    scores = _sigmoid(logits)
    choice = scores + bias_ref[...]
    c3 = choice.reshape(G, P, tm)
    s3 = scores.reshape(G, P, tm)
    j_iota = lax.broadcasted_iota(I32, (G, P, tm), 1)
    g_iota = lax.broadcasted_iota(I32, (G, 1, tm), 0)
    e_iota = lax.broadcasted_iota(I32, (G, P, tm), 0) * P + j_iota
    ninf = -jnp.inf

    def amax3(v, idx, big):
        m = jnp.max(jnp.max(v, axis=1, keepdims=True), axis=0, keepdims=True)
        f = jnp.min(jnp.min(jnp.where(v == m, idx, big), axis=1, keepdims=True), axis=0, keepdims=True)
        return m, f

    m1 = jnp.max(c3, axis=1, keepdims=True)
    f1 = jnp.min(jnp.where(c3 == m1, j_iota, P), axis=1, keepdims=True)
    m2 = jnp.max(jnp.where(j_iota == f1, ninf, c3), axis=1, keepdims=True)
    gsc = m1 + m2
    gsel = jnp.zeros((G, 1, tm), jnp.bool_)
    for _ in range(TOPK_GROUPS):
        _, f = amax3(gsc, g_iota, G)
        hit = g_iota == f
        gsel = jnp.logical_or(gsel, hit)
        gsc = jnp.where(hit, ninf, gsc)
    cm = jnp.where(gsel, c3, ninf)

    picks = []
    wts = []
    msel = jnp.zeros((G, P, tm), F32)
    for _ in range(TOP_K):
        _, f = amax3(cm, e_iota, N_EXPERTS)
        hit = e_iota == f
        wts.append(jnp.sum(jnp.sum(jnp.where(hit, s3, 0.0), axis=1, keepdims=True), axis=0, keepdims=True))
        picks.append(f)
        msel = jnp.where(hit, 1.0, msel)
        cm = jnp.where(hit, ninf, cm)
    wsum = wts[0]
    for w in wts[1:]:
        wsum = wsum + w

    m2d = msel.reshape(N_EXPERTS, tm)
    before = run_ref[...] + _dot(m2d.astype(BF16), tri_ref[...])
    b3 = before.reshape(G, P, tm)
    for k in range(TOP_K):
        hit = e_iota == picks[k]
        rk = jnp.sum(jnp.sum(jnp.where(hit, b3, 0.0), axis=1, keepdims=True), axis=0, keepdims=True)
        eidx_ref[k:k + 1, :] = picks[k].reshape(1, tm)
        rank_ref[k:k + 1, :] = rk.reshape(1, tm).astype(I32)
        gate_ref[k:k + 1, :] = (wts[k] / wsum * ROUTED_SCALE).reshape(1, tm)
    run_ref[...] = run_ref[...] + jnp.sum(m2d, axis=1, keepdims=True)
    cnt_ref[...] = run_ref[...].astype(I32)


def _router(x1, w_router, router_bias):
    T, D = x1.shape
    tm = min(512, T)
    E = N_EXPERTS
    tok = lambda: pl.BlockSpec((TOP_K, tm), lambda i: (0, i))
    return pl.pallas_call(
        functools.partial(_router_kernel, tm=tm),
        grid=(T // tm,),
        in_specs=[pl.BlockSpec((tm, D), lambda i: (i, 0)),
                  pl.BlockSpec((E, D), lambda i: (0, 0)),
                  pl.BlockSpec((E, 1), lambda i: (0, 0))],
        out_specs=[tok(), tok(), tok(), pl.BlockSpec((E, 1), lambda i: (0, 0))],
        out_shape=[jax.ShapeDtypeStruct((TOP_K, T), I32), jax.ShapeDtypeStruct((TOP_K, T), I32),
                   jax.ShapeDtypeStruct((TOP_K, T), F32), jax.ShapeDtypeStruct((E, 1), I32)],
        scratch_shapes=[pltpu.VMEM((E, 1), F32), pltpu.VMEM((tm, tm), BF16)],
        compiler_params=_params(("arbitrary",), 32),
        name="router",
    )(x1, w_router.astype(F32).T, router_bias.astype(F32).reshape(E, 1))


def _dest_kernel(starts_ref, eidx_ref, rank_ref, o_ref):
    e = eidx_ref[...]
    acc = rank_ref[...]
    for j in range(N_EXPERTS):
        acc = acc + jnp.where(e == j, starts_ref[j], 0)
    o_ref[...] = acc


def _dest(starts, eidx, rank):
    full = lambda: pl.BlockSpec(eidx.shape, lambda i, s: (0, 0))
    return pl.pallas_call(
        _dest_kernel,
        grid_spec=pltpu.PrefetchScalarGridSpec(num_scalar_prefetch=1, grid=(1,),
                                               in_specs=[full(), full()], out_specs=full()),
        out_shape=jax.ShapeDtypeStruct(eidx.shape, I32),
        compiler_params=_params(("arbitrary",), 16),
        name="dest",
    )(starts, eidx, rank)


def _dispatch_kernel(dest_ref, lo_ref, hi_ref, nvis_ref, x_ref, x1_ref, xb_ref, wsg_ref, wsu_ref, wsd_ref,
                     xs_hbm, base_ref, zero_ref, sem, *, tm, nblk):
    i = pl.program_id(0)

    def row_copy(t, k):
        return pltpu.make_async_copy(x_ref.at[pl.ds(t, 1)], xs_hbm.at[pl.ds(dest_ref[k, t], 1)], sem)

    def issue(t, carry):
        for k in range(TOP_K):
            row_copy(t, k).start(priority=k % 2)
        return carry

    lax.fori_loop(0, tm, issue, 0)
    xb = xb_ref[...]
    hidden = _silu(_dot(xb, wsg_ref[...])) * _dot(xb, wsu_ref[...])
    base_ref[...] = DEEPNORM_ALPHA * x1_ref[...] + _dot(hidden.astype(BF16), wsd_ref[...])
    for k in range(TOP_K):
        pltpu.make_async_copy(x_ref, xs_hbm.at[pl.ds(0, tm)], sem).wait()

    @pl.when(i == pl.num_programs(0) - 1)
    def _():
        zero_ref[...] = jnp.zeros(zero_ref.shape, zero_ref.dtype)

        def for_each_fill(fn):
            def per_expert(e, carry):
                n = hi_ref[e] - lo_ref[e]
                off = lo_ref[e]
                size = GMM_ROWS // 2
                while size >= 1:
                    @pl.when((n & size) != 0)
                    def _(off=off, size=size):
                        fn(pltpu.make_async_copy(zero_ref.at[pl.ds(0, size)], xs_hbm.at[pl.ds(off, size)], sem))
                    off = off + (n & size)
                    size //= 2
                return carry

            def per_block(b, carry):
                start = pl.multiple_of(b * GMM_ROWS, GMM_ROWS)
                fn(pltpu.make_async_copy(zero_ref, xs_hbm.at[pl.ds(start, GMM_ROWS)], sem))
                return carry

            lax.fori_loop(0, N_EXPERTS, per_expert, 0)
            lax.fori_loop(nvis_ref[0], nblk, per_block, 0)

        for_each_fill(lambda c: c.start())
        for_each_fill(lambda c: c.wait())


def _dispatch(dest, x1p, pad_lo, pad_hi, nvis, nblk, x1, x1b, wsg, wsu, wsd):
    T, D = x1.shape
    FF = wsg.shape[1]
    tm = min(512, T)
    smem = lambda: pl.BlockSpec(memory_space=pltpu.SMEM)
    row = lambda: pl.BlockSpec((tm, D), lambda i: (i, 0))
    return pl.pallas_call(
        functools.partial(_dispatch_kernel, tm=tm, nblk=nblk),
        grid=(T // tm,),
        in_specs=[pl.BlockSpec((TOP_K, tm), lambda i: (0, i), memory_space=pltpu.SMEM),
                  smem(), smem(), smem(),
                  pl.BlockSpec((tm,) + ROW_TILE, lambda i: (i, 0, 0)),
                  row(), row(),
                  pl.BlockSpec((D, FF), lambda i: (0, 0)),
                  pl.BlockSpec((D, FF), lambda i: (0, 0)),
                  pl.BlockSpec((FF, D), lambda i: (0, 0))],
        out_specs=[pl.BlockSpec(memory_space=pl.ANY), row()],
        out_shape=[jax.ShapeDtypeStruct((nblk * GMM_ROWS,) + ROW_TILE, x1p.dtype),
                   jax.ShapeDtypeStruct((T, D), F32)],
        scratch_shapes=[pltpu.VMEM((GMM_ROWS,) + ROW_TILE, x1p.dtype), pltpu.SemaphoreType.DMA(())],
        compiler_params=_params(("arbitrary",), 48),
        name="dispatch",
    )(dest, pad_lo, pad_hi, nvis, x1p, x1, x1b, wsg, wsu, wsd)


def _gmm_kernel(vblk, vexp, vchg, vslot, vnext, nvis, xs_ref, wg_hbm, wu_hbm, wd_hbm, o_ref,
                wg_buf, wu_buf, wd_buf, wg_s, wu_s, wd_s, sem):
    i = pl.program_id(0)
    e = vexp[i]

    def weight_copies(expert, slot):
        return [pltpu.make_async_copy(w_hbm.at[0, expert], buf.at[slot], sem.at[slot, j])
                for j, (w_hbm, buf) in enumerate(((wg_hbm, wg_buf), (wu_hbm, wu_buf), (wd_hbm, wd_buf)))]

    @pl.when(i == 0)
    def _():
        for c in weight_copies(e, vslot[0]):
            c.start(priority=1)

    @pl.when(vchg[i] == 1)
    def _():
        slot = vslot[i]
        for c in weight_copies(e, slot):
            c.wait()

        @pl.when(vnext[i] >= 0)
        def _():
            for c in weight_copies(vnext[i], 1 - slot):
                c.start(priority=1)

        wg_s[...] = wg_buf[slot].astype(BF16)
        wu_s[...] = wu_buf[slot].astype(BF16)
        wd_s[...] = wd_buf[slot].astype(BF16)

    @pl.when(i < nvis[0])
    def _():
        x = _unpack_halves(_load_row_tiles(xs_ref)).astype(BF16)
        g = _dot(x, wg_s[...])
        u = _dot(x, wu_s[...])
        _store_row_tiles(o_ref, _pack_halves(_dot((_silu(g) * u).astype(BF16), wd_s[...])))

    @pl.when(i >= nvis[0])
    def _():
        o_ref[...] = jnp.zeros(o_ref.shape, o_ref.dtype)


def _moe_blocks(n_assign):
    return n_assign // GMM_ROWS + N_EXPERTS


def _moe_layout(counts, nblk):
    E = N_EXPERTS
    padded = (counts + (GMM_ROWS - 1)) // GMM_ROWS * GMM_ROWS
    pend = jnp.cumsum(padded).astype(I32)
    pstart = pend - padded
    nvis = pend[-1] // GMM_ROWS
    count_le = lambda table, x: jnp.sum((table[None, :] <= x[:, None]).astype(I32), axis=1)
    take = lambda table, idx: jnp.sum(jnp.where(idx[:, None] == jnp.arange(table.shape[0], dtype=I32)[None, :],
                                                table[None, :], 0), axis=1)
    vblk = jnp.minimum(jnp.arange(nblk, dtype=I32), nvis - 1)
    vexp = jnp.minimum(count_le(pend, vblk * GMM_ROWS), E - 1)
    vchg = jnp.concatenate([jnp.ones((1,), I32), (vexp[1:] != vexp[:-1]).astype(I32)])
    vslot = (jnp.cumsum(vchg) - 1) % 2
    pos = jnp.where(vchg == 1, jnp.arange(nblk, dtype=I32), nblk)
    nxt = jnp.concatenate([lax.cummin(pos[::-1])[::-1][1:], jnp.full((1,), nblk, I32)])
    vnext = jnp.where(nxt < nblk, take(vexp, jnp.minimum(nxt, nblk - 1)), -1)
    sched = (vblk, vexp, vchg, vslot.astype(I32), vnext.astype(I32), nvis.reshape(1).astype(I32))
    return pstart, pstart + counts, pend, sched


def _gmm(xs, sched, w_gate, w_up, w_down):
    R = xs.shape[0]
    D = D_MODEL
    FF = EXPERT_FF
    rows = lambda: pl.BlockSpec((GMM_ROWS,) + ROW_TILE, lambda i, vb, *_: (vb[i], 0, 0))
    hbm = lambda: pl.BlockSpec(memory_space=pl.ANY)
    grid_spec = pltpu.PrefetchScalarGridSpec(
        num_scalar_prefetch=len(sched),
        grid=(R // GMM_ROWS,),
        in_specs=[rows(), hbm(), hbm(), hbm()],
        out_specs=pl.BlockSpec((GMM_ROWS,) + ROW_TILE, lambda i, *_: (i, 0, 0)),
        scratch_shapes=[pltpu.VMEM((2, D, FF), F32), pltpu.VMEM((2, D, FF), F32), pltpu.VMEM((2, FF, D), F32),
                        pltpu.VMEM((D, FF), BF16), pltpu.VMEM((D, FF), BF16), pltpu.VMEM((FF, D), BF16),
                        pltpu.SemaphoreType.DMA((2, 3))],
    )
    return pl.pallas_call(
        _gmm_kernel,
        grid_spec=grid_spec,
        out_shape=jax.ShapeDtypeStruct((R,) + ROW_TILE, jnp.uint32),
        compiler_params=_params(("arbitrary",), 56),
        name="gmm",
    )(*sched, xs, w_gate, w_up, w_down)


def _combine_kernel(dest_ref, next_ref, ys_hbm, gate_ref, base_ref, g_ref, b_ref, o_ref, buf_ref, sem, *, tm):
    i = pl.program_id(0)
    last = pl.num_programs(0) - 1
    slot = lax.rem(i, 2)

    def row_copy(idx_ref, s, t, k):
        return pltpu.make_async_copy(ys_hbm.at[pl.ds(idx_ref[k, t], 1)],
                                     buf_ref.at[s, k, pl.ds(t, 1)], sem.at[s])

    def rolled(fn):
        def body(t, carry):
            for k in range(TOP_K):
                fn(t, k)
            return carry
        lax.fori_loop(0, tm, body, 0)

    @pl.when(i == 0)
    def _():
        rolled(lambda t, k: row_copy(dest_ref, slot, t, k).start(priority=k % 2))

    def wait_slot(s):
        for k in range(TOP_K):
            pltpu.make_async_copy(ys_hbm.at[pl.ds(0, tm)], buf_ref.at[s, k], sem.at[s]).wait()

    wait_slot(slot)

    for t in range(tm):
        for k in range(TOP_K):
            row_copy(next_ref, 1 - slot, t, k).start(priority=k % 2)
    acc = base_ref[...]
    gates = gate_ref[...]
    for k in range(TOP_K):
        acc = acc + gates[:, k:k + 1] * _unpack_halves(_load_row_tiles(buf_ref.at[slot, k]))
    o_ref[...] = _layer_norm(acc, g_ref[...], b_ref[...])

    @pl.when(i == last)
    def _():
        wait_slot(1 - slot)


def _combine(dest, ys, gates_t, base, g, b):
    T, D = base.shape
    tm = min(256, T)
    row = lambda: pl.BlockSpec((tm, D), lambda i: (i, 0))
    vec = lambda: pl.BlockSpec((1, D), lambda i: (0, 0))
    nt = T // tm
    return pl.pallas_call(
        functools.partial(_combine_kernel, tm=tm),
        grid=(nt,),
        in_specs=[pl.BlockSpec((TOP_K, tm), lambda i: (0, i), memory_space=pltpu.SMEM),
                  pl.BlockSpec((TOP_K, tm), lambda i: (0, jnp.minimum(i + 1, nt - 1)), memory_space=pltpu.SMEM),
                  pl.BlockSpec(memory_space=pl.ANY),
                  pl.BlockSpec((tm, TOP_K), lambda i: (i, 0)),
                  row(), vec(), vec()],
        out_specs=row(),
        out_shape=jax.ShapeDtypeStruct((T, D), F32),
        scratch_shapes=[pltpu.VMEM((2, TOP_K, tm) + ROW_TILE, jnp.uint32), pltpu.SemaphoreType.DMA((2,))],
        compiler_params=_params(("arbitrary",), 40),
        name="combine",
    )(dest, dest, ys, gates_t, base, g.reshape(1, D), b.reshape(1, D))


def kernel(x, w_in, conv_w, gdn_a_log, gdn_dt_bias, gdn_norm_w, diff_lambda, diff_subln_w, rel_bias_table,
           w_branch_a, w_branch_b, w_out, ln1_g, ln1_b, w_router, router_bias, w_gate, w_up, w_down,
           ws_gate, ws_up, ws_down, ln2_g, ln2_b):
    B, T, D = x.shape
    assert B == 1 and D == D_MODEL
    x2 = x.reshape(T, D)
    xb = x2.astype(BF16)
    w_in_t = jnp.swapaxes(w_in, 1, 2)
    qkv = _proj(xb, w_in_t, 0, 3, 1024, BF16)
    g4 = _proj(xb, w_in_t, 3, 4, 1024, F32)
    gab = _proj(xb, w_in_t, 56, 1, LANES, F32)

    tq = min(512, T)
    ya = _attn(qkv, _attn_bias_band(rel_bias_table.astype(F32), tq), diff_lambda[0].astype(F32),
               diff_subln_w[0].astype(F32), tq)
    yb = _gdn(g4, gab, conv_w[0].astype(F32), gdn_a_log[0], gdn_dt_bias[0], gdn_norm_w[0])
    merged = _merge(ya, yb, xb, w_branch_a[0], w_branch_b[0], w_in_t)
    x1, x1b, x1p = _outln(merged, w_out[0].astype(BF16), x2, ln1_g[0], ln1_b[0])

    eidx, rank, gates, counts = _router(x1, w_router[0], router_bias[0])
    nblk = _moe_blocks(T * TOP_K)
    starts, pad_lo, pad_hi, sched = _moe_layout(counts.reshape(N_EXPERTS), nblk)
    dest = _dest(starts, eidx, rank)
    xs, base = _dispatch(dest, x1p, pad_lo, pad_hi, sched[-1], nblk, x1, x1b, ws_gate[0].astype(BF16),
                         ws_up[0].astype(BF16), ws_down[0].astype(BF16))
    ys = _gmm(xs, sched, w_gate, w_up, w_down)
    out = _combine(dest, ys, gates.T, base, ln2_g[0], ln2_b[0])
    return out.reshape(B, T, D)
```
